```python
import math
import jax
import jax.numpy as jnp
from jax import lax
import numpy as np


D_MODEL = 1024
BATCH = 8
SEQ = 8192
DEPTH = 2

CHUNK = 64
Q_BLOCK = 128
EPS = 1e-6
ML_HEADS = 4
ML_HD = 256
ML_W = ML_HEADS * ML_HD
ML_CONV = 4
GLA_HEADS = 4
GLA_DK = 64
GLA_DV = 128
GLA_WK = GLA_HEADS * GLA_DK
GLA_W = GLA_HEADS * GLA_DV
GLA_RANK = 16
GLA_TAU = 16.0
DSA_HEADS = 4
DSA_HD = 128
DSA_W = DSA_HEADS * DSA_HD
IDX_HEADS = 8
IDX_DIM = 64
TOPK_MAX = 256
REL_BUCKETS = 32
REL_MAX_DIST = 128
D_MIX = ML_W + GLA_W + DSA_W

COLUMN_LAYOUT = (
    ('ml_q', ML_W), ('ml_k', ML_W), ('ml_v', ML_W), ('ml_o', ML_W), ('ml_z', ML_W),
    ('ml_i', ML_HEADS), ('ml_f', ML_HEADS),
    ('gla_q', GLA_WK), ('gla_k', GLA_WK), ('gla_v', GLA_W), ('gla_a', GLA_RANK), ('gla_r', GLA_W),
    ('dsa_q', DSA_W), ('dsa_k', DSA_W), ('dsa_v', DSA_W), ('dsa_z', DSA_W),
    ('idx_q', IDX_HEADS * IDX_DIM), ('idx_k', IDX_DIM), ('idx_w', IDX_HEADS),
)
N_IN = sum(width for _, width in COLUMN_LAYOUT)

kernel_name = 'hybrid_mlstm_gla_dsa_block'


def rmsnorm(x, g):
    xf = x.astype(jnp.float32)
    y = xf * lax.rsqrt(jnp.mean(xf * xf, axis=-1, keepdims=True) + EPS)
    return (y * g.astype(jnp.float32)).astype(x.dtype)


def split_columns(u):
    out, off = {}, 0
    for name, width in COLUMN_LAYOUT:
        out[name] = u[..., off:off + width]
        off += width
    return out


def heads(a, h):
    return a.reshape(a.shape[:-1] + (h, a.shape[-1] // h))


def causal_dwconv(x, w, b):
    c = x.shape[-1]
    y = lax.conv_general_dilated(x, w.astype(x.dtype)[:, None, :], window_strides=(1,),
                                 padding=[(w.shape[0] - 1, 0)],
                                 dimension_numbers=('NWC', 'WIO', 'NWC'), feature_group_count=c)
    return y + b.astype(x.dtype)


def to_chunks(a):
    b, s, h, d = a.shape
    return a.reshape(b, s // CHUNK, CHUNK, h, d).transpose(1, 0, 3, 2, 4)


def gate_chunks(g):
    b, s, h = g.shape
    return g.reshape(b, s // CHUNK, CHUNK, h).transpose(1, 0, 3, 2)


def from_chunks(y):
    nc, b, h, l, d = y.shape
    return y.transpose(1, 0, 3, 2, 4).reshape(b, nc * l, h, d)


def mlstm_chunkwise(q, k, v, li, lf):
    b, s, h, d = q.shape
    dv = v.shape[-1]
    f32 = jnp.float32
    qc = to_chunks(q.astype(f32))
    kc = to_chunks(k.astype(f32) * (d ** -0.5))
    vc = to_chunks(v.astype(f32))
    lic, lfc = gate_chunks(li.astype(f32)), gate_chunks(lf.astype(f32))
    causal = jnp.tril(jnp.ones((CHUNK, CHUNK), dtype=bool))

    def step(carry, inp):
        C, n, m = carry
        qj, kj, vj, lij, lfj = inp
        bcum = jnp.cumsum(lfj, axis=-1)
        dmat = jnp.where(causal, bcum[..., :, None] - bcum[..., None, :] + lij[..., None, :], -jnp.inf)
        inter = bcum + m[..., None]
        m_row = jnp.maximum(inter, jnp.max(dmat, axis=-1))
        w_intra = jnp.exp(dmat - m_row[..., None])
        w_inter = jnp.exp(inter - m_row)
        s_qk = jnp.einsum('bhld,bhsd->bhls', qj, kj) * w_intra
        num = jnp.einsum('bhls,bhse->bhle', s_qk, vj) + w_inter[..., None] * jnp.einsum('bhld,bhde->bhle', qj, C)
        den = jnp.sum(s_qk, axis=-1) + w_inter * jnp.einsum('bhld,bhd->bhl', qj, n)
        hj = num / jnp.maximum(jnp.abs(den), jnp.exp(-m_row))[..., None]
        g = bcum[..., -1]
        a = g[..., None] - bcum + lij
        m_new = jnp.maximum(g + m, jnp.max(a, axis=-1))
        decay = jnp.exp(g + m - m_new)
        wa = jnp.exp(a - m_new[..., None])
        C_new = decay[..., None, None] * C + jnp.einsum('bhl,bhld,bhle->bhde', wa, kj, vj)
        n_new = decay[..., None] * n + jnp.einsum('bhl,bhld->bhd', wa, kj)
        return (C_new, n_new, m_new), hj

    init = (jnp.zeros((b, h, d, dv), f32), jnp.zeros((b, h, d), f32), jnp.zeros((b, h), f32))
    _, hc = lax.scan(step, init, (qc, kc, vc, lic, lfc))
    return from_chunks(hc)


def gla_chunked(q, k, v, la):
    b, s, h, dk = q.shape
    dv = v.shape[-1]
    f32 = jnp.float32
    qc = to_chunks(q.astype(f32) * (dk ** -0.5))
    kc, vc, lac = to_chunks(k.astype(f32)), to_chunks(v.astype(f32)), to_chunks(la.astype(f32))
    causal = jnp.tril(jnp.ones((CHUNK, CHUNK), dtype=bool))

    def step(S, inp):
        qj, kj, vj, laj = inp
        bcum = jnp.cumsum(laj, axis=2)
        diff = bcum[:, :, :, None, :] - bcum[:, :, None, :, :]
        decay = jnp.exp(jnp.where(causal[..., None], diff, -jnp.inf))
        attn = jnp.einsum('bhld,bhsd,bhlsd->bhls', qj, kj, decay)
        o = jnp.einsum('bhls,bhse->bhle', attn, vj) + jnp.einsum('bhld,bhde->bhle', qj * jnp.exp(bcum), S)
        btot = bcum[:, :, -1:, :]
        S_new = jnp.exp(btot[:, :, 0, :])[..., None] * S + jnp.einsum('bhld,bhle->bhde', kj * jnp.exp(btot - bcum), vj)
        return S_new, o

    _, oc = lax.scan(step, jnp.zeros((b, h, dk, dv), f32), (qc, kc, vc, lac))
    return from_chunks(oc)


def t5_bucket(rel):
    half = REL_BUCKETS // 2
    max_exact = half // 2
    ret = jnp.where(rel > 0, half, 0)
    n = jnp.abs(rel)
    nf = jnp.maximum(n, 1).astype(jnp.float32)
    large = max_exact + (jnp.log(nf / max_exact) / math.log(REL_MAX_DIST / max_exact)
                         * (half - max_exact)).astype(jnp.int32)
    large = jnp.minimum(large, half - 1)
    return ret + jnp.where(n < max_exact, n, large)


def dsa_sparse_attention(q, k, v, iq, ik, iw, rel_bias, topk):
    b, s, h, dh = q.shape
    nqb = s // Q_BLOCK
    key_pos = jnp.arange(s, dtype=jnp.int32)
    bidx = jnp.arange(b)[:, None, None]

    def blocks(a):
        return jnp.moveaxis(a.reshape((b, nqb, Q_BLOCK) + a.shape[2:]), 1, 0)

    qpos = key_pos.reshape(nqb, Q_BLOCK)

    def one_block(args):
        qb, iqb, iwb, pos = args
        limit = (pos // CHUNK + 1) * CHUNK
        admissible = key_pos[None, :] < limit[:, None]
        idx_logits = jnp.einsum('bthd,bsd->bths', iqb, ik)
        score = jnp.einsum('bths,bth->bts', jax.nn.relu(idx_logits), iwb).astype(jnp.float32)
        score = jnp.where(admissible[None], score, -jnp.inf)
        _, sel = lax.top_k(score, topk)
        ks = k[bidx, sel]
        vs = v[bidx, sel]
        valid = sel < limit[None, :, None]
        bias = rel_bias[t5_bucket(sel - pos[None, :, None])].astype(jnp.float32)
        logits = jnp.einsum('bthd,btkhd->bthk', qb, ks).astype(jnp.float32) * (dh ** -0.5) \
            + jnp.moveaxis(bias, -1, -2)
        logits = jnp.where(valid[:, :, None, :], logits, -jnp.inf)
        p = jax.nn.softmax(logits, axis=-1).astype(v.dtype)
        return jnp.einsum('bthk,btkhd->bthd', p, vs)

    out = lax.map(one_block, (blocks(q), blocks(iq), blocks(iw), qpos))
    return jnp.moveaxis(out, 0, 1).reshape(b, s, h, dh)


def hybrid_layer(x, norm_g, w_in, ml_conv_w, ml_conv_b, ml_b_i, ml_b_f, ml_norm_g,
                 gla_w_a, gla_b_a, gla_norm_g, dsa_q_g, dsa_k_g, w_out, rel_bias):
    b, s, _ = x.shape
    dt = x.dtype
    p = split_columns(rmsnorm(x, norm_g) @ w_in)
    qk = jax.nn.silu(causal_dwconv(jnp.concatenate([p['ml_q'], p['ml_k']], axis=-1), ml_conv_w, ml_conv_b))
    mq, mk = jnp.split(qk, 2, axis=-1)
    li = (p['ml_i'] + ml_b_i).astype(jnp.float32)
    lf = jax.nn.log_sigmoid((p['ml_f'] + ml_b_f).astype(jnp.float32))
    hm = mlstm_chunkwise(heads(mq, ML_HEADS), heads(mk, ML_HEADS), heads(p['ml_v'], ML_HEADS), li, lf).astype(dt)
    hm = hm * jax.nn.sigmoid(heads(p['ml_o'], ML_HEADS))
    y_ml = rmsnorm(hm, ml_norm_g.reshape(ML_HEADS, ML_HD)).reshape(b, s, ML_W) * jax.nn.silu(p['ml_z'])
    la = jax.nn.log_sigmoid((p['gla_a'] @ gla_w_a + gla_b_a).astype(jnp.float32)) / GLA_TAU
    hg = gla_chunked(heads(p['gla_q'], GLA_HEADS), heads(p['gla_k'], GLA_HEADS),
                     heads(p['gla_v'], GLA_HEADS), heads(la, GLA_HEADS)).astype(dt)
    y_gla = rmsnorm(hg, gla_norm_g.reshape(GLA_HEADS, GLA_DV)).reshape(b, s, GLA_W) * jax.nn.silu(p['gla_r'])
    dq = rmsnorm(heads(p['dsa_q'], DSA_HEADS), dsa_q_g)
    dk = rmsnorm(heads(p['dsa_k'], DSA_HEADS), dsa_k_g)
    iq = heads(p['idx_q'], IDX_HEADS) * (IDX_DIM ** -0.5)
    iw = p['idx_w'] * (IDX_HEADS ** -0.5)
    topk = min(TOPK_MAX, s // 4)
    hd = dsa_sparse_attention(dq, dk, heads(p['dsa_v'], DSA_HEADS), iq, p['idx_k'], iw, rel_bias, topk)
    y_dsa = hd.reshape(b, s, DSA_W) * jax.nn.silu(p['dsa_z'])
    y = jnp.concatenate([y_ml, y_gla, y_dsa], axis=-1) @ w_out
    return x + y


def setup_inputs(seed: int = 0) -> dict:
    key = jax.random.key(seed)
    ks = jax.random.split(key, 16)
    f32 = jnp.float32

    def nrm(k, shape, scale):
        return jax.random.normal(k, shape, f32) * scale

    x = nrm(ks[0], (BATCH, SEQ, D_MODEL), 1.0)
    norm_g = 1.0 + nrm(ks[1], (DEPTH, D_MODEL), 0.02)
    w_in = nrm(ks[2], (DEPTH, D_MODEL, N_IN), D_MODEL ** -0.5)
    ml_conv_w = nrm(ks[3], (DEPTH, ML_CONV, 2 * ML_W), ML_CONV ** -0.5)
    ml_conv_b = nrm(ks[4], (DEPTH, 2 * ML_W), 0.02)
    ml_b_i = nrm(ks[5], (DEPTH, ML_HEADS), 0.1)
    ml_b_f = jnp.linspace(3.0, 6.0, ML_HEADS, dtype=f32)[None, :] + nrm(ks[6], (DEPTH, ML_HEADS), 0.1)
    ml_norm_g = 1.0 + nrm(ks[7], (DEPTH, ML_W), 0.02)
    gla_w_a = nrm(ks[8], (DEPTH, GLA_RANK, GLA_WK), GLA_RANK ** -0.5)
    gla_b_a = nrm(ks[9], (DEPTH, GLA_WK), 0.02)
    gla_norm_g = 1.0 + nrm(ks[10], (DEPTH, GLA_W), 0.02)
    dsa_q_g = 1.0 + nrm(ks[11], (DEPTH, DSA_HD), 0.02)
    dsa_k_g = 1.0 + nrm(ks[12], (DEPTH, DSA_HD), 0.02)
    w_out = nrm(ks[13], (DEPTH, D_MIX, D_MODEL), D_MIX ** -0.5)
    rel_bias = nrm(ks[14], (REL_BUCKETS, DSA_HEADS), 0.2)
    return {'x': x, 'norm_g': norm_g, 'w_in': w_in, 'ml_conv_w': ml_conv_w, 'ml_conv_b': ml_conv_b,
            'ml_b_i': ml_b_i, 'ml_b_f': ml_b_f, 'ml_norm_g': ml_norm_g, 'gla_w_a': gla_w_a,
            'gla_b_a': gla_b_a, 'gla_norm_g': gla_norm_g, 'dsa_q_g': dsa_q_g, 'dsa_k_g': dsa_k_g,
            'w_out': w_out, 'rel_bias': rel_bias}


def reference(x, norm_g, w_in, ml_conv_w, ml_conv_b, ml_b_i, ml_b_f, ml_norm_g,
              gla_w_a, gla_b_a, gla_norm_g, dsa_q_g, dsa_k_g, w_out, rel_bias):
    for layer in range(DEPTH):
        x = hybrid_layer(x, norm_g[layer], w_in[layer], ml_conv_w[layer], ml_conv_b[layer],
                         ml_b_i[layer], ml_b_f[layer], ml_norm_g[layer], gla_w_a[layer],
                         gla_b_a[layer], gla_norm_g[layer], dsa_q_g[layer], dsa_k_g[layer],
                         w_out[layer], rel_bias)
    return x
```

```python
import functools
import math

import jax
import jax.numpy as jnp
from jax import lax
from jax.experimental import pallas as pl
from jax.experimental.pallas import tpu as pltpu

F32 = jnp.float32
BF16 = jnp.bfloat16
I32 = jnp.int32

D_MODEL = 1024
CHUNK = 64
EPS = 1e-6
ML_HEADS, ML_HD = 4, 256
ML_W = ML_HEADS * ML_HD
ML_CONV = 4
GLA_HEADS, GLA_DK, GLA_DV = 4, 64, 128
GLA_WK = GLA_HEADS * GLA_DK
GLA_W = GLA_HEADS * GLA_DV
GLA_RANK = 16
GLA_TAU = 16.0
DSA_HEADS, DSA_HD = 4, 128
DSA_W = DSA_HEADS * DSA_HD
IDX_HEADS, IDX_DIM = 8, 64
TOPK_MAX = 256
REL_BUCKETS, REL_MAX_DIST = 32, 128
D_MIX = ML_W + GLA_W + DSA_W

LANES = 128
SUBLANES = 8
VMEM_BYTES = 64 << 20

TILE = 128
IDX_PAD = LANES

P_ML = 0
P_IDXQ = 5 * ML_W
P_GLA_Q = P_IDXQ + IDX_HEADS * IDX_PAD
P_GLA_K = P_GLA_Q + GLA_WK
P_GLA_V = P_GLA_K + GLA_WK
P_GLA_R = P_GLA_V + GLA_W
P_DSA_Q = P_GLA_R + GLA_W
P_DSA_K = P_DSA_Q + DSA_W
P_DSA_V = P_DSA_K + DSA_W
P_DSA_Z = P_DSA_V + DSA_W
P_COLS = P_DSA_Z + DSA_W
G_IDXK = 0
G_MLI = 64
G_MLF = 72
G_GLAA = 80
G_IDXW = 96

INT_MIN = -(2 ** 31)
NEG_BIG = -1e30


def _cparams(sem, vmem_mb):
    return pltpu.CompilerParams(dimension_semantics=sem, vmem_limit_bytes=vmem_mb << 20)


def _dot(a, b):
    return jnp.dot(a, b, preferred_element_type=F32)


def _dot_nt(a, b):
    return lax.dot_general(a, b, (((1,), (1,)), ((), ())), preferred_element_type=F32)


def _split3(x):
    hi = x.astype(BF16)
    r1 = x - hi.astype(F32)
    mid = r1.astype(BF16)
    lo = (r1 - mid.astype(F32)).astype(BF16)
    return hi, mid, lo


def _dot_f32_lhs(x, m01):
    hi, mid, lo = _split3(x)
    return _dot(hi, m01) + _dot(mid, m01) + _dot(lo, m01)


def _dot_f32_rhs(m01, x):
    hi, mid, lo = _split3(x)
    return _dot(m01, hi) + _dot(m01, mid) + _dot(m01, lo)


def _log_sigmoid(x):
    return jnp.minimum(x, 0.0) - jnp.log(1.0 + jnp.exp(-jnp.abs(x)))


def _sigmoid(x):
    return 1.0 / (1.0 + jnp.exp(-x))


def _silu(x):
    return x * _sigmoid(x)


def _in_proj_kernel(x_ref, g_ref, wb_ref, ws_ref, p_ref, gate_ref, xn_ref):
    @pl.when(pl.program_id(1) == 0)
    def _():
        x = x_ref[...]
        ms = jnp.mean(x * x, axis=-1, keepdims=True)
        xn = (x * lax.rsqrt(ms + EPS) * g_ref[...]).astype(BF16)
        xn_ref[...] = xn
        gate_ref[...] = _dot(xn, ws_ref[...])

    p_ref[...] = _dot(xn_ref[...], wb_ref[...]).astype(BF16)


def _in_proj(x2d, g, wbig, wsmall):
    t = x2d.shape[0]
    tm = min(512, t)
    tn = P_COLS // 4
    return pl.pallas_call(
        _in_proj_kernel,
        grid=(t // tm, P_COLS // tn),
        in_specs=[
            pl.BlockSpec((tm, D_MODEL), lambda i, j: (i, 0)),
            pl.BlockSpec((1, D_MODEL), lambda i, j: (0, 0)),
            pl.BlockSpec((D_MODEL, tn), lambda i, j: (0, j)),
            pl.BlockSpec((D_MODEL, LANES), lambda i, j: (0, 0)),
        ],
        out_specs=[
            pl.BlockSpec((tm, tn), lambda i, j: (i, j)),
            pl.BlockSpec((tm, LANES), lambda i, j: (i, 0)),
        ],
        out_shape=[jax.ShapeDtypeStruct((t, P_COLS), BF16), jax.ShapeDtypeStruct((t, LANES), F32)],
        scratch_shapes=[pltpu.VMEM((tm, D_MODEL), BF16)],
        compiler_params=_cparams(("arbitrary", "arbitrary"), 48),
        name="in_proj",
    )(x2d, g, wbig, wsmall)


def _mlstm_kernel(q_ref, k_ref, v_ref, o_ref, z_ref, g_ref, gb_ref, cw_ref, cb_ref, ng_ref,
                  y_ref, c_scr, n_scr, m_scr, xbuf):
    L = TILE

    @pl.when(pl.program_id(1) == 0)
    def _():
        c_scr[...] = jnp.zeros_like(c_scr)
        n_scr[...] = jnp.zeros_like(n_scr)
        m_scr[...] = jnp.zeros_like(m_scr)
        xbuf[0:SUBLANES, :] = jnp.zeros((SUBLANES, 2 * ML_W), F32)

    xbuf[SUBLANES:SUBLANES + L, 0:ML_W] = q_ref[...].astype(F32)
    xbuf[SUBLANES:SUBLANES + L, ML_W:2 * ML_W] = k_ref[...].astype(F32)
    conv = jnp.broadcast_to(cb_ref[...], (L, 2 * ML_W))
    for j in range(ML_CONV):
        off = SUBLANES - (ML_CONV - 1) + j
        conv = conv + cw_ref[j:j + 1, :] * xbuf[off:off + L, :]
    xbuf[0:SUBLANES, :] = xbuf[L:L + SUBLANES, :]
    qk = _silu(conv)

    gate = g_ref[...] + gb_ref[...]
    lf_cols = _log_sigmoid(gate)
    gate_t = gate.T
    li_rows = gate_t[G_MLI:G_MLI + SUBLANES, :]
    lf_rows = _log_sigmoid(gate_t[G_MLF:G_MLF + SUBLANES, :])
    r_io = lax.broadcasted_iota(I32, (L, L), 0)
    c_io = lax.broadcasted_iota(I32, (L, L), 1)
    causal = c_io <= r_io
    tri_u = (r_io <= c_io).astype(BF16)
    tri_l = causal.astype(BF16)
    bc_rows = _dot_f32_lhs(lf_rows, tri_u)
    bc_cols = _dot_f32_rhs(tri_l, lf_cols)

    for h in range(ML_HEADS):
        hs = slice(h * ML_HD, (h + 1) * ML_HD)
        qh = qk[:, hs]
        kh = qk[:, ML_W + h * ML_HD:ML_W + (h + 1) * ML_HD] * (ML_HD ** -0.5)
        vh = v_ref[:, hs]
        qb = qh.astype(BF16)
        kb = kh.astype(BF16)
        bc_col = bc_cols[:, G_MLF + h:G_MLF + h + 1]
        bc_row = bc_rows[h:h + 1, :]
        li_row = li_rows[h:h + 1, :]
        m_prev = m_scr[h, :, 0:1]

        dmat = jnp.where(causal, bc_col - (bc_row - li_row), -jnp.inf)
        inter = bc_col + m_prev
        m_row = jnp.maximum(inter, jnp.max(dmat, axis=1, keepdims=True))
        w_intra = jnp.exp(dmat - m_row)
        w_inter = jnp.exp(inter - m_row)
        s_qk = _dot_nt(qb, kb) * w_intra
        num = _dot(s_qk.astype(BF16), vh) + w_inter * _dot(qb, c_scr[h].astype(BF16))
        qn = jnp.sum(qh * n_scr[h], axis=1, keepdims=True)
        den = jnp.sum(s_qk, axis=1, keepdims=True) + w_inter * qn
        hh = num / jnp.maximum(jnp.abs(den), jnp.exp(-m_row))

        g_tot = bc_row[:, L - 1:L]
        a_row = g_tot - bc_row + li_row
        m_new = jnp.maximum(g_tot + m_prev, jnp.max(a_row, axis=1, keepdims=True))
        decay = jnp.exp(g_tot + m_prev - m_new)
        wa_row = jnp.exp(a_row - m_new)
        ktw = (kh.T * wa_row).astype(BF16)
        c_scr[h] = decay * c_scr[h] + _dot(ktw, vh)
        wa8 = jnp.broadcast_to(wa_row, (SUBLANES, L)).astype(BF16)
        n_scr[h] = decay * n_scr[h] + _dot(wa8, kb)[0:1, :]
        m_scr[h] = jnp.broadcast_to(m_new, (1, LANES))

        hm = hh * _sigmoid(o_ref[:, hs].astype(F32))
        hm = hm * lax.rsqrt(jnp.mean(hm * hm, axis=1, keepdims=True) + EPS) * ng_ref[:, hs]
        y_ref[:, hs] = (hm * _silu(z_ref[:, hs].astype(F32))).astype(BF16)


def _mlstm(p, g, gbias, conv_w, conv_b, norm_g, b, s):
    nt = s // TILE
    t = b * s

    def pcol(c):
        return pl.BlockSpec((TILE, ML_W), lambda bi, ti, c=c: (bi * nt + ti, c))

    full = lambda shape: pl.BlockSpec(shape, lambda bi, ti: (0,) * len(shape))
    return pl.pallas_call(
        _mlstm_kernel,
        grid=(b, nt),
        in_specs=[pcol(0), pcol(1), pcol(2), pcol(3), pcol(4),
                  pl.BlockSpec((TILE, LANES), lambda bi, ti: (bi * nt + ti, 0)),
                  full((1, LANES)), full((ML_CONV, 2 * ML_W)), full((1, 2 * ML_W)), full((1, ML_W))],
        out_specs=pl.BlockSpec((TILE, ML_W), lambda bi, ti: (bi * nt + ti, 0)),
        out_shape=jax.ShapeDtypeStruct((t, ML_W), BF16),
        scratch_shapes=[pltpu.VMEM((ML_HEADS, ML_HD, ML_HD), F32),
                        pltpu.VMEM((ML_HEADS, 1, ML_HD), F32),
                        pltpu.VMEM((ML_HEADS, 1, LANES), F32),
                        pltpu.VMEM((TILE + SUBLANES, 2 * ML_W), F32)],
        compiler_params=_cparams(("arbitrary", "arbitrary"), 40),
        name="mlstm",
    )(p, p, p, p, p, g, gbias, conv_w, conv_b, norm_g)


def _gla_kernel(q_ref, k_ref, v_ref, r_ref, g_ref, wa_ref, ba_ref, ng_ref, y_ref, st_scr):
    L = TILE
    H = GLA_HEADS

    @pl.when(pl.program_id(1) == 0)
    def _():
        st_scr[...] = jnp.zeros_like(st_scr)

    la = _log_sigmoid(_dot(g_ref[...].astype(BF16), wa_ref[...]) + ba_ref[...]) * (1.0 / GLA_TAU)
    r_io = lax.broadcasted_iota(I32, (L, L), 0)
    c_io = lax.broadcasted_iota(I32, (L, L), 1)
    tri_l = (c_io <= r_io).astype(BF16)
    bcum = _dot_f32_rhs(tri_l, la)
    btot = bcum[L - 1:L, :]
    bmid = bcum[L // 2 - 1:L // 2, :]

    q = q_ref[...].astype(F32) * (GLA_DK ** -0.5)
    k = k_ref[...].astype(F32)
    v = v_ref[...]
    q_in = (q * jnp.exp(bcum - bmid)).astype(BF16)
    k_in = k * jnp.exp(bmid - bcum)
    q_st = (q * jnp.exp(bcum)).astype(BF16)
    k_st = (k * jnp.exp(btot - bcum)).astype(BF16)

    kt = k_in.T
    kt4 = jnp.concatenate([kt] * H, axis=1)
    rr = lax.broadcasted_iota(I32, (GLA_WK, H * L), 0)
    cc = lax.broadcasted_iota(I32, (GLA_WK, H * L), 1)
    k_bd = jnp.where(rr // GLA_DK == cc // L, kt4, 0.0).astype(BF16)
    att = _dot(q_in, k_bd)
    ar = lax.broadcasted_iota(I32, (L, H * L), 0)
    ac = lax.broadcasted_iota(I32, (L, H * L), 1)
    att = jnp.where(ac % L <= ar, att, 0.0).astype(BF16)
    v4 = jnp.concatenate([v] * H, axis=0)
    vr = lax.broadcasted_iota(I32, (H * L, GLA_W), 0)
    vc = lax.broadcasted_iota(I32, (H * L, GLA_W), 1)
    v_bd = jnp.where(vr // L == vc // GLA_DV, v4, jnp.zeros_like(v4))
    o = _dot(att, v_bd) + _dot_nt(q_st, st_scr[...].astype(BF16))

    sr = lax.broadcasted_iota(I32, (GLA_W, GLA_WK), 0)
    sc = lax.broadcasted_iota(I32, (GLA_W, GLA_WK), 1)
    upd = _dot(v.astype(F32).T.astype(BF16), k_st)
    st_scr[...] = st_scr[...] * jnp.exp(btot) + jnp.where(sr // GLA_DV == sc // GLA_DK, upd, 0.0)

    for h in range(H):
        hs = slice(h * GLA_DV, (h + 1) * GLA_DV)
        oh = o[:, hs]
        oh = oh * lax.rsqrt(jnp.mean(oh * oh, axis=1, keepdims=True) + EPS) * ng_ref[:, hs]
        y_ref[:, hs] = (oh * _silu(r_ref[:, hs].astype(F32))).astype(BF16)


def _gla(p, g, wa_pad, b_a, norm_g, b, s):
    nt = s // TILE
    t = b * s

    def pcol(width, off):
        assert off % width == 0
        return pl.BlockSpec((TILE, width), lambda bi, ti: (bi * nt + ti, off // width))

    full = lambda shape: pl.BlockSpec(shape, lambda bi, ti: (0,) * len(shape))
    return pl.pallas_call(
        _gla_kernel,
        grid=(b, nt),
        in_specs=[pcol(GLA_WK, P_GLA_Q), pcol(GLA_WK, P_GLA_K), pcol(GLA_W, P_GLA_V), pcol(GLA_W, P_GLA_R),
                  pl.BlockSpec((TILE, LANES), lambda bi, ti: (bi * nt + ti, 0)),
                  full((LANES, GLA_WK)), full((1, GLA_WK)), full((1, GLA_W))],
        out_specs=pl.BlockSpec((TILE, GLA_W), lambda bi, ti: (bi * nt + ti, 0)),
        out_shape=jax.ShapeDtypeStruct((t, GLA_W), BF16),
        scratch_shapes=[pltpu.VMEM((GLA_W, GLA_WK), F32)],
        compiler_params=_cparams(("arbitrary", "arbitrary"), 32),
        name="gla",
    )(p, p, p, p, g, wa_pad, b_a, norm_g)


def _dsa_prep_kernel(q_ref, k_ref, g_ref, qg_ref, kg_ref, qn_ref, kn_ref, ik_ref):
    for h in range(DSA_HEADS):
        hs = slice(h * DSA_HD, (h + 1) * DSA_HD)
        q = q_ref[:, hs].astype(F32)
        k = k_ref[:, hs].astype(F32)
        qn = q * lax.rsqrt(jnp.mean(q * q, axis=1, keepdims=True) + EPS) * qg_ref[...]
        kn = k * lax.rsqrt(jnp.mean(k * k, axis=1, keepdims=True) + EPS) * kg_ref[...]
        qn_ref[:, hs] = (qn * (DSA_HD ** -0.5)).astype(BF16)
        kn_ref[:, hs] = kn.astype(BF16)
    g = g_ref[...]
    lane = lax.broadcasted_iota(I32, g.shape, 1)
    ik_ref[...] = jnp.where(lane < IDX_DIM, g, 0.0).astype(BF16)


def _dsa_prep(p, g, q_g, k_g):
    t = p.shape[0]
    tm = min(512, t)
    full = lambda shape: pl.BlockSpec(shape, lambda i: (0,) * len(shape))
    return pl.pallas_call(
        _dsa_prep_kernel,
        grid=(t // tm,),
        in_specs=[pl.BlockSpec((tm, DSA_W), lambda i: (i, P_DSA_Q // DSA_W)),
                  pl.BlockSpec((tm, DSA_W), lambda i: (i, P_DSA_K // DSA_W)),
                  pl.BlockSpec((tm, LANES), lambda i: (i, 0)),
                  full((1, DSA_HD)), full((1, DSA_HD))],
        out_specs=[pl.BlockSpec((tm, DSA_W), lambda i: (i, 0)),
                   pl.BlockSpec((tm, DSA_W), lambda i: (i, 0)),
                   pl.BlockSpec((tm, LANES), lambda i: (i, 0))],
        out_shape=[jax.ShapeDtypeStruct((t, DSA_W), BF16), jax.ShapeDtypeStruct((t, DSA_W), BF16),
                   jax.ShapeDtypeStruct((t, LANES), BF16)],
        compiler_params=_cparams(("arbitrary",), 32),
        name="dsa_prep",
    )(p, p, g, q_g, k_g)


def _dsa_kernel(qn_ref, iq_ref, g_ref, z_ref, kn_ref, v_ref, ik_ref, bias_ref, y_ref,
                keys_scr, iw_scr, acc_scr, m_scr, l_scr, *, topk, idx_bits):
    Q = TILE
    i = pl.program_id(1)
    n_kt = i + 1
    lane_io = lax.broadcasted_iota(I32, (Q, Q), 1)
    row_io = lax.broadcasted_iota(I32, (Q, Q), 0)

    def tile_off(kt):
        return pl.multiple_of(kt * Q, Q)

    iw_scale = (IDX_HEADS ** -0.5) * (IDX_DIM ** -0.5)
    for h in range(IDX_HEADS):
        iw_scr[h] = jnp.broadcast_to(g_ref[:, G_IDXW + h:G_IDXW + h + 1] * iw_scale, (Q, Q))

    def score_body(kt, carry):
        ikt = ik_ref[pl.ds(tile_off(kt), Q), :]
        sc = jnp.zeros((Q, Q), F32)
        for h in range(IDX_HEADS):
            lg = _dot_nt(iq_ref[:, h * IDX_PAD:(h + 1) * IDX_PAD], ikt)
            sc = sc + jnp.maximum(lg, 0.0) * iw_scr[h]
        bits = lax.bitcast_convert_type(sc, I32)
        keys_scr[:, pl.ds(tile_off(kt), Q)] = bits ^ ((bits >> 31) & 0x7FFFFFFF)
        return carry

    lax.fori_loop(0, n_kt, score_body, 0)
    kd = keys_scr[:, pl.ds(tile_off(i), Q)]
    keys_scr[:, pl.ds(tile_off(i), Q)] = jnp.where(lane_io >= (row_io // CHUNK + 1) * CHUNK, INT_MIN, kd)

    def count(pred):
        def body(kt, acc):
            return acc + jnp.where(pred(keys_scr[:, pl.ds(tile_off(kt), Q)], kt), 1.0, 0.0)
        acc = lax.fori_loop(0, n_kt, body, jnp.zeros((Q, Q), F32))
        return jnp.sum(acc, axis=1, keepdims=True)

    def thr_bit(b, thr):
        cand = thr + lax.shift_left(jnp.int32(1), 31 - b)
        return jnp.where(count(lambda kk, kt: kk >= cand) >= topk, cand, thr)

    thr = lax.fori_loop(0, 32, thr_bit, jnp.full((Q, 1), INT_MIN, I32))
    thr = jnp.maximum(thr, INT_MIN + 1)
    n_ge = count(lambda kk, kt: kk >= thr)

    @pl.when(jnp.max(n_ge) > topk)
    def _():
        need = topk - count(lambda kk, kt: kk > thr)

        def idx_bit(b, j):
            cand = j + lax.shift_left(jnp.int32(1), idx_bits - 1 - b)
            n_before = count(lambda kk, kt: (kk == thr) & (kt * Q + lane_io < cand))
            return jnp.where(n_before < need, cand, j)

        last = lax.fori_loop(0, idx_bits, idx_bit, jnp.zeros((Q, 1), I32))

        def drop(kt, carry):
            kk = keys_scr[:, pl.ds(tile_off(kt), Q)]
            keys_scr[:, pl.ds(tile_off(kt), Q)] = jnp.where((kk == thr) & (kt * Q + lane_io > last), INT_MIN, kk)
            return carry

        lax.fori_loop(0, n_kt, drop, 0)

    m_scr[...] = jnp.full(m_scr.shape, NEG_BIG, F32)
    l_scr[...] = jnp.zeros_like(l_scr)
    acc_scr[...] = jnp.zeros_like(acc_scr)

    def att_body(kt, carry):
        off = tile_off(kt)
        sel = keys_scr[:, pl.ds(off, Q)] >= thr
        bsel = jnp.minimum(i - kt, 2)
        for h in range(DSA_HEADS):
            hs = slice(h * DSA_HD, (h + 1) * DSA_HD)
            s = _dot_nt(qn_ref[:, hs], kn_ref[pl.ds(off, Q), hs]) + bias_ref[h, bsel]
            s = jnp.where(sel, s, NEG_BIG)
            m_old = m_scr[h]
            m_new = jnp.maximum(m_old, jnp.max(s, axis=1, keepdims=True))
            alpha = jnp.exp(m_old - m_new)
            p = jnp.where(sel, jnp.exp(s - m_new), 0.0)
            l_scr[h] = alpha * l_scr[h] + jnp.sum(p, axis=1, keepdims=True)
            acc_scr[h] = alpha * acc_scr[h] + _dot(p.astype(BF16), v_ref[pl.ds(off, Q), hs])
            m_scr[h] = m_new
        return carry

    lax.fori_loop(0, n_kt, att_body, 0)
    for h in range(DSA_HEADS):
        hs = slice(h * DSA_HD, (h + 1) * DSA_HD)
        y_ref[:, hs] = (acc_scr[h] / l_scr[h] * _silu(z_ref[:, hs].astype(F32))).astype(BF16)


def _dsa(p, g, qn, kn, ikp, bias_tiles, b, s):
    nt = s // TILE
    t = b * s
    topk = min(TOPK_MAX, s // 4)
    idx_bits = max(1, (s - 1).bit_length())
    kernel = functools.partial(_dsa_kernel, topk=topk, idx_bits=idx_bits)
    tok = lambda width, cb: pl.BlockSpec((TILE, width), lambda bi, ti: (bi * nt + ti, cb))
    seq = lambda width, cb: pl.BlockSpec((s, width), lambda bi, ti: (bi, cb))
    return pl.pallas_call(
        kernel,
        grid=(b, nt),
        in_specs=[tok(DSA_W, 0),
                  tok(IDX_HEADS * IDX_PAD, P_IDXQ // (IDX_HEADS * IDX_PAD)),
                  tok(LANES, 0),
                  tok(DSA_W, P_DSA_Z // DSA_W),
                  seq(DSA_W, 0),
                  seq(DSA_W, P_DSA_V // DSA_W),
                  seq(LANES, 0),
                  pl.BlockSpec((DSA_HEADS, 3, TILE, TILE), lambda bi, ti: (0, 0, 0, 0))],
        out_specs=pl.BlockSpec((TILE, DSA_W), lambda bi, ti: (bi * nt + ti, 0)),
        out_shape=jax.ShapeDtypeStruct((t, DSA_W), BF16),
        scratch_shapes=[pltpu.VMEM((TILE, s), I32),
                        pltpu.VMEM((IDX_HEADS, TILE, TILE), F32),
                        pltpu.VMEM((DSA_HEADS, TILE, DSA_HD), F32),
                        pltpu.VMEM((DSA_HEADS, TILE, 1), F32),
                        pltpu.VMEM((DSA_HEADS, TILE, 1), F32)],
        compiler_params=_cparams(("arbitrary", "arbitrary"), 56),
        name="dsa",
    )(qn, p, g, p, kn, p, ikp, bias_tiles)


def _out_proj_kernel(yml_ref, ygla_ref, ydsa_ref, w_ref, x_ref, o_ref):
    y = _dot(yml_ref[...], w_ref[0:ML_W, :])
    y = y + _dot(ygla_ref[...], w_ref[ML_W:ML_W + GLA_W, :])
    y = y + _dot(ydsa_ref[...], w_ref[ML_W + GLA_W:D_MIX, :])
    o_ref[...] = x_ref[...] + y


def _out_proj(y_ml, y_gla, y_dsa, w_out, x2d):
    t = x2d.shape[0]
    tm = min(512, t)
    row = lambda width: pl.BlockSpec((tm, width), lambda i: (i, 0))
    return pl.pallas_call(
        _out_proj_kernel,
        grid=(t // tm,),
        in_specs=[row(ML_W), row(GLA_W), row(DSA_W),
                  pl.BlockSpec((D_MIX, D_MODEL), lambda i: (0, 0)), row(D_MODEL)],
        out_specs=row(D_MODEL),
        out_shape=jax.ShapeDtypeStruct((t, D_MODEL), F32),
        compiler_params=_cparams(("arbitrary",), 40),
        name="out_proj",
    )(y_ml, y_gla, y_dsa, w_out, x2d)


_COLUMN_LAYOUT = (
    ('ml_q', ML_W), ('ml_k', ML_W), ('ml_v', ML_W), ('ml_o', ML_W), ('ml_z', ML_W),
    ('ml_i', ML_HEADS), ('ml_f', ML_HEADS),
    ('gla_q', GLA_WK), ('gla_k', GLA_WK), ('gla_v', GLA_W), ('gla_a', GLA_RANK), ('gla_r', GLA_W),
    ('dsa_q', DSA_W), ('dsa_k', DSA_W), ('dsa_v', DSA_W), ('dsa_z', DSA_W),
    ('idx_q', IDX_HEADS * IDX_DIM), ('idx_k', IDX_DIM), ('idx_w', IDX_HEADS),
)


def _split_w_in(w_in):
    cols, off = {}, 0
    for name, width in _COLUMN_LAYOUT:
        cols[name] = w_in[:, off:off + width]
        off += width
    return cols


def _pack_w_in(w_in):
    c = _split_w_in(w_in)
    d = w_in.shape[0]
    idxq = c['idx_q'].reshape(d, IDX_HEADS, IDX_DIM)
    idxq = jnp.pad(idxq, ((0, 0), (0, 0), (0, IDX_PAD - IDX_DIM))).reshape(d, IDX_HEADS * IDX_PAD)
    wbig = jnp.concatenate([c['ml_q'], c['ml_k'], c['ml_v'], c['ml_o'], c['ml_z'], idxq,
                            c['gla_q'], c['gla_k'], c['gla_v'], c['gla_r'],
                            c['dsa_q'], c['dsa_k'], c['dsa_v'], c['dsa_z']], axis=1)
    assert wbig.shape[1] == P_COLS
    z = lambda n: jnp.zeros((d, n), w_in.dtype)
    wsmall = jnp.concatenate([c['idx_k'], c['ml_i'], z(4), c['ml_f'], z(4), c['gla_a'], c['idx_w'],
                              z(LANES - G_IDXW - IDX_HEADS)], axis=1)
    assert wsmall.shape[1] == LANES
    return wbig.astype(BF16), wsmall.astype(BF16)


def _rel_bucket(rel):
    half = REL_BUCKETS // 2
    max_exact = half // 2
    ret = jnp.where(rel > 0, half, 0)
    n = jnp.abs(rel)
    nf = jnp.maximum(n, 1).astype(F32)
    large = max_exact + (jnp.log(nf / max_exact) / math.log(REL_MAX_DIST / max_exact)
                         * (half - max_exact)).astype(I32)
    large = jnp.minimum(large, half - 1)
    return ret + jnp.where(n < max_exact, n, large)


def _bias_tiles(rel_bias):
    assert REL_MAX_DIST <= TILE
    ql = jnp.arange(TILE, dtype=I32)[:, None]
    kl = jnp.arange(TILE, dtype=I32)[None, :]
    far = jnp.full((TILE, TILE), -(TILE + 1), I32)
    rel = jnp.stack([kl - ql, kl - TILE - ql, far])
    return jnp.transpose(rel_bias.astype(F32)[_rel_bucket(rel)], (3, 0, 1, 2))


def _layer(x2d, b, s, norm_g, w_in, ml_conv_w, ml_conv_b, ml_b_i, ml_b_f, ml_norm_g,
           gla_w_a, gla_b_a, gla_norm_g, dsa_q_g, dsa_k_g, w_out, bias_tiles):
    wbig, wsmall = _pack_w_in(w_in)
    p, g = _in_proj(x2d, norm_g.reshape(1, D_MODEL), wbig, wsmall)

    gbias = jnp.zeros((1, LANES), F32)
    gbias = gbias.at[0, G_MLI:G_MLI + ML_HEADS].set(ml_b_i).at[0, G_MLF:G_MLF + ML_HEADS].set(ml_b_f)
    y_ml = _mlstm(p, g, gbias, ml_conv_w, ml_conv_b.reshape(1, -1), ml_norm_g.reshape(1, -1), b, s)

    wa_pad = jnp.zeros((LANES, GLA_WK), F32).at[G_GLAA:G_GLAA + GLA_RANK].set(gla_w_a).astype(BF16)
    y_gla = _gla(p, g, wa_pad, gla_b_a.reshape(1, -1), gla_norm_g.reshape(1, -1), b, s)

    qn, kn, ikp = _dsa_prep(p, g, dsa_q_g.reshape(1, -1), dsa_k_g.reshape(1, -1))
    y_dsa = _dsa(p, g, qn, kn, ikp, bias_tiles, b, s)

    return _out_proj(y_ml, y_gla, y_dsa, w_out.astype(BF16), x2d)


def kernel(x, norm_g, w_in, ml_conv_w, ml_conv_b, ml_b_i, ml_b_f, ml_norm_g, gla_w_a, gla_b_a, gla_norm_g,
           dsa_q_g, dsa_k_g, w_out, rel_bias):
    b, s, d = x.shape
    assert d == D_MODEL and s % TILE == 0
    depth = norm_g.shape[0]
    bias_tiles = _bias_tiles(rel_bias)
    x2d = x.reshape(b * s, d)
    for layer in range(depth):
        x2d = _layer(x2d, b, s, norm_g[layer], w_in[layer], ml_conv_w[layer], ml_conv_b[layer],
                     ml_b_i[layer], ml_b_f[layer], ml_norm_g[layer], gla_w_a[layer], gla_b_a[layer],
                     gla_norm_g[layer], dsa_q_g[layer], dsa_k_g[layer], w_out[layer], bias_tiles)
    return x2d.reshape(b, s, d)
```

```python
import functools
import math

import jax
import jax.numpy as jnp
from jax import lax
from jax.experimental import pallas as pl
from jax.experimental.pallas import tpu as pltpu

F32 = jnp.float32
BF16 = jnp.bfloat16
I32 = jnp.int32

D_MODEL = 1024
CHUNK = 64
EPS = 1e-6
ML_HEADS, ML_HD = 4, 256
ML_W = ML_HEADS * ML_HD
ML_CONV = 4
GLA_HEADS, GLA_DK, GLA_DV = 4, 64, 128
GLA_WK = GLA_HEADS * GLA_DK
GLA_W = GLA_HEADS * GLA_DV
GLA_RANK = 16
GLA_TAU = 16.0
DSA_HEADS, DSA_HD = 4, 128
DSA_W = DSA_HEADS * DSA_HD
IDX_HEADS, IDX_DIM = 8, 64
IDX_W = IDX_HEADS * IDX_DIM
TOPK_MAX = 256
REL_BUCKETS, REL_MAX_DIST = 32, 128
D_MIX = ML_W + GLA_W + DSA_W

LANES = 128
SUBLANES = 8

TILE = 128
DSA_TK = 512
DSA_SUB = DSA_TK // TILE
DSA_VROWS = DSA_HD + 16
LOG2E = 1.4426950408889634

P_ML = 0
P_IDXQ = 5 * ML_W
P_GLA_Q = P_IDXQ + IDX_W
P_GLA_K = P_GLA_Q + GLA_WK
P_GLA_V = P_GLA_K + GLA_WK
P_GLA_R = P_GLA_V + GLA_W
P_DSA_Q = P_GLA_R + GLA_W
P_DSA_K = P_DSA_Q + DSA_W
P_DSA_V = P_DSA_K + DSA_W
P_DSA_Z = P_DSA_V + DSA_W
P_COLS = P_DSA_Z + DSA_W
G_IDXK = 0
G_MLI = 64
G_MLF = 72
G_GLAA = 80
G_IDXW = 96

INT_MIN = -(2 ** 31)
NEG_BIG = -1e30


def _cparams(sem, vmem_mb):
    return pltpu.CompilerParams(dimension_semantics=sem, vmem_limit_bytes=vmem_mb << 20)


def _dot(a, b):
    return jnp.dot(a, b, preferred_element_type=F32)


def _dot_nt(a, b):
    return lax.dot_general(a, b, (((1,), (1,)), ((), ())), preferred_element_type=F32)


def _split3(x):
    hi = x.astype(BF16)
    r1 = x - hi.astype(F32)
    mid = r1.astype(BF16)
    lo = (r1 - mid.astype(F32)).astype(BF16)
    return hi, mid, lo


def _dot_f32_lhs(x, m01):
    hi, mid, lo = _split3(x)
    return _dot(hi, m01) + _dot(mid, m01) + _dot(lo, m01)


def _dot_f32_rhs(m01, x):
    hi, mid, lo = _split3(x)
    return _dot(m01, hi) + _dot(m01, mid) + _dot(m01, lo)


def _log_sigmoid(x):
    return jnp.minimum(x, 0.0) - jnp.log(1.0 + jnp.exp(-jnp.abs(x)))


def _sigmoid(x):
    return 1.0 / (1.0 + jnp.exp(-x))


def _silu(x):
    return x * _sigmoid(x)


def _in_proj_kernel(x_ref, g_ref, wb_ref, ws_ref, p_ref, gate_ref, xn_ref):
    @pl.when(pl.program_id(1) == 0)
    def _():
        x = x_ref[...]
        ms = jnp.mean(x * x, axis=-1, keepdims=True)
        xn = (x * lax.rsqrt(ms + EPS) * g_ref[...]).astype(BF16)
        xn_ref[...] = xn
        gate_ref[...] = _dot(xn, ws_ref[...])

    p_ref[...] = _dot(xn_ref[...], wb_ref[...]).astype(BF16)


def _in_proj(x2d, g, wbig, wsmall):
    t = x2d.shape[0]
    tm = min(512, t)
    tn = P_COLS // 4
    return pl.pallas_call(
        _in_proj_kernel,
        grid=(t // tm, P_COLS // tn),
        in_specs=[
            pl.BlockSpec((tm, D_MODEL), lambda i, j: (i, 0)),
            pl.BlockSpec((1, D_MODEL), lambda i, j: (0, 0)),
            pl.BlockSpec((D_MODEL, tn), lambda i, j: (0, j)),
            pl.BlockSpec((D_MODEL, LANES), lambda i, j: (0, 0)),
        ],
        out_specs=[
            pl.BlockSpec((tm, tn), lambda i, j: (i, j)),
            pl.BlockSpec((tm, LANES), lambda i, j: (i, 0)),
        ],
        out_shape=[jax.ShapeDtypeStruct((t, P_COLS), BF16), jax.ShapeDtypeStruct((t, LANES), F32)],
        scratch_shapes=[pltpu.VMEM((tm, D_MODEL), BF16)],
        compiler_params=_cparams(("arbitrary", "arbitrary"), 48),
        name="in_proj",
    )(x2d, g, wbig, wsmall)


def _mlstm_kernel(q_ref, k_ref, v_ref, o_ref, z_ref, g_ref, gb_ref, cw_ref, cb_ref, ng_ref,
                  y_ref, c_scr, n_scr, m_scr, xbuf):
    L = TILE

    @pl.when(pl.program_id(1) == 0)
    def _():
        c_scr[...] = jnp.zeros_like(c_scr)
        n_scr[...] = jnp.zeros_like(n_scr)
        m_scr[...] = jnp.zeros_like(m_scr)
        xbuf[0:SUBLANES, :] = jnp.zeros((SUBLANES, 2 * ML_W), F32)

    xbuf[SUBLANES:SUBLANES + L, 0:ML_W] = q_ref[...].astype(F32)
    xbuf[SUBLANES:SUBLANES + L, ML_W:2 * ML_W] = k_ref[...].astype(F32)
    conv = jnp.broadcast_to(cb_ref[...], (L, 2 * ML_W))
    for j in range(ML_CONV):
        off = SUBLANES - (ML_CONV - 1) + j
        conv = conv + cw_ref[j:j + 1, :] * xbuf[off:off + L, :]
    xbuf[0:SUBLANES, :] = xbuf[L:L + SUBLANES, :]
    qk = _silu(conv)

    gate = g_ref[...] + gb_ref[...]
    lf_cols = _log_sigmoid(gate)
    gate_t = gate.T
    li_rows = gate_t[G_MLI:G_MLI + SUBLANES, :]
    lf_rows = _log_sigmoid(gate_t[G_MLF:G_MLF + SUBLANES, :])
    r_io = lax.broadcasted_iota(I32, (L, L), 0)
    c_io = lax.broadcasted_iota(I32, (L, L), 1)
    causal = c_io <= r_io
    tri_u = (r_io <= c_io).astype(BF16)
    tri_l = causal.astype(BF16)
    bc_rows = _dot_f32_lhs(lf_rows, tri_u)
    bc_cols = _dot_f32_rhs(tri_l, lf_cols)

    for h in range(ML_HEADS):
        hs = slice(h * ML_HD, (h + 1) * ML_HD)
        qh = qk[:, hs]
        kh = qk[:, ML_W + h * ML_HD:ML_W + (h + 1) * ML_HD] * (ML_HD ** -0.5)
        vh = v_ref[:, hs]
        qb = qh.astype(BF16)
        kb = kh.astype(BF16)
        bc_col = bc_cols[:, G_MLF + h:G_MLF + h + 1]
        bc_row = bc_rows[h:h + 1, :]
        li_row = li_rows[h:h + 1, :]
        m_prev = m_scr[h, :, 0:1]

        dmat = jnp.where(causal, bc_col - (bc_row - li_row), -jnp.inf)
        inter = bc_col + m_prev
        m_row = jnp.maximum(inter, jnp.max(dmat, axis=1, keepdims=True))
        w_intra = jnp.exp(dmat - m_row)
        w_inter = jnp.exp(inter - m_row)
        s_qk = _dot_nt(qb, kb) * w_intra
        num = _dot(s_qk.astype(BF16), vh) + w_inter * _dot(qb, c_scr[h].astype(BF16))
        qn = jnp.sum(qh * n_scr[h], axis=1, keepdims=True)
        den = jnp.sum(s_qk, axis=1, keepdims=True) + w_inter * qn
        hh = num / jnp.maximum(jnp.abs(den), jnp.exp(-m_row))

        g_tot = bc_row[:, L - 1:L]
        a_row = g_tot - bc_row + li_row
        m_new = jnp.maximum(g_tot + m_prev, jnp.max(a_row, axis=1, keepdims=True))
        decay = jnp.exp(g_tot + m_prev - m_new)
        wa_row = jnp.exp(a_row - m_new)
        ktw = (kh.T * wa_row).astype(BF16)
        c_scr[h] = decay * c_scr[h] + _dot(ktw, vh)
        wa8 = jnp.broadcast_to(wa_row, (SUBLANES, L)).astype(BF16)
        n_scr[h] = decay * n_scr[h] + _dot(wa8, kb)[0:1, :]
        m_scr[h] = jnp.broadcast_to(m_new, (1, LANES))

        hm = hh * _sigmoid(o_ref[:, hs].astype(F32))
        hm = hm * lax.rsqrt(jnp.mean(hm * hm, axis=1, keepdims=True) + EPS) * ng_ref[:, hs]
        y_ref[:, hs] = (hm * _silu(z_ref[:, hs].astype(F32))).astype(BF16)


def _mlstm(p, g, gbias, conv_w, conv_b, norm_g, b, s):
    nt = s // TILE
    t = b * s

    def pcol(c):
        return pl.BlockSpec((TILE, ML_W), lambda bi, ti, c=c: (bi * nt + ti, c))

    full = lambda shape: pl.BlockSpec(shape, lambda bi, ti: (0,) * len(shape))
    return pl.pallas_call(
        _mlstm_kernel,
        grid=(b, nt),
        in_specs=[pcol(0), pcol(1), pcol(2), pcol(3), pcol(4),
                  pl.BlockSpec((TILE, LANES), lambda bi, ti: (bi * nt + ti, 0)),
                  full((1, LANES)), full((ML_CONV, 2 * ML_W)), full((1, 2 * ML_W)), full((1, ML_W))],
        out_specs=pl.BlockSpec((TILE, ML_W), lambda bi, ti: (bi * nt + ti, 0)),
        out_shape=jax.ShapeDtypeStruct((t, ML_W), BF16),
        scratch_shapes=[pltpu.VMEM((ML_HEADS, ML_HD, ML_HD), F32),
                        pltpu.VMEM((ML_HEADS, 1, ML_HD), F32),
                        pltpu.VMEM((ML_HEADS, 1, LANES), F32),
                        pltpu.VMEM((TILE + SUBLANES, 2 * ML_W), F32)],
        compiler_params=_cparams(("arbitrary", "arbitrary"), 40),
        name="mlstm",
    )(p, p, p, p, p, g, gbias, conv_w, conv_b, norm_g)


def _gla_kernel(q_ref, k_ref, v_ref, r_ref, g_ref, wa_ref, ba_ref, ng_ref, y_ref, st_scr):
    L = TILE
    H = GLA_HEADS

    @pl.when(pl.program_id(1) == 0)
    def _():
        st_scr[...] = jnp.zeros_like(st_scr)

    la = _log_sigmoid(_dot(g_ref[...].astype(BF16), wa_ref[...]) + ba_ref[...]) * (1.0 / GLA_TAU)
    r_io = lax.broadcasted_iota(I32, (L, L), 0)
    c_io = lax.broadcasted_iota(I32, (L, L), 1)
    tri_l = (c_io <= r_io).astype(BF16)
    bcum = _dot_f32_rhs(tri_l, la)
    btot = bcum[L - 1:L, :]
    bmid = bcum[L // 2 - 1:L // 2, :]

    q = q_ref[...].astype(F32) * (GLA_DK ** -0.5)
    k = k_ref[...].astype(F32)
    v = v_ref[...]
    q_in = (q * jnp.exp(bcum - bmid)).astype(BF16)
    k_in = k * jnp.exp(bmid - bcum)
    q_st = (q * jnp.exp(bcum)).astype(BF16)
    k_st = (k * jnp.exp(btot - bcum)).astype(BF16)

    kt = k_in.T
    kt4 = jnp.concatenate([kt] * H, axis=1)
    rr = lax.broadcasted_iota(I32, (GLA_WK, H * L), 0)
    cc = lax.broadcasted_iota(I32, (GLA_WK, H * L), 1)
    k_bd = jnp.where(rr // GLA_DK == cc // L, kt4, 0.0).astype(BF16)
    att = _dot(q_in, k_bd)
    ar = lax.broadcasted_iota(I32, (L, H * L), 0)
    ac = lax.broadcasted_iota(I32, (L, H * L), 1)
    att = jnp.where(ac % L <= ar, att, 0.0).astype(BF16)
    v4 = jnp.concatenate([v] * H, axis=0)
    vr = lax.broadcasted_iota(I32, (H * L, GLA_W), 0)
    vc = lax.broadcasted_iota(I32, (H * L, GLA_W), 1)
    v_bd = jnp.where(vr // L == vc // GLA_DV, v4, jnp.zeros_like(v4))
    o = _dot(att, v_bd) + _dot_nt(q_st, st_scr[...].astype(BF16))

    sr = lax.broadcasted_iota(I32, (GLA_W, GLA_WK), 0)
    sc = lax.broadcasted_iota(I32, (GLA_W, GLA_WK), 1)
    upd = _dot(v.astype(F32).T.astype(BF16), k_st)
    st_scr[...] = st_scr[...] * jnp.exp(btot) + jnp.where(sr // GLA_DV == sc // GLA_DK, upd, 0.0)

    for h in range(H):
        hs = slice(h * GLA_DV, (h + 1) * GLA_DV)
        oh = o[:, hs]
        oh = oh * lax.rsqrt(jnp.mean(oh * oh, axis=1, keepdims=True) + EPS) * ng_ref[:, hs]
        y_ref[:, hs] = (oh * _silu(r_ref[:, hs].astype(F32))).astype(BF16)


def _gla(p, g, wa_pad, b_a, norm_g, b, s):
    nt = s // TILE
    t = b * s

    def pcol(width, off):
        assert off % width == 0
        return pl.BlockSpec((TILE, width), lambda bi, ti: (bi * nt + ti, off // width))

    full = lambda shape: pl.BlockSpec(shape, lambda bi, ti: (0,) * len(shape))
    return pl.pallas_call(
        _gla_kernel,
        grid=(b, nt),
        in_specs=[pcol(GLA_WK, P_GLA_Q), pcol(GLA_WK, P_GLA_K), pcol(GLA_W, P_GLA_V), pcol(GLA_W, P_GLA_R),
                  pl.BlockSpec((TILE, LANES), lambda bi, ti: (bi * nt + ti, 0)),
                  full((LANES, GLA_WK)), full((1, GLA_WK)), full((1, GLA_W))],
        out_specs=pl.BlockSpec((TILE, GLA_W), lambda bi, ti: (bi * nt + ti, 0)),
        out_shape=jax.ShapeDtypeStruct((t, GLA_W), BF16),
        scratch_shapes=[pltpu.VMEM((GLA_W, GLA_WK), F32)],
        compiler_params=_cparams(("arbitrary", "arbitrary"), 32),
        name="gla",
    )(p, p, p, p, g, wa_pad, b_a, norm_g)


def _dsa_prep_kernel(q_ref, k_ref, v_ref, iq_ref, g_ref, qg_ref, kg_ref,
                     qt_ref, kn_ref, vt_ref, iqt_ref, ik_ref, iwt_ref):
    qn = []
    for h in range(DSA_HEADS):
        hs = slice(h * DSA_HD, (h + 1) * DSA_HD)
        q = q_ref[:, hs].astype(F32)
        k = k_ref[:, hs].astype(F32)
        qh = q * lax.rsqrt(jnp.mean(q * q, axis=1, keepdims=True) + EPS) * qg_ref[...]
        kh = k * lax.rsqrt(jnp.mean(k * k, axis=1, keepdims=True) + EPS) * kg_ref[...]
        qn.append(qh * (DSA_HD ** -0.5 * LOG2E))
        kn_ref[:, hs] = kh.astype(BF16)
        r0 = h * DSA_VROWS
        vt_ref[r0:r0 + DSA_HD, :] = v_ref[:, hs].astype(F32).T.astype(BF16)
        vt_ref[r0 + DSA_HD:r0 + DSA_VROWS, :] = jnp.ones((DSA_VROWS - DSA_HD, vt_ref.shape[1]), BF16)
    qt_ref[...] = jnp.concatenate(qn, axis=1).T.astype(BF16)
    iqt_ref[...] = iq_ref[...].astype(F32).T.astype(BF16)
    g = g_ref[...]
    ik_ref[...] = g[:, G_IDXK:G_IDXK + IDX_DIM].astype(BF16)
    iw_scale = (IDX_HEADS ** -0.5) * (IDX_DIM ** -0.5)
    iwt_ref[...] = g.T[G_IDXW:G_IDXW + IDX_HEADS, :] * iw_scale


def _dsa_prep(p, g, q_g, k_g, b, s):
    t = b * s
    tm = min(512, s)
    nt = s // tm
    full = lambda shape: pl.BlockSpec(shape, lambda bi, ti: (0,) * len(shape))
    tok = lambda width, cb: pl.BlockSpec((tm, width), lambda bi, ti: (bi * nt + ti, cb))
    feat = lambda rows: pl.BlockSpec((None, rows, tm), lambda bi, ti: (bi, 0, ti))
    return pl.pallas_call(
        _dsa_prep_kernel,
        grid=(b, nt),
        in_specs=[tok(DSA_W, P_DSA_Q // DSA_W), tok(DSA_W, P_DSA_K // DSA_W), tok(DSA_W, P_DSA_V // DSA_W),
                  tok(IDX_W, P_IDXQ // IDX_W), tok(LANES, 0), full((1, DSA_HD)), full((1, DSA_HD))],
        out_specs=[feat(DSA_W), tok(DSA_W, 0), feat(DSA_HEADS * DSA_VROWS), feat(IDX_W), tok(IDX_DIM, 0),
                   feat(IDX_HEADS)],
        out_shape=[jax.ShapeDtypeStruct((b, DSA_W, s), BF16),
                   jax.ShapeDtypeStruct((t, DSA_W), BF16),
                   jax.ShapeDtypeStruct((b, DSA_HEADS * DSA_VROWS, s), BF16),
                   jax.ShapeDtypeStruct((b, IDX_W, s), BF16),
                   jax.ShapeDtypeStruct((t, IDX_DIM), BF16),
                   jax.ShapeDtypeStruct((b, IDX_HEADS, s), F32)],
        compiler_params=_cparams(("arbitrary", "arbitrary"), 40),
        name="dsa_prep",
    )(p, p, p, p, g, q_g, k_g)


def _dsa_kernel(qt_ref, iqt_ref, iwt_ref, z_ref, kn_ref, vt_ref, ik_ref, bias_ref, y_ref,
                keys_scr, acc_scr, m_scr, iq2_scr, iw2_scr, *, topk, idx_bits):
    Q, TK, SUB = TILE, DSA_TK, DSA_SUB
    i = pl.program_id(1)
    n_it = (i + SUB) // SUB
    qpos = i * Q + lax.broadcasted_iota(I32, (1, Q), 1)
    limit = (qpos // CHUNK + 1) * CHUNK
    krow = lax.broadcasted_iota(I32, (TK, Q), 0)

    def koff(kt):
        return pl.multiple_of(kt * TK, TK)

    for j in range(IDX_HEADS // 2):
        for e in range(2):
            h = 2 * j + e
            iq2_scr[j, :, e * Q:(e + 1) * Q] = iqt_ref[h * IDX_DIM:(h + 1) * IDX_DIM, :]
            iw2_scr[j, :, e * Q:(e + 1) * Q] = iwt_ref[h:h + 1, :]

    def score_tile(kt, last):
        off = koff(kt)
        ikt = ik_ref[pl.ds(off, TK), :]
        sc = jnp.zeros((TK, Q), F32)
        for j in range(IDX_HEADS // 2):
            w = jnp.maximum(_dot(ikt, iq2_scr[j]), 0.0) * iw2_scr[j]
            sc = sc + (w[:, 0:Q] + w[:, Q:2 * Q])
        bits = lax.bitcast_convert_type(sc, I32)
        keys = bits ^ ((bits >> 31) & 0x7FFFFFFF)
        if last:
            keys = jnp.where(off + krow >= limit, INT_MIN, keys)
        keys_scr[pl.ds(off, TK), :] = keys

    def score_body(kt, carry):
        score_tile(kt, False)
        return carry

    lax.fori_loop(0, n_it - 1, score_body, 0)
    score_tile(n_it - 1, True)

    CR = 8 * SUBLANES

    def count(pred):
        def body(kt, acc):
            off = koff(kt)
            ind = jnp.where(pred(keys_scr[pl.ds(off, TK), :], off), 1.0, 0.0)
            return acc + jnp.sum(ind.reshape(TK // CR, CR, Q), axis=0)
        acc = lax.fori_loop(0, n_it, body, jnp.zeros((CR, Q), F32))
        return jnp.sum(acc, axis=0, keepdims=True)

    def thr_bit(b, thr):
        cand = thr + lax.shift_left(jnp.int32(1), 31 - b)
        return jnp.where(count(lambda kk, off: kk >= cand) >= topk, cand, thr)

    thr = lax.fori_loop(0, 32, thr_bit, jnp.full((1, Q), INT_MIN, I32))
    thr = jnp.maximum(thr, INT_MIN + 1)
    n_ge = count(lambda kk, off: kk >= thr)

    @pl.when(jnp.max(n_ge) > topk)
    def _():
        need = topk - count(lambda kk, off: kk > thr)

        def idx_bit(b, j):
            cand = j + lax.shift_left(jnp.int32(1), idx_bits - 1 - b)
            n_before = count(lambda kk, off: (kk == thr) & (off + krow < cand))
            return jnp.where(n_before < need, cand, j)

        last = lax.fori_loop(0, idx_bits, idx_bit, jnp.zeros((1, Q), I32))

        def drop(kt, carry):
            off = koff(kt)
            kk = keys_scr[pl.ds(off, TK), :]
            keys_scr[pl.ds(off, TK), :] = jnp.where((kk == thr) & (off + krow > last), INT_MIN, kk)
            return carry

        lax.fori_loop(0, n_it, drop, 0)

    m_scr[...] = jnp.full(m_scr.shape, NEG_BIG, F32)
    acc_scr[...] = jnp.zeros_like(acc_scr)

    def att_step(kt, near):
        off = koff(kt)
        sel = keys_scr[pl.ds(off, TK), :] >= thr
        if near:
            bsel = [jnp.clip(i - (kt * SUB + j), 0, 2) for j in range(SUB)]
        for h in range(DSA_HEADS):
            hs = slice(h * DSA_HD, (h + 1) * DSA_HD)
            s = _dot(kn_ref[pl.ds(off, TK), hs], qt_ref[hs, :])
            if near:
                s = s + jnp.concatenate([bias_ref[h, bsel[j]] for j in range(SUB)], axis=0)
            s = jnp.where(sel, s, NEG_BIG)
            m_old = m_scr[h]
            m_new = jnp.maximum(m_old, jnp.max(s, axis=0, keepdims=True))
            p = jnp.exp2(s - m_new).astype(BF16)
            v1 = vt_ref[h * DSA_VROWS:(h + 1) * DSA_VROWS, pl.ds(off, TK)]
            acc_scr[h] = jnp.exp2(m_old - m_new) * acc_scr[h] + _dot(v1, p)
            m_scr[h] = m_new

    def att_far(kt, carry):
        att_step(kt, False)
        return carry

    def att_near(kt, carry):
        att_step(kt, True)
        return carry

    n_far = jnp.maximum(n_it - 2, 0)
    lax.fori_loop(0, n_far, att_far, 0)
    lax.fori_loop(n_far, n_it, att_near, 0)
    for h in range(DSA_HEADS):
        hs = slice(h * DSA_HD, (h + 1) * DSA_HD)
        acc = acc_scr[h]
        out = (acc[0:DSA_HD, :] / acc[DSA_HD:DSA_HD + 1, :]).T
        y_ref[:, hs] = (out * _silu(z_ref[:, hs].astype(F32))).astype(BF16)


def _dsa(p, qt, kn, vt, iqt, ik, iwt, bias_tiles, b, s):
    assert s % DSA_TK == 0
    nt = s // TILE
    t = b * s
    topk = min(TOPK_MAX, s // 4)
    idx_bits = max(1, (s - 1).bit_length())
    kernel = functools.partial(_dsa_kernel, topk=topk, idx_bits=idx_bits)
    once = pl.Buffered(1)
    qfeat = lambda rows: pl.BlockSpec((None, rows, TILE), lambda bi, ti: (bi, 0, ti))
    return pl.pallas_call(
        kernel,
        grid=(b, nt),
        in_specs=[qfeat(DSA_W), qfeat(IDX_W), qfeat(IDX_HEADS),
                  pl.BlockSpec((TILE, DSA_W), lambda bi, ti: (bi * nt + ti, P_DSA_Z // DSA_W)),
                  pl.BlockSpec((s, DSA_W), lambda bi, ti: (bi, 0), pipeline_mode=once),
                  pl.BlockSpec((None, DSA_HEADS * DSA_VROWS, s), lambda bi, ti: (bi, 0, 0), pipeline_mode=once),
                  pl.BlockSpec((s, IDX_DIM), lambda bi, ti: (bi, 0), pipeline_mode=once),
                  pl.BlockSpec((DSA_HEADS, 3, TILE, TILE), lambda bi, ti: (0, 0, 0, 0))],
        out_specs=pl.BlockSpec((TILE, DSA_W), lambda bi, ti: (bi * nt + ti, 0)),
        out_shape=jax.ShapeDtypeStruct((t, DSA_W), BF16),
        scratch_shapes=[pltpu.VMEM((s, TILE), I32),
                        pltpu.VMEM((DSA_HEADS, DSA_VROWS, TILE), F32),
                        pltpu.VMEM((DSA_HEADS, 1, TILE), F32),
                        pltpu.VMEM((IDX_HEADS // 2, IDX_DIM, 2 * TILE), BF16),
                        pltpu.VMEM((IDX_HEADS // 2, 1, 2 * TILE), F32)],
        compiler_params=_cparams(("arbitrary", "arbitrary"), 48),
        name="dsa",
    )(qt, iqt, iwt, p, kn, vt, ik, bias_tiles)


def _out_proj_kernel(yml_ref, ygla_ref, ydsa_ref, w_ref, x_ref, o_ref):
    y = _dot(yml_ref[...], w_ref[0:ML_W, :])
    y = y + _dot(ygla_ref[...], w_ref[ML_W:ML_W + GLA_W, :])
    y = y + _dot(ydsa_ref[...], w_ref[ML_W + GLA_W:D_MIX, :])
    o_ref[...] = x_ref[...] + y


def _out_proj(y_ml, y_gla, y_dsa, w_out, x2d):
    t = x2d.shape[0]
    tm = min(512, t)
    row = lambda width: pl.BlockSpec((tm, width), lambda i: (i, 0))
    return pl.pallas_call(
        _out_proj_kernel,
        grid=(t // tm,),
        in_specs=[row(ML_W), row(GLA_W), row(DSA_W),
                  pl.BlockSpec((D_MIX, D_MODEL), lambda i: (0, 0)), row(D_MODEL)],
        out_specs=row(D_MODEL),
        out_shape=jax.ShapeDtypeStruct((t, D_MODEL), F32),
        compiler_params=_cparams(("arbitrary",), 40),
        name="out_proj",
    )(y_ml, y_gla, y_dsa, w_out, x2d)


_COLUMN_LAYOUT = (
    ('ml_q', ML_W), ('ml_k', ML_W), ('ml_v', ML_W), ('ml_o', ML_W), ('ml_z', ML_W),
    ('ml_i', ML_HEADS), ('ml_f', ML_HEADS),
    ('gla_q', GLA_WK), ('gla_k', GLA_WK), ('gla_v', GLA_W), ('gla_a', GLA_RANK), ('gla_r', GLA_W),
    ('dsa_q', DSA_W), ('dsa_k', DSA_W), ('dsa_v', DSA_W), ('dsa_z', DSA_W),
    ('idx_q', IDX_W), ('idx_k', IDX_DIM), ('idx_w', IDX_HEADS),
)


def _split_w_in(w_in):
    cols, off = {}, 0
    for name, width in _COLUMN_LAYOUT:
        cols[name] = w_in[:, off:off + width]
        off += width
    return cols


def _pack_w_in(w_in):
    c = _split_w_in(w_in)
    d = w_in.shape[0]
    wbig = jnp.concatenate([c['ml_q'], c['ml_k'], c['ml_v'], c['ml_o'], c['ml_z'], c['idx_q'],
                            c['gla_q'], c['gla_k'], c['gla_v'], c['gla_r'],
                            c['dsa_q'], c['dsa_k'], c['dsa_v'], c['dsa_z']], axis=1)
    assert wbig.shape[1] == P_COLS
    z = lambda n: jnp.zeros((d, n), w_in.dtype)
    wsmall = jnp.concatenate([c['idx_k'], c['ml_i'], z(4), c['ml_f'], z(4), c['gla_a'], c['idx_w'],
                              z(LANES - G_IDXW - IDX_HEADS)], axis=1)
    assert wsmall.shape[1] == LANES
    return wbig.astype(BF16), wsmall.astype(BF16)


def _rel_bucket(rel):
    half = REL_BUCKETS // 2
    max_exact = half // 2
    ret = jnp.where(rel > 0, half, 0)
    n = jnp.abs(rel)
    nf = jnp.maximum(n, 1).astype(F32)
    large = max_exact + (jnp.log(nf / max_exact) / math.log(REL_MAX_DIST / max_exact)
                         * (half - max_exact)).astype(I32)
    large = jnp.minimum(large, half - 1)
    return ret + jnp.where(n < max_exact, n, large)


def _bias_tiles(rel_bias):
    assert REL_MAX_DIST <= TILE
    kl = jnp.arange(TILE, dtype=I32)[:, None]
    ql = jnp.arange(TILE, dtype=I32)[None, :]
    far = jnp.full((TILE, TILE), -(TILE + 1), I32)
    rel = jnp.stack([kl - ql, kl - TILE - ql, far])
    tab = jnp.transpose(rel_bias.astype(F32)[_rel_bucket(rel)], (3, 0, 1, 2))
    return (tab - tab[:, 2:3]) * LOG2E


def _layer(x2d, b, s, norm_g, w_in, ml_conv_w, ml_conv_b, ml_b_i, ml_b_f, ml_norm_g,
           gla_w_a, gla_b_a, gla_norm_g, dsa_q_g, dsa_k_g, w_out, bias_tiles):
    wbig, wsmall = _pack_w_in(w_in)
    p, g = _in_proj(x2d, norm_g.reshape(1, D_MODEL), wbig, wsmall)

    gbias = jnp.zeros((1, LANES), F32)
    gbias = gbias.at[0, G_MLI:G_MLI + ML_HEADS].set(ml_b_i).at[0, G_MLF:G_MLF + ML_HEADS].set(ml_b_f)
    y_ml = _mlstm(p, g, gbias, ml_conv_w, ml_conv_b.reshape(1, -1), ml_norm_g.reshape(1, -1), b, s)

    wa_pad = jnp.zeros((LANES, GLA_WK), F32).at[G_GLAA:G_GLAA + GLA_RANK].set(gla_w_a).astype(BF16)
    y_gla = _gla(p, g, wa_pad, gla_b_a.reshape(1, -1), gla_norm_g.reshape(1, -1), b, s)

    qt, kn, vt, iqt, ik, iwt = _dsa_prep(p, g, dsa_q_g.reshape(1, -1), dsa_k_g.reshape(1, -1), b, s)
    y_dsa = _dsa(p, qt, kn, vt, iqt, ik, iwt, bias_tiles, b, s)

    return _out_proj(y_ml, y_gla, y_dsa, w_out.astype(BF16), x2d)


def kernel(x, norm_g, w_in, ml_conv_w, ml_conv_b, ml_b_i, ml_b_f, ml_norm_g, gla_w_a, gla_b_a, gla_norm_g,
           dsa_q_g, dsa_k_g, w_out, rel_bias):
    b, s, d = x.shape
    assert d == D_MODEL and s % TILE == 0
    depth = norm_g.shape[0]
    bias_tiles = _bias_tiles(rel_bias)
    x2d = x.reshape(b * s, d)
    for layer in range(depth):
        x2d = _layer(x2d, b, s, norm_g[layer], w_in[layer], ml_conv_w[layer], ml_conv_b[layer],
                     ml_b_i[layer], ml_b_f[layer], ml_norm_g[layer], gla_w_a[layer], gla_b_a[layer],
                     gla_norm_g[layer], dsa_q_g[layer], dsa_k_g[layer], w_out[layer], bias_tiles)
    return x2d.reshape(b, s, d)
```

```python
import functools
import math

import jax
import jax.numpy as jnp
from jax import lax
from jax.experimental import pallas as pl
from jax.experimental.pallas import tpu as pltpu

F32 = jnp.float32
BF16 = jnp.bfloat16
I32 = jnp.int32

D_MODEL = 1024
CHUNK = 64
EPS = 1e-6
ML_HEADS, ML_HD = 4, 256
ML_W = ML_HEADS * ML_HD
ML_CONV = 4
GLA_HEADS, GLA_DK, GLA_DV = 4, 64, 128
GLA_WK = GLA_HEADS * GLA_DK
GLA_W = GLA_HEADS * GLA_DV
GLA_RANK = 16
GLA_TAU = 16.0
DSA_HEADS, DSA_HD = 4, 128
DSA_W = DSA_HEADS * DSA_HD
IDX_HEADS, IDX_DIM = 8, 64
IDX_W = IDX_HEADS * IDX_DIM
TOPK_MAX = 256
REL_BUCKETS, REL_MAX_DIST = 32, 128
D_MIX = ML_W + GLA_W + DSA_W

LANES = 128
SUBLANES = 8

TILE = 128
DSA_TK = 512
DSA_SUB = DSA_TK // TILE
DSA_GROUP = 2
DSA_VROWS = DSA_HD + 16
LOG2E = 1.4426950408889634

P_ML = 0
P_IDXQ = 5 * ML_W
P_GLA_Q = P_IDXQ + IDX_W
P_GLA_K = P_GLA_Q + GLA_WK
P_GLA_V = P_GLA_K + GLA_WK
P_GLA_R = P_GLA_V + GLA_W
P_DSA_Q = P_GLA_R + GLA_W
P_DSA_K = P_DSA_Q + DSA_W
P_DSA_V = P_DSA_K + DSA_W
P_DSA_Z = P_DSA_V + DSA_W
P_COLS = P_DSA_Z + DSA_W
G_IDXK = 0
G_MLI = 64
G_MLF = 72
G_GLAA = 80
G_IDXW = 96

INT_MIN = -(2 ** 31)
NEG_BIG = -1e30


def _cparams(sem, vmem_mb):
    return pltpu.CompilerParams(dimension_semantics=sem, vmem_limit_bytes=vmem_mb << 20)


def _dot(a, b):
    return jnp.dot(a, b, preferred_element_type=F32)


def _dot_nt(a, b):
    return lax.dot_general(a, b, (((1,), (1,)), ((), ())), preferred_element_type=F32)


def _split3(x):
    hi = x.astype(BF16)
    r1 = x - hi.astype(F32)
    mid = r1.astype(BF16)
    lo = (r1 - mid.astype(F32)).astype(BF16)
    return hi, mid, lo


def _dot_f32_lhs(x, m01):
    hi, mid, lo = _split3(x)
    return _dot(hi, m01) + _dot(mid, m01) + _dot(lo, m01)


def _dot_f32_rhs(m01, x):
    hi, mid, lo = _split3(x)
    return _dot(m01, hi) + _dot(m01, mid) + _dot(m01, lo)


def _log_sigmoid(x):
    return jnp.minimum(x, 0.0) - jnp.log(1.0 + jnp.exp(-jnp.abs(x)))


def _sigmoid(x):
    return 1.0 / (1.0 + jnp.exp(-x))


def _silu(x):
    return x * _sigmoid(x)


def _in_proj_kernel(x_ref, g_ref, wb_ref, ws_ref, p_ref, gate_ref, xn_ref):
    @pl.when(pl.program_id(1) == 0)
    def _():
        x = x_ref[...]
        ms = jnp.mean(x * x, axis=-1, keepdims=True)
        xn = (x * lax.rsqrt(ms + EPS) * g_ref[...]).astype(BF16)
        xn_ref[...] = xn
        gate_ref[...] = _dot(xn, ws_ref[...])

    p_ref[...] = _dot(xn_ref[...], wb_ref[...]).astype(BF16)


def _in_proj(x2d, g, wbig, wsmall):
    t = x2d.shape[0]
    tm = min(512, t)
    tn = P_COLS // 4
    return pl.pallas_call(
        _in_proj_kernel,
        grid=(t // tm, P_COLS // tn),
        in_specs=[
            pl.BlockSpec((tm, D_MODEL), lambda i, j: (i, 0)),
            pl.BlockSpec((1, D_MODEL), lambda i, j: (0, 0)),
            pl.BlockSpec((D_MODEL, tn), lambda i, j: (0, j)),
            pl.BlockSpec((D_MODEL, LANES), lambda i, j: (0, 0)),
        ],
        out_specs=[
            pl.BlockSpec((tm, tn), lambda i, j: (i, j)),
            pl.BlockSpec((tm, LANES), lambda i, j: (i, 0)),
        ],
        out_shape=[jax.ShapeDtypeStruct((t, P_COLS), BF16), jax.ShapeDtypeStruct((t, LANES), F32)],
        scratch_shapes=[pltpu.VMEM((tm, D_MODEL), BF16)],
        compiler_params=_cparams(("arbitrary", "arbitrary"), 48),
        name="in_proj",
    )(x2d, g, wbig, wsmall)


def _mlstm_kernel(q_ref, k_ref, v_ref, o_ref, z_ref, g_ref, gb_ref, cw_ref, cb_ref, ng_ref,
                  y_ref, c_scr, n_scr, m_scr, xbuf):
    L = TILE

    @pl.when(pl.program_id(1) == 0)
    def _():
        c_scr[...] = jnp.zeros_like(c_scr)
        n_scr[...] = jnp.zeros_like(n_scr)
        m_scr[...] = jnp.zeros_like(m_scr)
        xbuf[0:SUBLANES, :] = jnp.zeros((SUBLANES, 2 * ML_W), F32)

    xbuf[SUBLANES:SUBLANES + L, 0:ML_W] = q_ref[...].astype(F32)
    xbuf[SUBLANES:SUBLANES + L, ML_W:2 * ML_W] = k_ref[...].astype(F32)
    conv = jnp.broadcast_to(cb_ref[...], (L, 2 * ML_W))
    for j in range(ML_CONV):
        off = SUBLANES - (ML_CONV - 1) + j
        conv = conv + cw_ref[j:j + 1, :] * xbuf[off:off + L, :]
    xbuf[0:SUBLANES, :] = xbuf[L:L + SUBLANES, :]
    qk = _silu(conv)

    gate = g_ref[...] + gb_ref[...]
    lf_cols = _log_sigmoid(gate)
    gate_t = gate.T
    li_rows = gate_t[G_MLI:G_MLI + SUBLANES, :]
    lf_rows = _log_sigmoid(gate_t[G_MLF:G_MLF + SUBLANES, :])
    r_io = lax.broadcasted_iota(I32, (L, L), 0)
    c_io = lax.broadcasted_iota(I32, (L, L), 1)
    causal = c_io <= r_io
    tri_u = (r_io <= c_io).astype(BF16)
    tri_l = causal.astype(BF16)
    bc_rows = _dot_f32_lhs(lf_rows, tri_u)
    bc_cols = _dot_f32_rhs(tri_l, lf_cols)

    for h in range(ML_HEADS):
        hs = slice(h * ML_HD, (h + 1) * ML_HD)
        qh = qk[:, hs]
        kh = qk[:, ML_W + h * ML_HD:ML_W + (h + 1) * ML_HD] * (ML_HD ** -0.5)
        vh = v_ref[:, hs]
        qb = qh.astype(BF16)
        kb = kh.astype(BF16)
        bc_col = bc_cols[:, G_MLF + h:G_MLF + h + 1]
        bc_row = bc_rows[h:h + 1, :]
        li_row = li_rows[h:h + 1, :]
        m_prev = m_scr[h, :, 0:1]

        dmat = jnp.where(causal, bc_col - (bc_row - li_row), -jnp.inf)
        inter = bc_col + m_prev
        m_row = jnp.maximum(inter, jnp.max(dmat, axis=1, keepdims=True))
        w_intra = jnp.exp(dmat - m_row)
        w_inter = jnp.exp(inter - m_row)
        s_qk = _dot_nt(qb, kb) * w_intra
        num = _dot(s_qk.astype(BF16), vh) + w_inter * _dot(qb, c_scr[h].astype(BF16))
        qn = jnp.sum(qh * n_scr[h], axis=1, keepdims=True)
        den = jnp.sum(s_qk, axis=1, keepdims=True) + w_inter * qn
        hh = num / jnp.maximum(jnp.abs(den), jnp.exp(-m_row))

        g_tot = bc_row[:, L - 1:L]
        a_row = g_tot - bc_row + li_row
        m_new = jnp.maximum(g_tot + m_prev, jnp.max(a_row, axis=1, keepdims=True))
        decay = jnp.exp(g_tot + m_prev - m_new)
        wa_row = jnp.exp(a_row - m_new)
        ktw = (kh.T * wa_row).astype(BF16)
        c_scr[h] = decay * c_scr[h] + _dot(ktw, vh)
        wa8 = jnp.broadcast_to(wa_row, (SUBLANES, L)).astype(BF16)
        n_scr[h] = decay * n_scr[h] + _dot(wa8, kb)[0:1, :]
        m_scr[h] = jnp.broadcast_to(m_new, (1, LANES))

        hm = hh * _sigmoid(o_ref[:, hs].astype(F32))
        hm = hm * lax.rsqrt(jnp.mean(hm * hm, axis=1, keepdims=True) + EPS) * ng_ref[:, hs]
        y_ref[:, hs] = (hm * _silu(z_ref[:, hs].astype(F32))).astype(BF16)


def _mlstm(p, g, gbias, conv_w, conv_b, norm_g, b, s):
    nt = s // TILE
    t = b * s

    def pcol(c):
        return pl.BlockSpec((TILE, ML_W), lambda bi, ti, c=c: (bi * nt + ti, c))

    full = lambda shape: pl.BlockSpec(shape, lambda bi, ti: (0,) * len(shape))
    return pl.pallas_call(
        _mlstm_kernel,
        grid=(b, nt),
        in_specs=[pcol(0), pcol(1), pcol(2), pcol(3), pcol(4),
                  pl.BlockSpec((TILE, LANES), lambda bi, ti: (bi * nt + ti, 0)),
                  full((1, LANES)), full((ML_CONV, 2 * ML_W)), full((1, 2 * ML_W)), full((1, ML_W))],
        out_specs=pl.BlockSpec((TILE, ML_W), lambda bi, ti: (bi * nt + ti, 0)),
        out_shape=jax.ShapeDtypeStruct((t, ML_W), BF16),
        scratch_shapes=[pltpu.VMEM((ML_HEADS, ML_HD, ML_HD), F32),
                        pltpu.VMEM((ML_HEADS, 1, ML_HD), F32),
                        pltpu.VMEM((ML_HEADS, 1, LANES), F32),
                        pltpu.VMEM((TILE + SUBLANES, 2 * ML_W), F32)],
        compiler_params=_cparams(("arbitrary", "arbitrary"), 40),
        name="mlstm",
    )(p, p, p, p, p, g, gbias, conv_w, conv_b, norm_g)


def _gla_kernel(q_ref, k_ref, v_ref, r_ref, g_ref, wa_ref, ba_ref, ng_ref, y_ref, st_scr):
    L = TILE
    H = GLA_HEADS

    @pl.when(pl.program_id(1) == 0)
    def _():
        st_scr[...] = jnp.zeros_like(st_scr)

    la = _log_sigmoid(_dot(g_ref[...].astype(BF16), wa_ref[...]) + ba_ref[...]) * (1.0 / GLA_TAU)
    r_io = lax.broadcasted_iota(I32, (L, L), 0)
    c_io = lax.broadcasted_iota(I32, (L, L), 1)
    tri_l = (c_io <= r_io).astype(BF16)
    bcum = _dot_f32_rhs(tri_l, la)
    btot = bcum[L - 1:L, :]
    bmid = bcum[L // 2 - 1:L // 2, :]

    q = q_ref[...].astype(F32) * (GLA_DK ** -0.5)
    k = k_ref[...].astype(F32)
    v = v_ref[...]
    q_in = (q * jnp.exp(bcum - bmid)).astype(BF16)
    k_in = k * jnp.exp(bmid - bcum)
    q_st = (q * jnp.exp(bcum)).astype(BF16)
    k_st = (k * jnp.exp(btot - bcum)).astype(BF16)

    kt = k_in.T
    kt4 = jnp.concatenate([kt] * H, axis=1)
    rr = lax.broadcasted_iota(I32, (GLA_WK, H * L), 0)
    cc = lax.broadcasted_iota(I32, (GLA_WK, H * L), 1)
    k_bd = jnp.where(rr // GLA_DK == cc // L, kt4, 0.0).astype(BF16)
    att = _dot(q_in, k_bd)
    ar = lax.broadcasted_iota(I32, (L, H * L), 0)
    ac = lax.broadcasted_iota(I32, (L, H * L), 1)
    att = jnp.where(ac % L <= ar, att, 0.0).astype(BF16)
    v4 = jnp.concatenate([v] * H, axis=0)
    vr = lax.broadcasted_iota(I32, (H * L, GLA_W), 0)
    vc = lax.broadcasted_iota(I32, (H * L, GLA_W), 1)
    v_bd = jnp.where(vr // L == vc // GLA_DV, v4, jnp.zeros_like(v4))
    o = _dot(att, v_bd) + _dot_nt(q_st, st_scr[...].astype(BF16))

    sr = lax.broadcasted_iota(I32, (GLA_W, GLA_WK), 0)
    sc = lax.broadcasted_iota(I32, (GLA_W, GLA_WK), 1)
    upd = _dot(v.astype(F32).T.astype(BF16), k_st)
    st_scr[...] = st_scr[...] * jnp.exp(btot) + jnp.where(sr // GLA_DV == sc // GLA_DK, upd, 0.0)

    for h in range(H):
        hs = slice(h * GLA_DV, (h + 1) * GLA_DV)
        oh = o[:, hs]
        oh = oh * lax.rsqrt(jnp.mean(oh * oh, axis=1, keepdims=True) + EPS) * ng_ref[:, hs]
        y_ref[:, hs] = (oh * _silu(r_ref[:, hs].astype(F32))).astype(BF16)


def _gla(p, g, wa_pad, b_a, norm_g, b, s):
    nt = s // TILE
    t = b * s

    def pcol(width, off):
        assert off % width == 0
        return pl.BlockSpec((TILE, width), lambda bi, ti: (bi * nt + ti, off // width))

    full = lambda shape: pl.BlockSpec(shape, lambda bi, ti: (0,) * len(shape))
    return pl.pallas_call(
        _gla_kernel,
        grid=(b, nt),
        in_specs=[pcol(GLA_WK, P_GLA_Q), pcol(GLA_WK, P_GLA_K), pcol(GLA_W, P_GLA_V), pcol(GLA_W, P_GLA_R),
                  pl.BlockSpec((TILE, LANES), lambda bi, ti: (bi * nt + ti, 0)),
                  full((LANES, GLA_WK)), full((1, GLA_WK)), full((1, GLA_W))],
        out_specs=pl.BlockSpec((TILE, GLA_W), lambda bi, ti: (bi * nt + ti, 0)),
        out_shape=jax.ShapeDtypeStruct((t, GLA_W), BF16),
        scratch_shapes=[pltpu.VMEM((GLA_W, GLA_WK), F32)],
        compiler_params=_cparams(("arbitrary", "arbitrary"), 32),
        name="gla",
    )(p, p, p, p, g, wa_pad, b_a, norm_g)


def _dsa_prep_kernel(q_ref, k_ref, v_ref, iq_ref, g_ref, qg_ref, kg_ref,
                     qt_ref, kn_ref, vt_ref, iqt_ref, ik_ref, iwt_ref):
    qn = []
    for h in range(DSA_HEADS):
        hs = slice(h * DSA_HD, (h + 1) * DSA_HD)
        q = q_ref[:, hs].astype(F32)
        k = k_ref[:, hs].astype(F32)
        qh = q * lax.rsqrt(jnp.mean(q * q, axis=1, keepdims=True) + EPS) * qg_ref[...]
        kh = k * lax.rsqrt(jnp.mean(k * k, axis=1, keepdims=True) + EPS) * kg_ref[...]
        qn.append(qh * (DSA_HD ** -0.5 * LOG2E))
        kn_ref[:, hs] = kh.astype(BF16)
        r0 = h * DSA_VROWS
        vt_ref[r0:r0 + DSA_HD, :] = v_ref[:, hs].astype(F32).T.astype(BF16)
        vt_ref[r0 + DSA_HD:r0 + DSA_VROWS, :] = jnp.ones((DSA_VROWS - DSA_HD, vt_ref.shape[1]), BF16)
    qt_ref[...] = jnp.concatenate(qn, axis=1).T.astype(BF16)
    iqt_ref[...] = iq_ref[...].astype(F32).T.astype(BF16)
    g = g_ref[...]
    ik_ref[...] = g[:, G_IDXK:G_IDXK + IDX_DIM].astype(BF16)
    iw_scale = (IDX_HEADS ** -0.5) * (IDX_DIM ** -0.5)
    iwt_ref[...] = g.T[G_IDXW:G_IDXW + IDX_HEADS, :] * iw_scale


def _dsa_prep(p, g, q_g, k_g, b, s):
    t = b * s
    tm = min(512, s)
    nt = s // tm
    full = lambda shape: pl.BlockSpec(shape, lambda bi, ti: (0,) * len(shape))
    tok = lambda width, cb: pl.BlockSpec((tm, width), lambda bi, ti: (bi * nt + ti, cb))
    feat = lambda rows: pl.BlockSpec((None, rows, tm), lambda bi, ti: (bi, 0, ti))
    return pl.pallas_call(
        _dsa_prep_kernel,
        grid=(b, nt),
        in_specs=[tok(DSA_W, P_DSA_Q // DSA_W), tok(DSA_W, P_DSA_K // DSA_W), tok(DSA_W, P_DSA_V // DSA_W),
                  tok(IDX_W, P_IDXQ // IDX_W), tok(LANES, 0), full((1, DSA_HD)), full((1, DSA_HD))],
        out_specs=[feat(DSA_W), tok(DSA_W, 0), feat(DSA_HEADS * DSA_VROWS), feat(IDX_W), tok(IDX_DIM, 0),
                   feat(IDX_HEADS)],
        out_shape=[jax.ShapeDtypeStruct((b, DSA_W, s), BF16),
                   jax.ShapeDtypeStruct((t, DSA_W), BF16),
                   jax.ShapeDtypeStruct((b, DSA_HEADS * DSA_VROWS, s), BF16),
                   jax.ShapeDtypeStruct((b, IDX_W, s), BF16),
                   jax.ShapeDtypeStruct((t, IDX_DIM), BF16),
                   jax.ShapeDtypeStruct((b, IDX_HEADS, s), F32)],
        compiler_params=_cparams(("arbitrary", "arbitrary"), 40),
        name="dsa_prep",
    )(p, p, p, p, g, q_g, k_g)


def _dsa_kernel(qt_ref, iqt_ref, iwt_ref, z_ref, kn_ref, vt_ref, ik_ref, bias_ref, y_ref,
                keys_scr, hi_scr, b1_scr, b0_scr, acc_scr, m_scr, iq2_scr, iw2_scr, s_scr, ml_scr, *, topk, idx_bits):
    Q, TK, SUB = TILE, DSA_TK, DSA_SUB
    i = pl.program_id(1)
    n_it = (i + SUB) // SUB
    qpos = i * Q + lax.broadcasted_iota(I32, (1, Q), 1)
    limit = (qpos // CHUNK + 1) * CHUNK
    krow = lax.broadcasted_iota(I32, (TK, Q), 0)

    def koff(kt):
        return pl.multiple_of(kt * TK, TK)

    for j in range(IDX_HEADS // 2):
        for e in range(2):
            h = 2 * j + e
            iq2_scr[j, :, e * Q:(e + 1) * Q] = iqt_ref[h * IDX_DIM:(h + 1) * IDX_DIM, :]
            iw2_scr[j, :, e * Q:(e + 1) * Q] = iwt_ref[h:h + 1, :]

    def score_tile(kt, last):
        off = koff(kt)
        ikt = ik_ref[pl.ds(off, TK), :]
        sc = jnp.zeros((TK, Q), F32)
        for j in range(IDX_HEADS // 2):
            w = jnp.maximum(_dot(ikt, iq2_scr[j]), 0.0) * iw2_scr[j]
            sc = sc + (w[:, 0:Q] + w[:, Q:2 * Q])
        bits = lax.bitcast_convert_type(sc, I32)
        keys = bits ^ ((bits >> 31) & 0x7FFFFFFF)
        hi = lax.bitcast_convert_type(bits & -65536, F32)
        if last:
            inadm = off + krow >= limit
            keys = jnp.where(inadm, INT_MIN, keys)
            hi = jnp.where(inadm, -jnp.inf, hi)
        keys_scr[pl.ds(off, TK), :] = keys
        hi_scr[pl.ds(off, TK), :] = hi.astype(BF16)
        b1_scr[pl.ds(off, TK), :] = ((keys >> 8) & 255).astype(F32).astype(BF16)
        b0_scr[pl.ds(off, TK), :] = (keys & 255).astype(F32).astype(BF16)

    def score_body(kt, carry):
        score_tile(kt, False)
        return carry

    lax.fori_loop(0, n_it - 1, score_body, 0)
    score_tile(n_it - 1, True)

    CR = 8 * SUBLANES
    CRP = 8 * 2 * SUBLANES

    def count(pred):
        def body(kt, acc):
            off = koff(kt)
            ind = jnp.where(pred(keys_scr[pl.ds(off, TK), :], off), 1.0, 0.0)
            return acc + jnp.sum(ind.reshape(TK // CR, CR, Q), axis=0)
        acc = lax.fori_loop(0, n_it, body, jnp.zeros((CR, Q), F32))
        return jnp.sum(acc, axis=0, keepdims=True)

    one_p, zero_p = jnp.ones((), BF16), jnp.zeros((), BF16)

    def count_p(ref, pred):
        def body(kt, acc):
            ind = jnp.where(pred(ref[pl.ds(koff(kt), TK), :]), one_p, zero_p)
            for r in range(TK // CRP):
                acc = acc + ind[r * CRP:(r + 1) * CRP, :]
            return acc
        acc = lax.fori_loop(0, n_it, body, jnp.zeros((CRP, Q), BF16))
        return jnp.sum(acc.astype(F32), axis=0, keepdims=True)

    def rewrite_p(dst, fn):
        def body(kt, carry):
            off = koff(kt)
            dst[pl.ds(off, TK), :] = fn(off)
            return carry
        lax.fori_loop(0, n_it, body, 0)

    def coarse_value(c):
        b16 = c ^ ((c >> 31) & 0x7FFF)
        return lax.bitcast_convert_type(b16 << 16, F32).astype(BF16)

    def coarse_bit(b, c):
        cand = c + lax.shift_left(jnp.int32(1), 15 - b)
        cv = coarse_value(cand)
        return jnp.where(count_p(hi_scr, lambda x: x >= cv) >= topk, cand, c)

    c_hi = lax.fori_loop(0, 16, coarse_bit, jnp.full((1, Q), -32768, I32))
    c_hi = jnp.where((c_hi > 0) & (c_hi < 128), 0, c_hi)
    v_hi = coarse_value(c_hi)
    need = topk - count_p(hi_scr, lambda x: x > v_hi)

    def byte_search(ref, need):
        def bit(b, d):
            cand = d + lax.shift_left(jnp.int32(1), 7 - b)
            cv = cand.astype(F32).astype(BF16)
            return jnp.where(count_p(ref, lambda x: x >= cv) >= need, cand, d)
        return lax.fori_loop(0, 8, bit, jnp.zeros((1, Q), I32))

    minus1 = jnp.full((), -1.0, BF16)
    rewrite_p(b1_scr, lambda off: jnp.where(hi_scr[pl.ds(off, TK), :] == v_hi, b1_scr[pl.ds(off, TK), :], minus1))
    d1 = byte_search(b1_scr, need)
    v1b = d1.astype(F32).astype(BF16)
    need = need - count_p(b1_scr, lambda x: x > v1b)
    rewrite_p(b0_scr, lambda off: jnp.where(b1_scr[pl.ds(off, TK), :] == v1b, b0_scr[pl.ds(off, TK), :], minus1))
    d0 = byte_search(b0_scr, need)
    thr = (c_hi << 16) | (d1 << 8) | d0
    thr = jnp.where(c_hi == -32768, INT_MIN + 1, thr)
    n_ge = count(lambda kk, off: kk >= thr)

    @pl.when(jnp.max(n_ge) > topk)
    def _():
        need = topk - count(lambda kk, off: kk > thr)

        def idx_bit(b, j):
            cand = j + lax.shift_left(jnp.int32(1), idx_bits - 1 - b)
            n_before = count(lambda kk, off: (kk == thr) & (off + krow < cand))
            return jnp.where(n_before < need, cand, j)

        last = lax.fori_loop(0, idx_bits, idx_bit, jnp.zeros((1, Q), I32))

        def drop(kt, carry):
            off = koff(kt)
            kk = keys_scr[pl.ds(off, TK), :]
            keys_scr[pl.ds(off, TK), :] = jnp.where((kk == thr) & (off + krow > last), INT_MIN, kk)
            return carry

        lax.fori_loop(0, n_it, drop, 0)

    m_scr[...] = jnp.full(m_scr.shape, NEG_BIG, F32)
    acc_scr[...] = jnp.zeros_like(acc_scr)

    G = DSA_GROUP
    n_grp = (n_it + G - 1) // G

    def logits(kt, buf, thr_kt):
        off = koff(kt)
        sel = keys_scr[pl.ds(off, TK), :] >= thr_kt
        bsel = [jnp.clip(i - (kt * SUB + j), 0, 2) for j in range(SUB)]
        for h in range(DSA_HEADS):
            hs = slice(h * DSA_HD, (h + 1) * DSA_HD)
            s = _dot(kn_ref[pl.ds(off, TK), hs], qt_ref[hs, :])
            s = s + jnp.concatenate([bias_ref[h, bsel[j]] for j in range(SUB)], axis=0)
            s = jnp.where(sel, s, NEG_BIG)
            s_scr[buf, h] = s
            ml_scr[buf, h] = jnp.max(s, axis=0, keepdims=True)

    def values(kt, buf):
        off = koff(kt)
        for h in range(DSA_HEADS):
            m_old = m_scr[h]
            m_new = jnp.maximum(m_old, ml_scr[buf, h])
            p = jnp.exp2(s_scr[buf, h] - m_new).astype(BF16)
            v1 = vt_ref[h * DSA_VROWS:(h + 1) * DSA_VROWS, pl.ds(off, TK)]
            acc_scr[h] = jnp.exp2(m_old - m_new) * acc_scr[h] + _dot(v1, p)
            m_scr[h] = m_new

    def att_group(gi, carry):
        steps = []
        for e in range(G):
            kt = gi * G + e
            steps.append((jnp.minimum(kt, n_it - 1), jnp.where(kt < n_it, thr, jnp.int32(2 ** 31 - 1))))
        for e in range(G):
            logits(steps[e][0], e, steps[e][1])
        for e in range(G):
            values(steps[e][0], e)
        return carry

    lax.fori_loop(0, n_grp, att_group, 0)
    for h in range(DSA_HEADS):
        hs = slice(h * DSA_HD, (h + 1) * DSA_HD)
        acc = acc_scr[h]
        out = (acc[0:DSA_HD, :] / acc[DSA_HD:DSA_HD + 1, :]).T
        y_ref[:, hs] = (out * _silu(z_ref[:, hs].astype(F32))).astype(BF16)


def _dsa(p, qt, kn, vt, iqt, ik, iwt, bias_tiles, b, s):
    assert s % DSA_TK == 0 and s // DSA_TK * (DSA_TK // (16 * SUBLANES)) <= 256
    nt = s // TILE
    t = b * s
    topk = min(TOPK_MAX, s // 4)
    idx_bits = max(1, (s - 1).bit_length())
    kernel = functools.partial(_dsa_kernel, topk=topk, idx_bits=idx_bits)
    once = pl.Buffered(1)
    qfeat = lambda rows: pl.BlockSpec((None, rows, TILE), lambda bi, ti: (bi, 0, ti))
    return pl.pallas_call(
        kernel,
        grid=(b, nt),
        in_specs=[qfeat(DSA_W), qfeat(IDX_W), qfeat(IDX_HEADS),
                  pl.BlockSpec((TILE, DSA_W), lambda bi, ti: (bi * nt + ti, P_DSA_Z // DSA_W)),
                  pl.BlockSpec((s, DSA_W), lambda bi, ti: (bi, 0), pipeline_mode=once),
                  pl.BlockSpec((None, DSA_HEADS * DSA_VROWS, s), lambda bi, ti: (bi, 0, 0), pipeline_mode=once),
                  pl.BlockSpec((s, IDX_DIM), lambda bi, ti: (bi, 0), pipeline_mode=once),
                  pl.BlockSpec((DSA_HEADS, 3, TILE, TILE), lambda bi, ti: (0, 0, 0, 0))],
        out_specs=pl.BlockSpec((TILE, DSA_W), lambda bi, ti: (bi * nt + ti, 0)),
        out_shape=jax.ShapeDtypeStruct((t, DSA_W), BF16),
        scratch_shapes=[pltpu.VMEM((s, TILE), I32),
                        pltpu.VMEM((s, TILE), BF16), pltpu.VMEM((s, TILE), BF16), pltpu.VMEM((s, TILE), BF16),
                        pltpu.VMEM((DSA_HEADS, DSA_VROWS, TILE), F32),
                        pltpu.VMEM((DSA_HEADS, 1, TILE), F32),
                        pltpu.VMEM((IDX_HEADS // 2, IDX_DIM, 2 * TILE), BF16),
                        pltpu.VMEM((IDX_HEADS // 2, 1, 2 * TILE), F32),
                        pltpu.VMEM((DSA_GROUP, DSA_HEADS, DSA_TK, TILE), F32),
                        pltpu.VMEM((DSA_GROUP, DSA_HEADS, 1, TILE), F32)],
        compiler_params=_cparams(("arbitrary", "arbitrary"), 48),
        name="dsa",
    )(qt, iqt, iwt, p, kn, vt, ik, bias_tiles)


def _out_proj_kernel(yml_ref, ygla_ref, ydsa_ref, w_ref, x_ref, o_ref):
    y = _dot(yml_ref[...], w_ref[0:ML_W, :])
    y = y + _dot(ygla_ref[...], w_ref[ML_W:ML_W + GLA_W, :])
    y = y + _dot(ydsa_ref[...], w_ref[ML_W + GLA_W:D_MIX, :])
    o_ref[...] = x_ref[...] + y


def _out_proj(y_ml, y_gla, y_dsa, w_out, x2d):
    t = x2d.shape[0]
    tm = min(512, t)
    row = lambda width: pl.BlockSpec((tm, width), lambda i: (i, 0))
    return pl.pallas_call(
        _out_proj_kernel,
        grid=(t // tm,),
        in_specs=[row(ML_W), row(GLA_W), row(DSA_W),
                  pl.BlockSpec((D_MIX, D_MODEL), lambda i: (0, 0)), row(D_MODEL)],
        out_specs=row(D_MODEL),
        out_shape=jax.ShapeDtypeStruct((t, D_MODEL), F32),
        compiler_params=_cparams(("arbitrary",), 40),
        name="out_proj",
    )(y_ml, y_gla, y_dsa, w_out, x2d)


_COLUMN_LAYOUT = (
    ('ml_q', ML_W), ('ml_k', ML_W), ('ml_v', ML_W), ('ml_o', ML_W), ('ml_z', ML_W),
    ('ml_i', ML_HEADS), ('ml_f', ML_HEADS),
    ('gla_q', GLA_WK), ('gla_k', GLA_WK), ('gla_v', GLA_W), ('gla_a', GLA_RANK), ('gla_r', GLA_W),
    ('dsa_q', DSA_W), ('dsa_k', DSA_W), ('dsa_v', DSA_W), ('dsa_z', DSA_W),
    ('idx_q', IDX_W), ('idx_k', IDX_DIM), ('idx_w', IDX_HEADS),
)


def _split_w_in(w_in):
    cols, off = {}, 0
    for name, width in _COLUMN_LAYOUT:
        cols[name] = w_in[:, off:off + width]
        off += width
    return cols


def _pack_w_in(w_in):
    c = _split_w_in(w_in)
    d = w_in.shape[0]
    wbig = jnp.concatenate([c['ml_q'], c['ml_k'], c['ml_v'], c['ml_o'], c['ml_z'], c['idx_q'],
                            c['gla_q'], c['gla_k'], c['gla_v'], c['gla_r'],
                            c['dsa_q'], c['dsa_k'], c['dsa_v'], c['dsa_z']], axis=1)
    assert wbig.shape[1] == P_COLS
    z = lambda n: jnp.zeros((d, n), w_in.dtype)
    wsmall = jnp.concatenate([c['idx_k'], c['ml_i'], z(4), c['ml_f'], z(4), c['gla_a'], c['idx_w'],
                              z(LANES - G_IDXW - IDX_HEADS)], axis=1)
    assert wsmall.shape[1] == LANES
    return wbig.astype(BF16), wsmall.astype(BF16)


def _rel_bucket(rel):
    half = REL_BUCKETS // 2
    max_exact = half // 2
    ret = jnp.where(rel > 0, half, 0)
    n = jnp.abs(rel)
    nf = jnp.maximum(n, 1).astype(F32)
    large = max_exact + (jnp.log(nf / max_exact) / math.log(REL_MAX_DIST / max_exact)
                         * (half - max_exact)).astype(I32)
    large = jnp.minimum(large, half - 1)
    return ret + jnp.where(n < max_exact, n, large)


def _bias_tiles(rel_bias):
    assert REL_MAX_DIST <= TILE
    kl = jnp.arange(TILE, dtype=I32)[:, None]
    ql = jnp.arange(TILE, dtype=I32)[None, :]
    far = jnp.full((TILE, TILE), -(TILE + 1), I32)
    rel = jnp.stack([kl - ql, kl - TILE - ql, far])
    tab = jnp.transpose(rel_bias.astype(F32)[_rel_bucket(rel)], (3, 0, 1, 2))
    return (tab - tab[:, 2:3]) * LOG2E


def _layer(x2d, b, s, norm_g, w_in, ml_conv_w, ml_conv_b, ml_b_i, ml_b_f, ml_norm_g,
           gla_w_a, gla_b_a, gla_norm_g, dsa_q_g, dsa_k_g, w_out, bias_tiles):
    wbig, wsmall = _pack_w_in(w_in)
    p, g = _in_proj(x2d, norm_g.reshape(1, D_MODEL), wbig, wsmall)

    gbias = jnp.zeros((1, LANES), F32)
    gbias = gbias.at[0, G_MLI:G_MLI + ML_HEADS].set(ml_b_i).at[0, G_MLF:G_MLF + ML_HEADS].set(ml_b_f)
    y_ml = _mlstm(p, g, gbias, ml_conv_w, ml_conv_b.reshape(1, -1), ml_norm_g.reshape(1, -1), b, s)

    wa_pad = jnp.zeros((LANES, GLA_WK), F32).at[G_GLAA:G_GLAA + GLA_RANK].set(gla_w_a).astype(BF16)
    y_gla = _gla(p, g, wa_pad, gla_b_a.reshape(1, -1), gla_norm_g.reshape(1, -1), b, s)

    qt, kn, vt, iqt, ik, iwt = _dsa_prep(p, g, dsa_q_g.reshape(1, -1), dsa_k_g.reshape(1, -1), b, s)
    y_dsa = _dsa(p, qt, kn, vt, iqt, ik, iwt, bias_tiles, b, s)

    return _out_proj(y_ml, y_gla, y_dsa, w_out.astype(BF16), x2d)


def kernel(x, norm_g, w_in, ml_conv_w, ml_conv_b, ml_b_i, ml_b_f, ml_norm_g, gla_w_a, gla_b_a, gla_norm_g,
           dsa_q_g, dsa_k_g, w_out, rel_bias):
    b, s, d = x.shape
    assert d == D_MODEL and s % TILE == 0
    depth = norm_g.shape[0]
    bias_tiles = _bias_tiles(rel_bias)
    x2d = x.reshape(b * s, d)
    for layer in range(depth):
        x2d = _layer(x2d, b, s, norm_g[layer], w_in[layer], ml_conv_w[layer], ml_conv_b[layer],
                     ml_b_i[layer], ml_b_f[layer], ml_norm_g[layer], gla_w_a[layer], gla_b_a[layer],
                     gla_norm_g[layer], dsa_q_g[layer], dsa_k_g[layer], w_out[layer], bias_tiles)
    return x2d.reshape(b, s, d)
```

```python
import functools
import math

import jax
import jax.numpy as jnp
from jax import lax
from jax.experimental import pallas as pl
from jax.experimental.pallas import tpu as pltpu

F32 = jnp.float32
BF16 = jnp.bfloat16
I32 = jnp.int32

D_MODEL = 1024
CHUNK = 64
EPS = 1e-6
ML_HEADS, ML_HD = 4, 256
ML_W = ML_HEADS * ML_HD
ML_CONV = 4
GLA_HEADS, GLA_DK, GLA_DV = 4, 64, 128
GLA_WK = GLA_HEADS * GLA_DK
GLA_W = GLA_HEADS * GLA_DV
GLA_RANK = 16
GLA_TAU = 16.0
DSA_HEADS, DSA_HD = 4, 128
DSA_W = DSA_HEADS * DSA_HD
IDX_HEADS, IDX_DIM = 8, 64
IDX_W = IDX_HEADS * IDX_DIM
TOPK_MAX = 256
REL_BUCKETS, REL_MAX_DIST = 32, 128
D_MIX = ML_W + GLA_W + DSA_W

LANES = 128
SUBLANES = 8

TILE = 128
DSA_TK = 512
DSA_SUB = DSA_TK // TILE
DSA_GROUP = 2
DSA_VROWS = DSA_HD + 16
LOG2E = 1.4426950408889634

P_ML = 0
P_IDXQ = 5 * ML_W
P_GLA_Q = P_IDXQ + IDX_W
P_GLA_K = P_GLA_Q + GLA_WK
P_GLA_V = P_GLA_K + GLA_WK
P_GLA_R = P_GLA_V + GLA_W
P_DSA_Q = P_GLA_R + GLA_W
P_DSA_K = P_DSA_Q + DSA_W
P_DSA_V = P_DSA_K + DSA_W
P_DSA_Z = P_DSA_V + DSA_W
P_COLS = P_DSA_Z + DSA_W
G_IDXK = 0
G_MLI = 64
G_MLF = 72
G_GLAA = 80
G_IDXW = 96

INT_MIN = -(2 ** 31)
NEG_BIG = -1e30


def _cparams(sem, vmem_mb):
    return pltpu.CompilerParams(dimension_semantics=sem, vmem_limit_bytes=vmem_mb << 20)


def _dot(a, b):
    return jnp.dot(a, b, preferred_element_type=F32)


def _dot_nt(a, b):
    return lax.dot_general(a, b, (((1,), (1,)), ((), ())), preferred_element_type=F32)


def _split3(x):
    hi = x.astype(BF16)
    r1 = x - hi.astype(F32)
    mid = r1.astype(BF16)
    lo = (r1 - mid.astype(F32)).astype(BF16)
    return hi, mid, lo


def _dot_f32_lhs(x, m01):
    hi, mid, lo = _split3(x)
    return _dot(hi, m01) + _dot(mid, m01) + _dot(lo, m01)


def _dot_f32_rhs(m01, x):
    hi, mid, lo = _split3(x)
    return _dot(m01, hi) + _dot(m01, mid) + _dot(m01, lo)


def _log_sigmoid(x):
    return jnp.minimum(x, 0.0) - jnp.log(1.0 + jnp.exp(-jnp.abs(x)))


def _sigmoid(x):
    return 1.0 / (1.0 + jnp.exp(-x))


def _silu(x):
    return x * _sigmoid(x)


def _in_proj_kernel(x_ref, g_ref, wb_ref, ws_ref, p_ref, gate_ref, xn_ref):
    @pl.when(pl.program_id(1) == 0)
    def _():
        x = x_ref[...]
        ms = jnp.mean(x * x, axis=-1, keepdims=True)
        xn = (x * lax.rsqrt(ms + EPS) * g_ref[...]).astype(BF16)
        xn_ref[...] = xn
        gate_ref[...] = _dot(xn, ws_ref[...])

    p_ref[...] = _dot(xn_ref[...], wb_ref[...]).astype(BF16)


def _in_proj(x2d, g, wbig, wsmall):
    t = x2d.shape[0]
    tm = min(512, t)
    tn = P_COLS // 4
    return pl.pallas_call(
        _in_proj_kernel,
        grid=(t // tm, P_COLS // tn),
        in_specs=[
            pl.BlockSpec((tm, D_MODEL), lambda i, j: (i, 0)),
            pl.BlockSpec((1, D_MODEL), lambda i, j: (0, 0)),
            pl.BlockSpec((D_MODEL, tn), lambda i, j: (0, j)),
            pl.BlockSpec((D_MODEL, LANES), lambda i, j: (0, 0)),
        ],
        out_specs=[
            pl.BlockSpec((tm, tn), lambda i, j: (i, j)),
            pl.BlockSpec((tm, LANES), lambda i, j: (i, 0)),
        ],
        out_shape=[jax.ShapeDtypeStruct((t, P_COLS), BF16), jax.ShapeDtypeStruct((t, LANES), F32)],
        scratch_shapes=[pltpu.VMEM((tm, D_MODEL), BF16)],
        compiler_params=_cparams(("arbitrary", "arbitrary"), 48),
        name="in_proj",
    )(x2d, g, wbig, wsmall)


def _mlstm_kernel(q_ref, k_ref, v_ref, o_ref, z_ref, g_ref, gb_ref, cw_ref, cb_ref, ng_ref,
                  y_ref, c_scr, n_scr, m_scr, xbuf):
    L = TILE

    @pl.when(pl.program_id(1) == 0)
    def _():
        c_scr[...] = jnp.zeros_like(c_scr)
        n_scr[...] = jnp.zeros_like(n_scr)
        m_scr[...] = jnp.zeros_like(m_scr)
        xbuf[0:SUBLANES, :] = jnp.zeros((SUBLANES, 2 * ML_W), F32)

    xbuf[SUBLANES:SUBLANES + L, 0:ML_W] = q_ref[...].astype(F32)
    xbuf[SUBLANES:SUBLANES + L, ML_W:2 * ML_W] = k_ref[...].astype(F32)
    conv = jnp.broadcast_to(cb_ref[...], (L, 2 * ML_W))
    for j in range(ML_CONV):
        off = SUBLANES - (ML_CONV - 1) + j
        conv = conv + cw_ref[j:j + 1, :] * xbuf[off:off + L, :]
    xbuf[0:SUBLANES, :] = xbuf[L:L + SUBLANES, :]
    qk = _silu(conv)

    gate = g_ref[...] + gb_ref[...]
    lf_cols = _log_sigmoid(gate)
    gate_t = gate.T
    li_rows = gate_t[G_MLI:G_MLI + SUBLANES, :]
    lf_rows = _log_sigmoid(gate_t[G_MLF:G_MLF + SUBLANES, :])
    r_io = lax.broadcasted_iota(I32, (L, L), 0)
    c_io = lax.broadcasted_iota(I32, (L, L), 1)
    causal = c_io <= r_io
    tri_u = (r_io <= c_io).astype(BF16)
    tri_l = causal.astype(BF16)
    bc_rows = _dot_f32_lhs(lf_rows, tri_u)
    bc_cols = _dot_f32_rhs(tri_l, lf_cols)

    for h in range(ML_HEADS):
        hs = slice(h * ML_HD, (h + 1) * ML_HD)
        qh = qk[:, hs]
        kh = qk[:, ML_W + h * ML_HD:ML_W + (h + 1) * ML_HD] * (ML_HD ** -0.5)
        vh = v_ref[:, hs]
        qb = qh.astype(BF16)
        kb = kh.astype(BF16)
        bc_col = bc_cols[:, G_MLF + h:G_MLF + h + 1]
        bc_row = bc_rows[h:h + 1, :]
        li_row = li_rows[h:h + 1, :]
        m_prev = m_scr[h, :, 0:1]

        dmat = jnp.where(causal, bc_col - (bc_row - li_row), -jnp.inf)
        inter = bc_col + m_prev
        m_row = jnp.maximum(inter, jnp.max(dmat, axis=1, keepdims=True))
        w_intra = jnp.exp(dmat - m_row)
        w_inter = jnp.exp(inter - m_row)
        s_qk = _dot_nt(qb, kb) * w_intra
        num = _dot(s_qk.astype(BF16), vh) + w_inter * _dot(qb, c_scr[h].astype(BF16))
        qn = jnp.sum(qh * n_scr[h], axis=1, keepdims=True)
        den = jnp.sum(s_qk, axis=1, keepdims=True) + w_inter * qn
        hh = num / jnp.maximum(jnp.abs(den), jnp.exp(-m_row))

        g_tot = bc_row[:, L - 1:L]
        a_row = g_tot - bc_row + li_row
        m_new = jnp.maximum(g_tot + m_prev, jnp.max(a_row, axis=1, keepdims=True))
        decay = jnp.exp(g_tot + m_prev - m_new)
        wa_row = jnp.exp(a_row - m_new)
        ktw = (kh.T * wa_row).astype(BF16)
        c_scr[h] = decay * c_scr[h] + _dot(ktw, vh)
        wa8 = jnp.broadcast_to(wa_row, (SUBLANES, L)).astype(BF16)
        n_scr[h] = decay * n_scr[h] + _dot(wa8, kb)[0:1, :]
        m_scr[h] = jnp.broadcast_to(m_new, (1, LANES))

        hm = hh * _sigmoid(o_ref[:, hs].astype(F32))
        hm = hm * lax.rsqrt(jnp.mean(hm * hm, axis=1, keepdims=True) + EPS) * ng_ref[:, hs]
        y_ref[:, hs] = (hm * _silu(z_ref[:, hs].astype(F32))).astype(BF16)


def _mlstm(p, g, gbias, conv_w, conv_b, norm_g, b, s):
    nt = s // TILE
    t = b * s

    def pcol(c):
        return pl.BlockSpec((TILE, ML_W), lambda bi, ti, c=c: (bi * nt + ti, c))

    full = lambda shape: pl.BlockSpec(shape, lambda bi, ti: (0,) * len(shape))
    return pl.pallas_call(
        _mlstm_kernel,
        grid=(b, nt),
        in_specs=[pcol(0), pcol(1), pcol(2), pcol(3), pcol(4),
                  pl.BlockSpec((TILE, LANES), lambda bi, ti: (bi * nt + ti, 0)),
                  full((1, LANES)), full((ML_CONV, 2 * ML_W)), full((1, 2 * ML_W)), full((1, ML_W))],
        out_specs=pl.BlockSpec((TILE, ML_W), lambda bi, ti: (bi * nt + ti, 0)),
        out_shape=jax.ShapeDtypeStruct((t, ML_W), BF16),
        scratch_shapes=[pltpu.VMEM((ML_HEADS, ML_HD, ML_HD), F32),
                        pltpu.VMEM((ML_HEADS, 1, ML_HD), F32),
                        pltpu.VMEM((ML_HEADS, 1, LANES), F32),
                        pltpu.VMEM((TILE + SUBLANES, 2 * ML_W), F32)],
        compiler_params=_cparams(("arbitrary", "arbitrary"), 40),
        name="mlstm",
    )(p, p, p, p, p, g, gbias, conv_w, conv_b, norm_g)


def _gla_kernel(q_ref, k_ref, v_ref, r_ref, g_ref, wa_ref, ba_ref, ng_ref, y_ref, st_scr):
    L = TILE
    H = GLA_HEADS

    @pl.when(pl.program_id(1) == 0)
    def _():
        st_scr[...] = jnp.zeros_like(st_scr)

    la = _log_sigmoid(_dot(g_ref[...].astype(BF16), wa_ref[...]) + ba_ref[...]) * (1.0 / GLA_TAU)
    r_io = lax.broadcasted_iota(I32, (L, L), 0)
    c_io = lax.broadcasted_iota(I32, (L, L), 1)
    tri_l = (c_io <= r_io).astype(BF16)
    bcum = _dot_f32_rhs(tri_l, la)
    btot = bcum[L - 1:L, :]
    bmid = bcum[L // 2 - 1:L // 2, :]

    q = q_ref[...].astype(F32) * (GLA_DK ** -0.5)
    k = k_ref[...].astype(F32)
    v = v_ref[...]
    q_in = (q * jnp.exp(bcum - bmid)).astype(BF16)
    k_in = k * jnp.exp(bmid - bcum)
    q_st = (q * jnp.exp(bcum)).astype(BF16)
    k_st = (k * jnp.exp(btot - bcum)).astype(BF16)

    kt = k_in.T
    kt4 = jnp.concatenate([kt] * H, axis=1)
    rr = lax.broadcasted_iota(I32, (GLA_WK, H * L), 0)
    cc = lax.broadcasted_iota(I32, (GLA_WK, H * L), 1)
    k_bd = jnp.where(rr // GLA_DK == cc // L, kt4, 0.0).astype(BF16)
    att = _dot(q_in, k_bd)
    ar = lax.broadcasted_iota(I32, (L, H * L), 0)
    ac = lax.broadcasted_iota(I32, (L, H * L), 1)
    att = jnp.where(ac % L <= ar, att, 0.0).astype(BF16)
    v4 = jnp.concatenate([v] * H, axis=0)
    vr = lax.broadcasted_iota(I32, (H * L, GLA_W), 0)
    vc = lax.broadcasted_iota(I32, (H * L, GLA_W), 1)
    v_bd = jnp.where(vr // L == vc // GLA_DV, v4, jnp.zeros_like(v4))
    o = _dot(att, v_bd) + _dot_nt(q_st, st_scr[...].astype(BF16))

    sr = lax.broadcasted_iota(I32, (GLA_W, GLA_WK), 0)
    sc = lax.broadcasted_iota(I32, (GLA_W, GLA_WK), 1)
    upd = _dot(v.astype(F32).T.astype(BF16), k_st)
    st_scr[...] = st_scr[...] * jnp.exp(btot) + jnp.where(sr // GLA_DV == sc // GLA_DK, upd, 0.0)

    for h in range(H):
        hs = slice(h * GLA_DV, (h + 1) * GLA_DV)
        oh = o[:, hs]
        oh = oh * lax.rsqrt(jnp.mean(oh * oh, axis=1, keepdims=True) + EPS) * ng_ref[:, hs]
        y_ref[:, hs] = (oh * _silu(r_ref[:, hs].astype(F32))).astype(BF16)


def _gla(p, g, wa_pad, b_a, norm_g, b, s):
    nt = s // TILE
    t = b * s

    def pcol(width, off):
        assert off % width == 0
        return pl.BlockSpec((TILE, width), lambda bi, ti: (bi * nt + ti, off // width))

    full = lambda shape: pl.BlockSpec(shape, lambda bi, ti: (0,) * len(shape))
    return pl.pallas_call(
        _gla_kernel,
        grid=(b, nt),
        in_specs=[pcol(GLA_WK, P_GLA_Q), pcol(GLA_WK, P_GLA_K), pcol(GLA_W, P_GLA_V), pcol(GLA_W, P_GLA_R),
                  pl.BlockSpec((TILE, LANES), lambda bi, ti: (bi * nt + ti, 0)),
                  full((LANES, GLA_WK)), full((1, GLA_WK)), full((1, GLA_W))],
        out_specs=pl.BlockSpec((TILE, GLA_W), lambda bi, ti: (bi * nt + ti, 0)),
        out_shape=jax.ShapeDtypeStruct((t, GLA_W), BF16),
        scratch_shapes=[pltpu.VMEM((GLA_W, GLA_WK), F32)],
        compiler_params=_cparams(("arbitrary", "arbitrary"), 32),
        name="gla",
    )(p, p, p, p, g, wa_pad, b_a, norm_g)


def _dsa_prep_kernel(q_ref, k_ref, v_ref, iq_ref, g_ref, qg_ref, kg_ref,
                     qt_ref, kn_ref, vt_ref, iqt_ref, ik_ref, iwt_ref):
    qn = []
    for h in range(DSA_HEADS):
        hs = slice(h * DSA_HD, (h + 1) * DSA_HD)
        q = q_ref[:, hs].astype(F32)
        k = k_ref[:, hs].astype(F32)
        qh = q * lax.rsqrt(jnp.mean(q * q, axis=1, keepdims=True) + EPS) * qg_ref[...]
        kh = k * lax.rsqrt(jnp.mean(k * k, axis=1, keepdims=True) + EPS) * kg_ref[...]
        qn.append(qh * (DSA_HD ** -0.5 * LOG2E))
        kn_ref[:, hs] = kh.astype(BF16)
        r0 = h * DSA_VROWS
        vt_ref[r0:r0 + DSA_HD, :] = v_ref[:, hs].astype(F32).T.astype(BF16)
        vt_ref[r0 + DSA_HD:r0 + DSA_VROWS, :] = jnp.ones((DSA_VROWS - DSA_HD, vt_ref.shape[1]), BF16)
    qt_ref[...] = jnp.concatenate(qn, axis=1).T.astype(BF16)
    iqt_ref[...] = iq_ref[...].astype(F32).T.astype(BF16)
    g = g_ref[...]
    ik_ref[...] = g[:, G_IDXK:G_IDXK + IDX_DIM].astype(BF16)
    iw_scale = (IDX_HEADS ** -0.5) * (IDX_DIM ** -0.5)
    iwt_ref[...] = g.T[G_IDXW:G_IDXW + IDX_HEADS, :] * iw_scale


def _dsa_prep(p, g, q_g, k_g, b, s):
    t = b * s
    tm = min(512, s)
    nt = s // tm
    full = lambda shape: pl.BlockSpec(shape, lambda bi, ti: (0,) * len(shape))
    tok = lambda width, cb: pl.BlockSpec((tm, width), lambda bi, ti: (bi * nt + ti, cb))
    feat = lambda rows: pl.BlockSpec((None, rows, tm), lambda bi, ti: (bi, 0, ti))
    return pl.pallas_call(
        _dsa_prep_kernel,
        grid=(b, nt),
        in_specs=[tok(DSA_W, P_DSA_Q // DSA_W), tok(DSA_W, P_DSA_K // DSA_W), tok(DSA_W, P_DSA_V // DSA_W),
                  tok(IDX_W, P_IDXQ // IDX_W), tok(LANES, 0), full((1, DSA_HD)), full((1, DSA_HD))],
        out_specs=[feat(DSA_W), tok(DSA_W, 0), feat(DSA_HEADS * DSA_VROWS), feat(IDX_W), tok(IDX_DIM, 0),
                   feat(IDX_HEADS)],
        out_shape=[jax.ShapeDtypeStruct((b, DSA_W, s), BF16),
                   jax.ShapeDtypeStruct((t, DSA_W), BF16),
                   jax.ShapeDtypeStruct((b, DSA_HEADS * DSA_VROWS, s), BF16),
                   jax.ShapeDtypeStruct((b, IDX_W, s), BF16),
                   jax.ShapeDtypeStruct((t, IDX_DIM), BF16),
                   jax.ShapeDtypeStruct((b, IDX_HEADS, s), F32)],
        compiler_params=_cparams(("arbitrary", "arbitrary"), 40),
        name="dsa_prep",
    )(p, p, p, p, g, q_g, k_g)


def _dsa_kernel(qt_ref, iqt_ref, iwt_ref, z_ref, kn_ref, vt_ref, ik_ref, bias_ref, y_ref,
                keys_scr, hi_scr, b1_scr, b0_scr, acc_scr, m_scr, iq2_scr, iw2_scr, s_scr, ml_scr, *, topk, idx_bits):
    Q, TK, SUB = TILE, DSA_TK, DSA_SUB
    i = pl.program_id(1)
    n_it = (i + SUB) // SUB
    qpos = i * Q + lax.broadcasted_iota(I32, (1, Q), 1)
    limit = (qpos // CHUNK + 1) * CHUNK
    krow = lax.broadcasted_iota(I32, (TK, Q), 0)

    def koff(kt):
        return pl.multiple_of(kt * TK, TK)

    for j in range(IDX_HEADS // 2):
        for e in range(2):
            h = 2 * j + e
            iq2_scr[j, :, e * Q:(e + 1) * Q] = iqt_ref[h * IDX_DIM:(h + 1) * IDX_DIM, :]
            iw2_scr[j, :, e * Q:(e + 1) * Q] = iwt_ref[h:h + 1, :]

    def score_tile(kt, last):
        off = koff(kt)
        ikt = ik_ref[pl.ds(off, TK), :]
        sc = jnp.zeros((TK, Q), F32)
        for j in range(IDX_HEADS // 2):
            w = jnp.maximum(_dot(ikt, iq2_scr[j]), 0.0) * iw2_scr[j]
            sc = sc + (w[:, 0:Q] + w[:, Q:2 * Q])
        bits = lax.bitcast_convert_type(sc, I32)
        keys = bits ^ ((bits >> 31) & 0x7FFFFFFF)
        hi = lax.bitcast_convert_type(bits & -65536, F32)
        if last:
            inadm = off + krow >= limit
            keys = jnp.where(inadm, INT_MIN, keys)
            hi = jnp.where(inadm, -jnp.inf, hi)
        keys_scr[pl.ds(off, TK), :] = keys

    def score_body(kt, carry):
        score_tile(kt, False)
        return carry

    lax.fori_loop(0, n_it - 1, score_body, 0)
    score_tile(n_it - 1, True)

    CR = 8 * SUBLANES
    CRP = 8 * 2 * SUBLANES

    def count(pred):
        def body(kt, acc):
            off = koff(kt)
            ind = jnp.where(pred(keys_scr[pl.ds(off, TK), :], off), 1.0, 0.0)
            return acc + jnp.sum(ind.reshape(TK // CR, CR, Q), axis=0)
        acc = lax.fori_loop(0, n_it, body, jnp.zeros((CR, Q), F32))
        return jnp.sum(acc, axis=0, keepdims=True)

    one_p, zero_p = jnp.ones((), BF16), jnp.zeros((), BF16)

    def count_p(ref, pred):
        def body(kt, acc):
            ind = jnp.where(pred(ref[pl.ds(koff(kt), TK), :]), one_p, zero_p)
            for r in range(TK // CRP):
                acc = acc + ind[r * CRP:(r + 1) * CRP, :]
            return acc
        acc = lax.fori_loop(0, n_it, body, jnp.zeros((CRP, Q), BF16))
        return jnp.sum(acc.astype(F32), axis=0, keepdims=True)

    def rewrite_p(dst, fn):
        def body(kt, carry):
            off = koff(kt)
            dst[pl.ds(off, TK), :] = fn(off)
            return carry
        lax.fori_loop(0, n_it, body, 0)

    def coarse_value(c):
        b16 = c ^ ((c >> 31) & 0x7FFF)
        return lax.bitcast_convert_type(b16 << 16, F32).astype(BF16)

    def coarse_bit(b, c):
        cand = c + lax.shift_left(jnp.int32(1), 15 - b)
        cv = coarse_value(cand)
        return jnp.where(count_p(hi_scr, lambda x: x >= cv) >= topk, cand, c)

    def thr_bit(b, thr):
        cand = thr + lax.shift_left(jnp.int32(1), 31 - b)
        return jnp.where(count(lambda kk, off: kk >= cand) >= topk, cand, thr)

    thr = lax.fori_loop(0, 32, thr_bit, jnp.full((1, Q), INT_MIN, I32))
    thr = jnp.maximum(thr, INT_MIN + 1)
    n_ge = count(lambda kk, off: kk >= thr)

    @pl.when(jnp.max(n_ge) > topk)
    def _():
        need = topk - count(lambda kk, off: kk > thr)

        def idx_bit(b, j):
            cand = j + lax.shift_left(jnp.int32(1), idx_bits - 1 - b)
            n_before = count(lambda kk, off: (kk == thr) & (off + krow < cand))
            return jnp.where(n_before < need, cand, j)

        last = lax.fori_loop(0, idx_bits, idx_bit, jnp.zeros((1, Q), I32))

        def drop(kt, carry):
            off = koff(kt)
            kk = keys_scr[pl.ds(off, TK), :]
            keys_scr[pl.ds(off, TK), :] = jnp.where((kk == thr) & (off + krow > last), INT_MIN, kk)
            return carry

        lax.fori_loop(0, n_it, drop, 0)

    m_scr[...] = jnp.full(m_scr.shape, NEG_BIG, F32)
    acc_scr[...] = jnp.zeros_like(acc_scr)

    G = DSA_GROUP
    n_grp = (n_it + G - 1) // G

    def logits(kt, buf, thr_kt):
        off = koff(kt)
        sel = keys_scr[pl.ds(off, TK), :] >= thr_kt
        bsel = [jnp.clip(i - (kt * SUB + j), 0, 2) for j in range(SUB)]
        for h in range(DSA_HEADS):
            hs = slice(h * DSA_HD, (h + 1) * DSA_HD)
            s = _dot(kn_ref[pl.ds(off, TK), hs], qt_ref[hs, :])
            s = s + jnp.concatenate([bias_ref[h, bsel[j]] for j in range(SUB)], axis=0)
            s = jnp.where(sel, s, NEG_BIG)
            s_scr[buf, h] = s
            ml_scr[buf, h] = jnp.max(s, axis=0, keepdims=True)

    def values(kt, buf):
        off = koff(kt)
        for h in range(DSA_HEADS):
            m_old = m_scr[h]
            m_new = jnp.maximum(m_old, ml_scr[buf, h])
            p = jnp.exp2(s_scr[buf, h] - m_new).astype(BF16)
            v1 = vt_ref[h * DSA_VROWS:(h + 1) * DSA_VROWS, pl.ds(off, TK)]
            acc_scr[h] = jnp.exp2(m_old - m_new) * acc_scr[h] + _dot(v1, p)
            m_scr[h] = m_new

    def att_group(gi, carry):
        steps = []
        for e in range(G):
            kt = gi * G + e
            steps.append((jnp.minimum(kt, n_it - 1), jnp.where(kt < n_it, thr, jnp.int32(2 ** 31 - 1))))
        for e in range(G):
            logits(steps[e][0], e, steps[e][1])
        for e in range(G):
            values(steps[e][0], e)
        return carry

    lax.fori_loop(0, n_grp, att_group, 0)
    for h in range(DSA_HEADS):
        hs = slice(h * DSA_HD, (h + 1) * DSA_HD)
        acc = acc_scr[h]
        out = (acc[0:DSA_HD, :] / acc[DSA_HD:DSA_HD + 1, :]).T
        y_ref[:, hs] = (out * _silu(z_ref[:, hs].astype(F32))).astype(BF16)


def _dsa(p, qt, kn, vt, iqt, ik, iwt, bias_tiles, b, s):
    assert s % DSA_TK == 0 and s // DSA_TK * (DSA_TK // (16 * SUBLANES)) <= 256
    nt = s // TILE
    t = b * s
    topk = min(TOPK_MAX, s // 4)
    idx_bits = max(1, (s - 1).bit_length())
    kernel = functools.partial(_dsa_kernel, topk=topk, idx_bits=idx_bits)
    once = pl.Buffered(1)
    qfeat = lambda rows: pl.BlockSpec((None, rows, TILE), lambda bi, ti: (bi, 0, ti))
    return pl.pallas_call(
        kernel,
        grid=(b, nt),
        in_specs=[qfeat(DSA_W), qfeat(IDX_W), qfeat(IDX_HEADS),
                  pl.BlockSpec((TILE, DSA_W), lambda bi, ti: (bi * nt + ti, P_DSA_Z // DSA_W)),
                  pl.BlockSpec((s, DSA_W), lambda bi, ti: (bi, 0), pipeline_mode=once),
                  pl.BlockSpec((None, DSA_HEADS * DSA_VROWS, s), lambda bi, ti: (bi, 0, 0), pipeline_mode=once),
                  pl.BlockSpec((s, IDX_DIM), lambda bi, ti: (bi, 0), pipeline_mode=once),
                  pl.BlockSpec((DSA_HEADS, 3, TILE, TILE), lambda bi, ti: (0, 0, 0, 0))],
        out_specs=pl.BlockSpec((TILE, DSA_W), lambda bi, ti: (bi * nt + ti, 0)),
        out_shape=jax.ShapeDtypeStruct((t, DSA_W), BF16),
        scratch_shapes=[pltpu.VMEM((s, TILE), I32),
                        pltpu.VMEM((s, TILE), BF16), pltpu.VMEM((s, TILE), BF16), pltpu.VMEM((s, TILE), BF16),
                        pltpu.VMEM((DSA_HEADS, DSA_VROWS, TILE), F32),
                        pltpu.VMEM((DSA_HEADS, 1, TILE), F32),
                        pltpu.VMEM((IDX_HEADS // 2, IDX_DIM, 2 * TILE), BF16),
                        pltpu.VMEM((IDX_HEADS // 2, 1, 2 * TILE), F32),
                        pltpu.VMEM((DSA_GROUP, DSA_HEADS, DSA_TK, TILE), F32),
                        pltpu.VMEM((DSA_GROUP, DSA_HEADS, 1, TILE), F32)],
        compiler_params=_cparams(("arbitrary", "arbitrary"), 48),
        name="dsa",
    )(qt, iqt, iwt, p, kn, vt, ik, bias_tiles)


def _out_proj_kernel(yml_ref, ygla_ref, ydsa_ref, w_ref, x_ref, o_ref):
    y = _dot(yml_ref[...], w_ref[0:ML_W, :])
    y = y + _dot(ygla_ref[...], w_ref[ML_W:ML_W + GLA_W, :])
    y = y + _dot(ydsa_ref[...], w_ref[ML_W + GLA_W:D_MIX, :])
    o_ref[...] = x_ref[...] + y


def _out_proj(y_ml, y_gla, y_dsa, w_out, x2d):
    t = x2d.shape[0]
    tm = min(512, t)
    row = lambda width: pl.BlockSpec((tm, width), lambda i: (i, 0))
    return pl.pallas_call(
        _out_proj_kernel,
        grid=(t // tm,),
        in_specs=[row(ML_W), row(GLA_W), row(DSA_W),
                  pl.BlockSpec((D_MIX, D_MODEL), lambda i: (0, 0)), row(D_MODEL)],
        out_specs=row(D_MODEL),
        out_shape=jax.ShapeDtypeStruct((t, D_MODEL), F32),
        compiler_params=_cparams(("arbitrary",), 40),
        name="out_proj",
    )(y_ml, y_gla, y_dsa, w_out, x2d)


_COLUMN_LAYOUT = (
    ('ml_q', ML_W), ('ml_k', ML_W), ('ml_v', ML_W), ('ml_o', ML_W), ('ml_z', ML_W),
    ('ml_i', ML_HEADS), ('ml_f', ML_HEADS),
    ('gla_q', GLA_WK), ('gla_k', GLA_WK), ('gla_v', GLA_W), ('gla_a', GLA_RANK), ('gla_r', GLA_W),
    ('dsa_q', DSA_W), ('dsa_k', DSA_W), ('dsa_v', DSA_W), ('dsa_z', DSA_W),
    ('idx_q', IDX_W), ('idx_k', IDX_DIM), ('idx_w', IDX_HEADS),
)


def _split_w_in(w_in):
    cols, off = {}, 0
    for name, width in _COLUMN_LAYOUT:
        cols[name] = w_in[:, off:off + width]
        off += width
    return cols


def _pack_w_in(w_in):
    c = _split_w_in(w_in)
    d = w_in.shape[0]
    wbig = jnp.concatenate([c['ml_q'], c['ml_k'], c['ml_v'], c['ml_o'], c['ml_z'], c['idx_q'],
                            c['gla_q'], c['gla_k'], c['gla_v'], c['gla_r'],
                            c['dsa_q'], c['dsa_k'], c['dsa_v'], c['dsa_z']], axis=1)
    assert wbig.shape[1] == P_COLS
    z = lambda n: jnp.zeros((d, n), w_in.dtype)
    wsmall = jnp.concatenate([c['idx_k'], c['ml_i'], z(4), c['ml_f'], z(4), c['gla_a'], c['idx_w'],
                              z(LANES - G_IDXW - IDX_HEADS)], axis=1)
    assert wsmall.shape[1] == LANES
    return wbig.astype(BF16), wsmall.astype(BF16)


def _rel_bucket(rel):
    half = REL_BUCKETS // 2
    max_exact = half // 2
    ret = jnp.where(rel > 0, half, 0)
    n = jnp.abs(rel)
    nf = jnp.maximum(n, 1).astype(F32)
    large = max_exact + (jnp.log(nf / max_exact) / math.log(REL_MAX_DIST / max_exact)
                         * (half - max_exact)).astype(I32)
    large = jnp.minimum(large, half - 1)
    return ret + jnp.where(n < max_exact, n, large)


def _bias_tiles(rel_bias):
    assert REL_MAX_DIST <= TILE
    kl = jnp.arange(TILE, dtype=I32)[:, None]
    ql = jnp.arange(TILE, dtype=I32)[None, :]
    far = jnp.full((TILE, TILE), -(TILE + 1), I32)
    rel = jnp.stack([kl - ql, kl - TILE - ql, far])
    tab = jnp.transpose(rel_bias.astype(F32)[_rel_bucket(rel)], (3, 0, 1, 2))
    return (tab - tab[:, 2:3]) * LOG2E


def _layer(x2d, b, s, norm_g, w_in, ml_conv_w, ml_conv_b, ml_b_i, ml_b_f, ml_norm_g,
           gla_w_a, gla_b_a, gla_norm_g, dsa_q_g, dsa_k_g, w_out, bias_tiles):
    wbig, wsmall = _pack_w_in(w_in)
    p, g = _in_proj(x2d, norm_g.reshape(1, D_MODEL), wbig, wsmall)

    gbias = jnp.zeros((1, LANES), F32)
    gbias = gbias.at[0, G_MLI:G_MLI + ML_HEADS].set(ml_b_i).at[0, G_MLF:G_MLF + ML_HEADS].set(ml_b_f)
    y_ml = _mlstm(p, g, gbias, ml_conv_w, ml_conv_b.reshape(1, -1), ml_norm_g.reshape(1, -1), b, s)

    wa_pad = jnp.zeros((LANES, GLA_WK), F32).at[G_GLAA:G_GLAA + GLA_RANK].set(gla_w_a).astype(BF16)
    y_gla = _gla(p, g, wa_pad, gla_b_a.reshape(1, -1), gla_norm_g.reshape(1, -1), b, s)

    qt, kn, vt, iqt, ik, iwt = _dsa_prep(p, g, dsa_q_g.reshape(1, -1), dsa_k_g.reshape(1, -1), b, s)
    y_dsa = _dsa(p, qt, kn, vt, iqt, ik, iwt, bias_tiles, b, s)

    return _out_proj(y_ml, y_gla, y_dsa, w_out.astype(BF16), x2d)


def kernel(x, norm_g, w_in, ml_conv_w, ml_conv_b, ml_b_i, ml_b_f, ml_norm_g, gla_w_a, gla_b_a, gla_norm_g,
           dsa_q_g, dsa_k_g, w_out, rel_bias):
    b, s, d = x.shape
    assert d == D_MODEL and s % TILE == 0
    depth = norm_g.shape[0]
    bias_tiles = _bias_tiles(rel_bias)
    x2d = x.reshape(b * s, d)
    for layer in range(depth):
        x2d = _layer(x2d, b, s, norm_g[layer], w_in[layer], ml_conv_w[layer], ml_conv_b[layer],
                     ml_b_i[layer], ml_b_f[layer], ml_norm_g[layer], gla_w_a[layer], gla_b_a[layer],
                     gla_norm_g[layer], dsa_q_g[layer], dsa_k_g[layer], w_out[layer], bias_tiles)
    return x2d.reshape(b, s, d)
```

```python
import functools
import math

import jax
import jax.numpy as jnp
from jax import lax
from jax.experimental import pallas as pl
from jax.experimental.pallas import tpu as pltpu

F32 = jnp.float32
BF16 = jnp.bfloat16
I32 = jnp.int32

D_MODEL = 1024
CHUNK = 64
EPS = 1e-6
ML_HEADS, ML_HD = 4, 256
ML_W = ML_HEADS * ML_HD
ML_CONV = 4
GLA_HEADS, GLA_DK, GLA_DV = 4, 64, 128
GLA_WK = GLA_HEADS * GLA_DK
GLA_W = GLA_HEADS * GLA_DV
GLA_RANK = 16
GLA_TAU = 16.0
DSA_HEADS, DSA_HD = 4, 128
DSA_W = DSA_HEADS * DSA_HD
IDX_HEADS, IDX_DIM = 8, 64
IDX_W = IDX_HEADS * IDX_DIM
TOPK_MAX = 256
REL_BUCKETS, REL_MAX_DIST = 32, 128
D_MIX = ML_W + GLA_W + DSA_W

LANES = 128
SUBLANES = 8

TILE = 128
DSA_TK = 512
DSA_SUB = DSA_TK // TILE
DSA_GROUP = 2
DSA_NARROW_BITS = 25
DSA_VROWS = DSA_HD + 16
LOG2E = 1.4426950408889634

P_ML = 0
P_IDXQ = 5 * ML_W
P_GLA_Q = P_IDXQ + IDX_W
P_GLA_K = P_GLA_Q + GLA_WK
P_GLA_V = P_GLA_K + GLA_WK
P_GLA_R = P_GLA_V + GLA_W
P_DSA_Q = P_GLA_R + GLA_W
P_DSA_K = P_DSA_Q + DSA_W
P_DSA_V = P_DSA_K + DSA_W
P_DSA_Z = P_DSA_V + DSA_W
P_COLS = P_DSA_Z + DSA_W
G_IDXK = 0
G_MLI = 64
G_MLF = 72
G_GLAA = 80
G_IDXW = 96

INT_MIN = -(2 ** 31)
NEG_BIG = -1e30


def _cparams(sem, vmem_mb):
    return pltpu.CompilerParams(dimension_semantics=sem, vmem_limit_bytes=vmem_mb << 20)


def _dot(a, b):
    return jnp.dot(a, b, preferred_element_type=F32)


def _dot_nt(a, b):
    return lax.dot_general(a, b, (((1,), (1,)), ((), ())), preferred_element_type=F32)


def _split3(x):
    hi = x.astype(BF16)
    r1 = x - hi.astype(F32)
    mid = r1.astype(BF16)
    lo = (r1 - mid.astype(F32)).astype(BF16)
    return hi, mid, lo


def _dot_f32_lhs(x, m01):
    hi, mid, lo = _split3(x)
    return _dot(hi, m01) + _dot(mid, m01) + _dot(lo, m01)


def _dot_f32_rhs(m01, x):
    hi, mid, lo = _split3(x)
    return _dot(m01, hi) + _dot(m01, mid) + _dot(m01, lo)


def _log_sigmoid(x):
    return jnp.minimum(x, 0.0) - jnp.log(1.0 + jnp.exp(-jnp.abs(x)))


def _sigmoid(x):
    return 1.0 / (1.0 + jnp.exp(-x))


def _silu(x):
    return x * _sigmoid(x)


def _in_proj_kernel(x_ref, g_ref, wb_ref, ws_ref, p_ref, gate_ref, xn_ref):
    @pl.when(pl.program_id(1) == 0)
    def _():
        x = x_ref[...]
        ms = jnp.mean(x * x, axis=-1, keepdims=True)
        xn = (x * lax.rsqrt(ms + EPS) * g_ref[...]).astype(BF16)
        xn_ref[...] = xn
        gate_ref[...] = _dot(xn, ws_ref[...])

    p_ref[...] = _dot(xn_ref[...], wb_ref[...]).astype(BF16)


def _in_proj(x2d, g, wbig, wsmall):
    t = x2d.shape[0]
    tm = min(1024, t)
    tn = P_COLS // 4
    return pl.pallas_call(
        _in_proj_kernel,
        grid=(t // tm, P_COLS // tn),
        in_specs=[
            pl.BlockSpec((tm, D_MODEL), lambda i, j: (i, 0)),
            pl.BlockSpec((1, D_MODEL), lambda i, j: (0, 0)),
            pl.BlockSpec((D_MODEL, tn), lambda i, j: (0, j)),
            pl.BlockSpec((D_MODEL, LANES), lambda i, j: (0, 0)),
        ],
        out_specs=[
            pl.BlockSpec((tm, tn), lambda i, j: (i, j)),
            pl.BlockSpec((tm, LANES), lambda i, j: (i, 0)),
        ],
        out_shape=[jax.ShapeDtypeStruct((t, P_COLS), BF16), jax.ShapeDtypeStruct((t, LANES), F32)],
        scratch_shapes=[pltpu.VMEM((tm, D_MODEL), BF16)],
        compiler_params=_cparams(("arbitrary", "arbitrary"), 56),
        name="in_proj",
    )(x2d, g, wbig, wsmall)


def _mlstm_kernel(q_ref, k_ref, v_ref, o_ref, z_ref, g_ref, gb_ref, cw_ref, cb_ref, ng_ref,
                  y_ref, c_scr, n_scr, m_scr, xbuf):
    L = TILE

    @pl.when(pl.program_id(1) == 0)
    def _():
        c_scr[...] = jnp.zeros_like(c_scr)
        n_scr[...] = jnp.zeros_like(n_scr)
        m_scr[...] = jnp.zeros_like(m_scr)
        xbuf[0:SUBLANES, :] = jnp.zeros((SUBLANES, 2 * ML_W), F32)

    xbuf[SUBLANES:SUBLANES + L, 0:ML_W] = q_ref[...].astype(F32)
    xbuf[SUBLANES:SUBLANES + L, ML_W:2 * ML_W] = k_ref[...].astype(F32)
    conv = jnp.broadcast_to(cb_ref[...], (L, 2 * ML_W))
    for j in range(ML_CONV):
        off = SUBLANES - (ML_CONV - 1) + j
        conv = conv + cw_ref[j:j + 1, :] * xbuf[off:off + L, :]
    xbuf[0:SUBLANES, :] = xbuf[L:L + SUBLANES, :]
    qk = _silu(conv)

    gate = g_ref[...] + gb_ref[...]
    lf_cols = _log_sigmoid(gate)
    gate_t = gate.T
    li_rows = gate_t[G_MLI:G_MLI + SUBLANES, :]
    lf_rows = _log_sigmoid(gate_t[G_MLF:G_MLF + SUBLANES, :])
    r_io = lax.broadcasted_iota(I32, (L, L), 0)
    c_io = lax.broadcasted_iota(I32, (L, L), 1)
    causal = c_io <= r_io
    tri_u = (r_io <= c_io).astype(BF16)
    tri_l = causal.astype(BF16)
    bc_rows = _dot_f32_lhs(lf_rows, tri_u)
    bc_cols = _dot_f32_rhs(tri_l, lf_cols)

    for h in range(ML_HEADS):
        hs = slice(h * ML_HD, (h + 1) * ML_HD)
        qh = qk[:, hs]
        kh = qk[:, ML_W + h * ML_HD:ML_W + (h + 1) * ML_HD] * (ML_HD ** -0.5)
        vh = v_ref[:, hs]
        qb = qh.astype(BF16)
        kb = kh.astype(BF16)
        bc_col = bc_cols[:, G_MLF + h:G_MLF + h + 1]
        bc_row = bc_rows[h:h + 1, :]
        li_row = li_rows[h:h + 1, :]
        m_prev = m_scr[h, :, 0:1]

        dmat = jnp.where(causal, bc_col - (bc_row - li_row), -jnp.inf)
        inter = bc_col + m_prev
        m_row = jnp.maximum(inter, jnp.max(dmat, axis=1, keepdims=True))
        w_intra = jnp.exp(dmat - m_row)
        w_inter = jnp.exp(inter - m_row)
        s_qk = _dot_nt(qb, kb) * w_intra
        num = _dot(s_qk.astype(BF16), vh) + w_inter * _dot(qb, c_scr[h].astype(BF16))
        qn = jnp.sum(qh * n_scr[h], axis=1, keepdims=True)
        den = jnp.sum(s_qk, axis=1, keepdims=True) + w_inter * qn
        hh = num / jnp.maximum(jnp.abs(den), jnp.exp(-m_row))

        g_tot = bc_row[:, L - 1:L]
        a_row = g_tot - bc_row + li_row
        m_new = jnp.maximum(g_tot + m_prev, jnp.max(a_row, axis=1, keepdims=True))
        decay = jnp.exp(g_tot + m_prev - m_new)
        wa_row = jnp.exp(a_row - m_new)
        ktw = (kh.T * wa_row).astype(BF16)
        c_scr[h] = decay * c_scr[h] + _dot(ktw, vh)
        wa8 = jnp.broadcast_to(wa_row, (SUBLANES, L)).astype(BF16)
        n_scr[h] = decay * n_scr[h] + _dot(wa8, kb)[0:1, :]
        m_scr[h] = jnp.broadcast_to(m_new, (1, LANES))

        hm = hh * _sigmoid(o_ref[:, hs].astype(F32))
        hm = hm * lax.rsqrt(jnp.mean(hm * hm, axis=1, keepdims=True) + EPS) * ng_ref[:, hs]
        y_ref[:, hs] = (hm * _silu(z_ref[:, hs].astype(F32))).astype(BF16)


def _mlstm(p, g, gbias, conv_w, conv_b, norm_g, b, s):
    nt = s // TILE
    t = b * s

    def pcol(c):
        return pl.BlockSpec((TILE, ML_W), lambda bi, ti, c=c: (bi * nt + ti, c))

    full = lambda shape: pl.BlockSpec(shape, lambda bi, ti: (0,) * len(shape))
    return pl.pallas_call(
        _mlstm_kernel,
        grid=(b, nt),
        in_specs=[pcol(0), pcol(1), pcol(2), pcol(3), pcol(4),
                  pl.BlockSpec((TILE, LANES), lambda bi, ti: (bi * nt + ti, 0)),
                  full((1, LANES)), full((ML_CONV, 2 * ML_W)), full((1, 2 * ML_W)), full((1, ML_W))],
        out_specs=pl.BlockSpec((TILE, ML_W), lambda bi, ti: (bi * nt + ti, 0)),
        out_shape=jax.ShapeDtypeStruct((t, ML_W), BF16),
        scratch_shapes=[pltpu.VMEM((ML_HEADS, ML_HD, ML_HD), F32),
                        pltpu.VMEM((ML_HEADS, 1, ML_HD), F32),
                        pltpu.VMEM((ML_HEADS, 1, LANES), F32),
                        pltpu.VMEM((TILE + SUBLANES, 2 * ML_W), F32)],
        compiler_params=_cparams(("arbitrary", "arbitrary"), 40),
        name="mlstm",
    )(p, p, p, p, p, g, gbias, conv_w, conv_b, norm_g)


def _gla_kernel(q_ref, k_ref, v_ref, r_ref, g_ref, wa_ref, ba_ref, ng_ref, y_ref, st_scr):
    L = TILE
    H = GLA_HEADS

    @pl.when(pl.program_id(1) == 0)
    def _():
        st_scr[...] = jnp.zeros_like(st_scr)

    la = _log_sigmoid(_dot(g_ref[...].astype(BF16), wa_ref[...]) + ba_ref[...]) * (1.0 / GLA_TAU)
    r_io = lax.broadcasted_iota(I32, (L, L), 0)
    c_io = lax.broadcasted_iota(I32, (L, L), 1)
    tri_l = (c_io <= r_io).astype(BF16)
    bcum = _dot_f32_rhs(tri_l, la)
    btot = bcum[L - 1:L, :]
    bmid = bcum[L // 2 - 1:L // 2, :]

    q = q_ref[...].astype(F32) * (GLA_DK ** -0.5)
    k = k_ref[...].astype(F32)
    v = v_ref[...]
    q_in = (q * jnp.exp(bcum - bmid)).astype(BF16)
    k_in = k * jnp.exp(bmid - bcum)
    q_st = (q * jnp.exp(bcum)).astype(BF16)
    k_st = (k * jnp.exp(btot - bcum)).astype(BF16)

    kt = k_in.T
    kt4 = jnp.concatenate([kt] * H, axis=1)
    rr = lax.broadcasted_iota(I32, (GLA_WK, H * L), 0)
    cc = lax.broadcasted_iota(I32, (GLA_WK, H * L), 1)
    k_bd = jnp.where(rr // GLA_DK == cc // L, kt4, 0.0).astype(BF16)
    att = _dot(q_in, k_bd)
    ar = lax.broadcasted_iota(I32, (L, H * L), 0)
    ac = lax.broadcasted_iota(I32, (L, H * L), 1)
    att = jnp.where(ac % L <= ar, att, 0.0).astype(BF16)
    v4 = jnp.concatenate([v] * H, axis=0)
    vr = lax.broadcasted_iota(I32, (H * L, GLA_W), 0)
    vc = lax.broadcasted_iota(I32, (H * L, GLA_W), 1)
    v_bd = jnp.where(vr // L == vc // GLA_DV, v4, jnp.zeros_like(v4))
    o = _dot(att, v_bd) + _dot_nt(q_st, st_scr[...].astype(BF16))

    sr = lax.broadcasted_iota(I32, (GLA_W, GLA_WK), 0)
    sc = lax.broadcasted_iota(I32, (GLA_W, GLA_WK), 1)
    upd = _dot(v.astype(F32).T.astype(BF16), k_st)
    st_scr[...] = st_scr[...] * jnp.exp(btot) + jnp.where(sr // GLA_DV == sc // GLA_DK, upd, 0.0)

    for h in range(H):
        hs = slice(h * GLA_DV, (h + 1) * GLA_DV)
        oh = o[:, hs]
        oh = oh * lax.rsqrt(jnp.mean(oh * oh, axis=1, keepdims=True) + EPS) * ng_ref[:, hs]
        y_ref[:, hs] = (oh * _silu(r_ref[:, hs].astype(F32))).astype(BF16)


def _gla(p, g, wa_pad, b_a, norm_g, b, s):
    nt = s // TILE
    t = b * s

    def pcol(width, off):
        assert off % width == 0
        return pl.BlockSpec((TILE, width), lambda bi, ti: (bi * nt + ti, off // width))

    full = lambda shape: pl.BlockSpec(shape, lambda bi, ti: (0,) * len(shape))
    return pl.pallas_call(
        _gla_kernel,
        grid=(b, nt),
        in_specs=[pcol(GLA_WK, P_GLA_Q), pcol(GLA_WK, P_GLA_K), pcol(GLA_W, P_GLA_V), pcol(GLA_W, P_GLA_R),
                  pl.BlockSpec((TILE, LANES), lambda bi, ti: (bi * nt + ti, 0)),
                  full((LANES, GLA_WK)), full((1, GLA_WK)), full((1, GLA_W))],
        out_specs=pl.BlockSpec((TILE, GLA_W), lambda bi, ti: (bi * nt + ti, 0)),
        out_shape=jax.ShapeDtypeStruct((t, GLA_W), BF16),
        scratch_shapes=[pltpu.VMEM((GLA_W, GLA_WK), F32)],
        compiler_params=_cparams(("arbitrary", "arbitrary"), 32),
        name="gla",
    )(p, p, p, p, g, wa_pad, b_a, norm_g)


def _dsa_prep_kernel(q_ref, k_ref, v_ref, iq_ref, g_ref, qg_ref, kg_ref,
                     qt_ref, kn_ref, vt_ref, iqt_ref, ik_ref, iwt_ref):
    qn = []
    for h in range(DSA_HEADS):
        hs = slice(h * DSA_HD, (h + 1) * DSA_HD)
        q = q_ref[:, hs].astype(F32)
        k = k_ref[:, hs].astype(F32)
        qh = q * lax.rsqrt(jnp.mean(q * q, axis=1, keepdims=True) + EPS) * qg_ref[...]
        kh = k * lax.rsqrt(jnp.mean(k * k, axis=1, keepdims=True) + EPS) * kg_ref[...]
        qn.append(qh * (DSA_HD ** -0.5 * LOG2E))
        kn_ref[:, hs] = kh.astype(BF16)
        r0 = h * DSA_VROWS
        vt_ref[r0:r0 + DSA_HD, :] = v_ref[:, hs].astype(F32).T.astype(BF16)
        vt_ref[r0 + DSA_HD:r0 + DSA_VROWS, :] = jnp.ones((DSA_VROWS - DSA_HD, vt_ref.shape[1]), BF16)
    qt_ref[...] = jnp.concatenate(qn, axis=1).T.astype(BF16)
    iqt_ref[...] = iq_ref[...].astype(F32).T.astype(BF16)
    g = g_ref[...]
    ik_ref[...] = g[:, G_IDXK:G_IDXK + IDX_DIM].astype(BF16)
    iw_scale = (IDX_HEADS ** -0.5) * (IDX_DIM ** -0.5)
    iwt_ref[...] = g.T[G_IDXW:G_IDXW + IDX_HEADS, :] * iw_scale


def _dsa_prep(p, g, q_g, k_g, b, s):
    t = b * s
    tm = min(512, s)
    nt = s // tm
    full = lambda shape: pl.BlockSpec(shape, lambda bi, ti: (0,) * len(shape))
    tok = lambda width, cb: pl.BlockSpec((tm, width), lambda bi, ti: (bi * nt + ti, cb))
    feat = lambda rows: pl.BlockSpec((None, rows, tm), lambda bi, ti: (bi, 0, ti))
    return pl.pallas_call(
        _dsa_prep_kernel,
        grid=(b, nt),
        in_specs=[tok(DSA_W, P_DSA_Q // DSA_W), tok(DSA_W, P_DSA_K // DSA_W), tok(DSA_W, P_DSA_V // DSA_W),
                  tok(IDX_W, P_IDXQ // IDX_W), tok(LANES, 0), full((1, DSA_HD)), full((1, DSA_HD))],
        out_specs=[feat(DSA_W), tok(DSA_W, 0), feat(DSA_HEADS * DSA_VROWS), feat(IDX_W), tok(IDX_DIM, 0),
                   feat(IDX_HEADS)],
        out_shape=[jax.ShapeDtypeStruct((b, DSA_W, s), BF16),
                   jax.ShapeDtypeStruct((t, DSA_W), BF16),
                   jax.ShapeDtypeStruct((b, DSA_HEADS * DSA_VROWS, s), BF16),
                   jax.ShapeDtypeStruct((b, IDX_W, s), BF16),
                   jax.ShapeDtypeStruct((t, IDX_DIM), BF16),
                   jax.ShapeDtypeStruct((b, IDX_HEADS, s), F32)],
        compiler_params=_cparams(("arbitrary", "arbitrary"), 40),
        name="dsa_prep",
    )(p, p, p, p, g, q_g, k_g)


def _dsa_kernel(qt_ref, iqt_ref, iwt_ref, z_ref, kn_ref, vt_ref, ik_ref, bias_ref, y_ref,
                keys_scr, acc_scr, m_scr, iq2_scr, iw2_scr, s_scr, ml_scr, *, topk, idx_bits):
    Q, TK, SUB = TILE, DSA_TK, DSA_SUB
    i = pl.program_id(1)
    n_it = (i + SUB) // SUB
    qpos = i * Q + lax.broadcasted_iota(I32, (1, Q), 1)
    limit = (qpos // CHUNK + 1) * CHUNK
    krow = lax.broadcasted_iota(I32, (TK, Q), 0)

    def koff(kt):
        return pl.multiple_of(kt * TK, TK)

    for j in range(IDX_HEADS // 2):
        for e in range(2):
            h = 2 * j + e
            iq2_scr[j, :, e * Q:(e + 1) * Q] = iqt_ref[h * IDX_DIM:(h + 1) * IDX_DIM, :]
            iw2_scr[j, :, e * Q:(e + 1) * Q] = iwt_ref[h:h + 1, :]

    CR = 8 * SUBLANES

    def to_key(x):
        bits = lax.bitcast_convert_type(x, I32)
        return bits ^ ((bits >> 31) & 0x7FFFFFFF)

    def score_tile(kt, last, smax):
        off = koff(kt)
        ikt = ik_ref[pl.ds(off, TK), :]
        sc = jnp.zeros((TK, Q), F32)
        for j in range(IDX_HEADS // 2):
            w = jnp.maximum(_dot(ikt, iq2_scr[j]), 0.0) * iw2_scr[j]
            sc = sc + (w[:, 0:Q] + w[:, Q:2 * Q])
        keys = to_key(sc)
        if last:
            inadm = off + krow >= limit
            keys = jnp.where(inadm, INT_MIN, keys)
            sc = jnp.where(inadm, -jnp.inf, sc)
        keys_scr[pl.ds(off, TK), :] = keys
        return jnp.maximum(smax, jnp.max(sc.reshape(TK // CR, CR, Q), axis=0))

    smax = lax.fori_loop(0, n_it - 1, lambda kt, m: score_tile(kt, False, m), jnp.full((CR, Q), -jnp.inf, F32))
    smax = score_tile(n_it - 1, True, smax)
    kmax = to_key(jnp.max(smax, axis=0, keepdims=True))

    def count(pred):
        def body(kt, acc):
            off = koff(kt)
            ind = jnp.where(pred(keys_scr[pl.ds(off, TK), :], off), 1.0, 0.0)
            return acc + jnp.sum(ind.reshape(TK // CR, CR, Q), axis=0)
        acc = lax.fori_loop(0, n_it, body, jnp.zeros((CR, Q), F32))
        return jnp.sum(acc, axis=0, keepdims=True)

    take_all = limit <= topk
    span = (1 << DSA_NARROW_BITS) - 1
    in_window = count(lambda kk, off: kk >= kmax - span) >= topk
    narrow_ok = jnp.min(jnp.where(take_all | in_window, 1.0, 0.0)) > 0.0

    def narrow_search():
        def bit(b, d):
            cand = d - lax.shift_left(jnp.int32(1), DSA_NARROW_BITS - 1 - b)
            return jnp.where(count(lambda kk, off: kk >= kmax - cand) >= topk, cand, d)
        return kmax - lax.fori_loop(0, DSA_NARROW_BITS, bit, jnp.full((1, Q), span, I32))

    def full_search():
        def bit(b, t):
            cand = t + lax.shift_left(jnp.int32(1), 31 - b)
            return jnp.where(count(lambda kk, off: kk >= cand) >= topk, cand, t)
        return lax.fori_loop(0, 32, bit, jnp.full((1, Q), INT_MIN, I32))

    thr = lax.cond(narrow_ok, narrow_search, full_search)
    thr = jnp.where(take_all, INT_MIN + 1, thr)
    n_ge = count(lambda kk, off: kk >= thr)

    @pl.when(jnp.max(n_ge) > topk)
    def _():
        need = topk - count(lambda kk, off: kk > thr)

        def idx_bit(b, j):
            cand = j + lax.shift_left(jnp.int32(1), idx_bits - 1 - b)
            n_before = count(lambda kk, off: (kk == thr) & (off + krow < cand))
            return jnp.where(n_before < need, cand, j)

        last = lax.fori_loop(0, idx_bits, idx_bit, jnp.zeros((1, Q), I32))

        def drop(kt, carry):
            off = koff(kt)
            kk = keys_scr[pl.ds(off, TK), :]
            keys_scr[pl.ds(off, TK), :] = jnp.where((kk == thr) & (off + krow > last), INT_MIN, kk)
            return carry

        lax.fori_loop(0, n_it, drop, 0)

    m_scr[...] = jnp.full(m_scr.shape, NEG_BIG, F32)
    acc_scr[...] = jnp.zeros_like(acc_scr)

    G = DSA_GROUP
    n_grp = (n_it + G - 1) // G

    def logits(kt, buf, thr_kt):
        off = koff(kt)
        sel = keys_scr[pl.ds(off, TK), :] >= thr_kt
        bsel = [jnp.clip(i - (kt * SUB + j), 0, 2) for j in range(SUB)]
        for h in range(DSA_HEADS):
            hs = slice(h * DSA_HD, (h + 1) * DSA_HD)
            s = _dot(kn_ref[pl.ds(off, TK), hs], qt_ref[hs, :])
            s = s + jnp.concatenate([bias_ref[h, bsel[j]] for j in range(SUB)], axis=0)
            s = jnp.where(sel, s, NEG_BIG)
            s_scr[buf, h] = s
            ml_scr[buf, h] = jnp.max(s, axis=0, keepdims=True)

    def values(kt, buf):
        off = koff(kt)
        for h in range(DSA_HEADS):
            m_old = m_scr[h]
            m_new = jnp.maximum(m_old, ml_scr[buf, h])
            p = jnp.exp2(s_scr[buf, h] - m_new).astype(BF16)
            v1 = vt_ref[h * DSA_VROWS:(h + 1) * DSA_VROWS, pl.ds(off, TK)]
            acc_scr[h] = jnp.exp2(m_old - m_new) * acc_scr[h] + _dot(v1, p)
            m_scr[h] = m_new

    def att_group(gi, carry):
        steps = []
        for e in range(G):
            kt = gi * G + e
            steps.append((jnp.minimum(kt, n_it - 1), jnp.where(kt < n_it, thr, jnp.int32(2 ** 31 - 1))))
        for e in range(G):
            logits(steps[e][0], e, steps[e][1])
        for e in range(G):
            values(steps[e][0], e)
        return carry

    lax.fori_loop(0, n_grp, att_group, 0)
    for h in range(DSA_HEADS):
        hs = slice(h * DSA_HD, (h + 1) * DSA_HD)
        acc = acc_scr[h]
        out = (acc[0:DSA_HD, :] / acc[DSA_HD:DSA_HD + 1, :]).T
        y_ref[:, hs] = (out * _silu(z_ref[:, hs].astype(F32))).astype(BF16)


def _dsa(p, qt, kn, vt, iqt, ik, iwt, bias_tiles, b, s):
    assert s % DSA_TK == 0
    nt = s // TILE
    t = b * s
    topk = min(TOPK_MAX, s // 4)
    idx_bits = max(1, (s - 1).bit_length())
    kernel = functools.partial(_dsa_kernel, topk=topk, idx_bits=idx_bits)
    once = pl.Buffered(1)
    qfeat = lambda rows: pl.BlockSpec((None, rows, TILE), lambda bi, ti: (bi, 0, ti))
    return pl.pallas_call(
        kernel,
        grid=(b, nt),
        in_specs=[qfeat(DSA_W), qfeat(IDX_W), qfeat(IDX_HEADS),
                  pl.BlockSpec((TILE, DSA_W), lambda bi, ti: (bi * nt + ti, P_DSA_Z // DSA_W)),
                  pl.BlockSpec((s, DSA_W), lambda bi, ti: (bi, 0), pipeline_mode=once),
                  pl.BlockSpec((None, DSA_HEADS * DSA_VROWS, s), lambda bi, ti: (bi, 0, 0), pipeline_mode=once),
                  pl.BlockSpec((s, IDX_DIM), lambda bi, ti: (bi, 0), pipeline_mode=once),
                  pl.BlockSpec((DSA_HEADS, 3, TILE, TILE), lambda bi, ti: (0, 0, 0, 0))],
        out_specs=pl.BlockSpec((TILE, DSA_W), lambda bi, ti: (bi * nt + ti, 0)),
        out_shape=jax.ShapeDtypeStruct((t, DSA_W), BF16),
        scratch_shapes=[pltpu.VMEM((s, TILE), I32),
                        pltpu.VMEM((DSA_HEADS, DSA_VROWS, TILE), F32),
                        pltpu.VMEM((DSA_HEADS, 1, TILE), F32),
                        pltpu.VMEM((IDX_HEADS // 2, IDX_DIM, 2 * TILE), BF16),
                        pltpu.VMEM((IDX_HEADS // 2, 1, 2 * TILE), F32),
                        pltpu.VMEM((DSA_GROUP, DSA_HEADS, DSA_TK, TILE), F32),
                        pltpu.VMEM((DSA_GROUP, DSA_HEADS, 1, TILE), F32)],
        compiler_params=_cparams(("arbitrary", "arbitrary"), 48),
        name="dsa",
    )(qt, iqt, iwt, p, kn, vt, ik, bias_tiles)


def _out_proj_kernel(yml_ref, ygla_ref, ydsa_ref, w_ref, x_ref, o_ref):
    y = _dot(yml_ref[...], w_ref[0:ML_W, :])
    y = y + _dot(ygla_ref[...], w_ref[ML_W:ML_W + GLA_W, :])
    y = y + _dot(ydsa_ref[...], w_ref[ML_W + GLA_W:D_MIX, :])
    o_ref[...] = x_ref[...] + y


def _out_proj(y_ml, y_gla, y_dsa, w_out, x2d):
    t = x2d.shape[0]
    tm = min(512, t)
    row = lambda width: pl.BlockSpec((tm, width), lambda i: (i, 0))
    return pl.pallas_call(
        _out_proj_kernel,
        grid=(t // tm,),
        in_specs=[row(ML_W), row(GLA_W), row(DSA_W),
                  pl.BlockSpec((D_MIX, D_MODEL), lambda i: (0, 0)), row(D_MODEL)],
        out_specs=row(D_MODEL),
        out_shape=jax.ShapeDtypeStruct((t, D_MODEL), F32),
        compiler_params=_cparams(("arbitrary",), 40),
        name="out_proj",
    )(y_ml, y_gla, y_dsa, w_out, x2d)


_COLUMN_LAYOUT = (
    ('ml_q', ML_W), ('ml_k', ML_W), ('ml_v', ML_W), ('ml_o', ML_W), ('ml_z', ML_W),
    ('ml_i', ML_HEADS), ('ml_f', ML_HEADS),
    ('gla_q', GLA_WK), ('gla_k', GLA_WK), ('gla_v', GLA_W), ('gla_a', GLA_RANK), ('gla_r', GLA_W),
    ('dsa_q', DSA_W), ('dsa_k', DSA_W), ('dsa_v', DSA_W), ('dsa_z', DSA_W),
    ('idx_q', IDX_W), ('idx_k', IDX_DIM), ('idx_w', IDX_HEADS),
)


def _split_w_in(w_in):
    cols, off = {}, 0
    for name, width in _COLUMN_LAYOUT:
        cols[name] = w_in[:, off:off + width]
        off += width
    return cols


def _pack_w_in(w_in):
    c = _split_w_in(w_in)
    d = w_in.shape[0]
    wbig = jnp.concatenate([c['ml_q'], c['ml_k'], c['ml_v'], c['ml_o'], c['ml_z'], c['idx_q'],
                            c['gla_q'], c['gla_k'], c['gla_v'], c['gla_r'],
                            c['dsa_q'], c['dsa_k'], c['dsa_v'], c['dsa_z']], axis=1)
    assert wbig.shape[1] == P_COLS
    z = lambda n: jnp.zeros((d, n), w_in.dtype)
    wsmall = jnp.concatenate([c['idx_k'], c['ml_i'], z(4), c['ml_f'], z(4), c['gla_a'], c['idx_w'],
                              z(LANES - G_IDXW - IDX_HEADS)], axis=1)
    assert wsmall.shape[1] == LANES
    return wbig.astype(BF16), wsmall.astype(BF16)


def _rel_bucket(rel):
    half = REL_BUCKETS // 2
    max_exact = half // 2
    ret = jnp.where(rel > 0, half, 0)
    n = jnp.abs(rel)
    nf = jnp.maximum(n, 1).astype(F32)
    large = max_exact + (jnp.log(nf / max_exact) / math.log(REL_MAX_DIST / max_exact)
                         * (half - max_exact)).astype(I32)
    large = jnp.minimum(large, half - 1)
    return ret + jnp.where(n < max_exact, n, large)


def _bias_tiles(rel_bias):
    assert REL_MAX_DIST <= TILE
    kl = jnp.arange(TILE, dtype=I32)[:, None]
    ql = jnp.arange(TILE, dtype=I32)[None, :]
    far = jnp.full((TILE, TILE), -(TILE + 1), I32)
    rel = jnp.stack([kl - ql, kl - TILE - ql, far])
    tab = jnp.transpose(rel_bias.astype(F32)[_rel_bucket(rel)], (3, 0, 1, 2))
    return (tab - tab[:, 2:3]) * LOG2E


def _layer(x2d, b, s, norm_g, w_in, ml_conv_w, ml_conv_b, ml_b_i, ml_b_f, ml_norm_g,
           gla_w_a, gla_b_a, gla_norm_g, dsa_q_g, dsa_k_g, w_out, bias_tiles):
    wbig, wsmall = _pack_w_in(w_in)
    p, g = _in_proj(x2d, norm_g.reshape(1, D_MODEL), wbig, wsmall)

    gbias = jnp.zeros((1, LANES), F32)
    gbias = gbias.at[0, G_MLI:G_MLI + ML_HEADS].set(ml_b_i).at[0, G_MLF:G_MLF + ML_HEADS].set(ml_b_f)
    y_ml = _mlstm(p, g, gbias, ml_conv_w, ml_conv_b.reshape(1, -1), ml_norm_g.reshape(1, -1), b, s)

    wa_pad = jnp.zeros((LANES, GLA_WK), F32).at[G_GLAA:G_GLAA + GLA_RANK].set(gla_w_a).astype(BF16)
    y_gla = _gla(p, g, wa_pad, gla_b_a.reshape(1, -1), gla_norm_g.reshape(1, -1), b, s)

    qt, kn, vt, iqt, ik, iwt = _dsa_prep(p, g, dsa_q_g.reshape(1, -1), dsa_k_g.reshape(1, -1), b, s)
    y_dsa = _dsa(p, qt, kn, vt, iqt, ik, iwt, bias_tiles, b, s)

    return _out_proj(y_ml, y_gla, y_dsa, w_out.astype(BF16), x2d)


def kernel(x, norm_g, w_in, ml_conv_w, ml_conv_b, ml_b_i, ml_b_f, ml_norm_g, gla_w_a, gla_b_a, gla_norm_g,
           dsa_q_g, dsa_k_g, w_out, rel_bias):
    b, s, d = x.shape
    assert d == D_MODEL and s % TILE == 0
    depth = norm_g.shape[0]
    bias_tiles = _bias_tiles(rel_bias)
    x2d = x.reshape(b * s, d)
    for layer in range(depth):
        x2d = _layer(x2d, b, s, norm_g[layer], w_in[layer], ml_conv_w[layer], ml_conv_b[layer],
                     ml_b_i[layer], ml_b_f[layer], ml_norm_g[layer], gla_w_a[layer], gla_b_a[layer],
                     gla_norm_g[layer], dsa_q_g[layer], dsa_k_g[layer], w_out[layer], bias_tiles)
    return x2d.reshape(b, s, d)
```

```python
import functools
import math

import jax
import jax.numpy as jnp
from jax import lax
from jax.experimental import pallas as pl
from jax.experimental.pallas import tpu as pltpu

F32 = jnp.float32
BF16 = jnp.bfloat16
I32 = jnp.int32

D_MODEL = 1024
CHUNK = 64
EPS = 1e-6
ML_HEADS, ML_HD = 4, 256
ML_W = ML_HEADS * ML_HD
ML_CONV = 4
GLA_HEADS, GLA_DK, GLA_DV = 4, 64, 128
GLA_WK = GLA_HEADS * GLA_DK
GLA_W = GLA_HEADS * GLA_DV
GLA_RANK = 16
GLA_TAU = 16.0
DSA_HEADS, DSA_HD = 4, 128
DSA_W = DSA_HEADS * DSA_HD
IDX_HEADS, IDX_DIM = 8, 64
IDX_W = IDX_HEADS * IDX_DIM
TOPK_MAX = 256
REL_BUCKETS, REL_MAX_DIST = 32, 128
D_MIX = ML_W + GLA_W + DSA_W

LANES = 128
SUBLANES = 8

TILE = 128
DSA_TK = 512
DSA_SUB = DSA_TK // TILE
DSA_GROUP = 2
DSA_SEARCH_STOPS = (0, 25, 29)
DSA_VROWS = DSA_HD + 16
LOG2E = 1.4426950408889634

P_ML = 0
P_IDXQ = 5 * ML_W
P_GLA_Q = P_IDXQ + IDX_W
P_GLA_K = P_GLA_Q + GLA_WK
P_GLA_V = P_GLA_K + GLA_WK
P_GLA_R = P_GLA_V + GLA_W
P_DSA_Q = P_GLA_R + GLA_W
P_DSA_K = P_DSA_Q + DSA_W
P_DSA_V = P_DSA_K + DSA_W
P_DSA_Z = P_DSA_V + DSA_W
P_COLS = P_DSA_Z + DSA_W
G_IDXK = 0
G_MLI = 64
G_MLF = 72
G_GLAA = 80
G_IDXW = 96

INT_MIN = -(2 ** 31)
NEG_BIG = -1e30


def _cparams(sem, vmem_mb):
    return pltpu.CompilerParams(dimension_semantics=sem, vmem_limit_bytes=vmem_mb << 20)


def _dot(a, b):
    return jnp.dot(a, b, preferred_element_type=F32)


def _dot_nt(a, b):
    return lax.dot_general(a, b, (((1,), (1,)), ((), ())), preferred_element_type=F32)


def _split3(x):
    hi = x.astype(BF16)
    r1 = x - hi.astype(F32)
    mid = r1.astype(BF16)
    lo = (r1 - mid.astype(F32)).astype(BF16)
    return hi, mid, lo


def _dot_f32_lhs(x, m01):
    hi, mid, lo = _split3(x)
    return _dot(hi, m01) + _dot(mid, m01) + _dot(lo, m01)


def _dot_f32_rhs(m01, x):
    hi, mid, lo = _split3(x)
    return _dot(m01, hi) + _dot(m01, mid) + _dot(m01, lo)


def _log_sigmoid(x):
    return jnp.minimum(x, 0.0) - jnp.log(1.0 + jnp.exp(-jnp.abs(x)))


def _sigmoid(x):
    return 1.0 / (1.0 + jnp.exp(-x))


def _silu(x):
    return x * _sigmoid(x)


def _in_proj_kernel(x_ref, g_ref, wb_ref, ws_ref, p_ref, gate_ref, xn_ref):
    @pl.when(pl.program_id(1) == 0)
    def _():
        x = x_ref[...]
        ms = jnp.mean(x * x, axis=-1, keepdims=True)
        xn = (x * lax.rsqrt(ms + EPS) * g_ref[...]).astype(BF16)
        xn_ref[...] = xn
        gate_ref[...] = _dot(xn, ws_ref[...])

    p_ref[...] = _dot(xn_ref[...], wb_ref[...]).astype(BF16)


def _in_proj(x2d, g, wbig, wsmall):
    t = x2d.shape[0]
    tm = min(1024, t)
    tn = P_COLS // 4
    return pl.pallas_call(
        _in_proj_kernel,
        grid=(t // tm, P_COLS // tn),
        in_specs=[
            pl.BlockSpec((tm, D_MODEL), lambda i, j: (i, 0)),
            pl.BlockSpec((1, D_MODEL), lambda i, j: (0, 0)),
            pl.BlockSpec((D_MODEL, tn), lambda i, j: (0, j)),
            pl.BlockSpec((D_MODEL, LANES), lambda i, j: (0, 0)),
        ],
        out_specs=[
            pl.BlockSpec((tm, tn), lambda i, j: (i, j)),
            pl.BlockSpec((tm, LANES), lambda i, j: (i, 0)),
        ],
        out_shape=[jax.ShapeDtypeStruct((t, P_COLS), BF16), jax.ShapeDtypeStruct((t, LANES), F32)],
        scratch_shapes=[pltpu.VMEM((tm, D_MODEL), BF16)],
        compiler_params=_cparams(("arbitrary", "arbitrary"), 56),
        name="in_proj",
    )(x2d, g, wbig, wsmall)


def _mlstm_kernel(q_ref, k_ref, v_ref, o_ref, z_ref, g_ref, gb_ref, cw_ref, cb_ref, ng_ref,
                  y_ref, c_scr, n_scr, m_scr, xbuf):
    L = TILE

    @pl.when(pl.program_id(1) == 0)
    def _():
        c_scr[...] = jnp.zeros_like(c_scr)
        n_scr[...] = jnp.zeros_like(n_scr)
        m_scr[...] = jnp.zeros_like(m_scr)
        xbuf[0:SUBLANES, :] = jnp.zeros((SUBLANES, 2 * ML_W), F32)

    xbuf[SUBLANES:SUBLANES + L, 0:ML_W] = q_ref[...].astype(F32)
    xbuf[SUBLANES:SUBLANES + L, ML_W:2 * ML_W] = k_ref[...].astype(F32)
    conv = jnp.broadcast_to(cb_ref[...], (L, 2 * ML_W))
    for j in range(ML_CONV):
        off = SUBLANES - (ML_CONV - 1) + j
        conv = conv + cw_ref[j:j + 1, :] * xbuf[off:off + L, :]
    xbuf[0:SUBLANES, :] = xbuf[L:L + SUBLANES, :]
    qk = _silu(conv)

    gate = g_ref[...] + gb_ref[...]
    lf_cols = _log_sigmoid(gate)
    gate_t = gate.T
    li_rows = gate_t[G_MLI:G_MLI + SUBLANES, :]
    lf_rows = _log_sigmoid(gate_t[G_MLF:G_MLF + SUBLANES, :])
    r_io = lax.broadcasted_iota(I32, (L, L), 0)
    c_io = lax.broadcasted_iota(I32, (L, L), 1)
    causal = c_io <= r_io
    tri_u = (r_io <= c_io).astype(BF16)
    tri_l = causal.astype(BF16)
    bc_rows = _dot_f32_lhs(lf_rows, tri_u)
    bc_cols = _dot_f32_rhs(tri_l, lf_cols)

    for h in range(ML_HEADS):
        hs = slice(h * ML_HD, (h + 1) * ML_HD)
        qh = qk[:, hs]
        kh = qk[:, ML_W + h * ML_HD:ML_W + (h + 1) * ML_HD] * (ML_HD ** -0.5)
        vh = v_ref[:, hs]
        qb = qh.astype(BF16)
        kb = kh.astype(BF16)
        bc_col = bc_cols[:, G_MLF + h:G_MLF + h + 1]
        bc_row = bc_rows[h:h + 1, :]
        li_row = li_rows[h:h + 1, :]
        m_prev = m_scr[h, :, 0:1]

        dmat = jnp.where(causal, bc_col - (bc_row - li_row), -jnp.inf)
        inter = bc_col + m_prev
        m_row = jnp.maximum(inter, jnp.max(dmat, axis=1, keepdims=True))
        w_intra = jnp.exp(dmat - m_row)
        w_inter = jnp.exp(inter - m_row)
        s_qk = _dot_nt(qb, kb) * w_intra
        num = _dot(s_qk.astype(BF16), vh) + w_inter * _dot(qb, c_scr[h].astype(BF16))
        qn = jnp.sum(qh * n_scr[h], axis=1, keepdims=True)
        den = jnp.sum(s_qk, axis=1, keepdims=True) + w_inter * qn
        hh = num / jnp.maximum(jnp.abs(den), jnp.exp(-m_row))

        g_tot = bc_row[:, L - 1:L]
        a_row = g_tot - bc_row + li_row
        m_new = jnp.maximum(g_tot + m_prev, jnp.max(a_row, axis=1, keepdims=True))
        decay = jnp.exp(g_tot + m_prev - m_new)
        wa_row = jnp.exp(a_row - m_new)
        ktw = (kh.T * wa_row).astype(BF16)
        c_scr[h] = decay * c_scr[h] + _dot(ktw, vh)
        wa8 = jnp.broadcast_to(wa_row, (SUBLANES, L)).astype(BF16)
        n_scr[h] = decay * n_scr[h] + _dot(wa8, kb)[0:1, :]
        m_scr[h] = jnp.broadcast_to(m_new, (1, LANES))

        hm = hh * _sigmoid(o_ref[:, hs].astype(F32))
        hm = hm * lax.rsqrt(jnp.mean(hm * hm, axis=1, keepdims=True) + EPS) * ng_ref[:, hs]
        y_ref[:, hs] = (hm * _silu(z_ref[:, hs].astype(F32))).astype(BF16)


def _mlstm(p, g, gbias, conv_w, conv_b, norm_g, b, s):
    nt = s // TILE
    t = b * s

    def pcol(c):
        return pl.BlockSpec((TILE, ML_W), lambda bi, ti, c=c: (bi * nt + ti, c))

    full = lambda shape: pl.BlockSpec(shape, lambda bi, ti: (0,) * len(shape))
    return pl.pallas_call(
        _mlstm_kernel,
        grid=(b, nt),
        in_specs=[pcol(0), pcol(1), pcol(2), pcol(3), pcol(4),
                  pl.BlockSpec((TILE, LANES), lambda bi, ti: (bi * nt + ti, 0)),
                  full((1, LANES)), full((ML_CONV, 2 * ML_W)), full((1, 2 * ML_W)), full((1, ML_W))],
        out_specs=pl.BlockSpec((TILE, ML_W), lambda bi, ti: (bi * nt + ti, 0)),
        out_shape=jax.ShapeDtypeStruct((t, ML_W), BF16),
        scratch_shapes=[pltpu.VMEM((ML_HEADS, ML_HD, ML_HD), F32),
                        pltpu.VMEM((ML_HEADS, 1, ML_HD), F32),
                        pltpu.VMEM((ML_HEADS, 1, LANES), F32),
                        pltpu.VMEM((TILE + SUBLANES, 2 * ML_W), F32)],
        compiler_params=_cparams(("arbitrary", "arbitrary"), 40),
        name="mlstm",
    )(p, p, p, p, p, g, gbias, conv_w, conv_b, norm_g)


def _gla_kernel(q_ref, k_ref, v_ref, r_ref, g_ref, wa_ref, ba_ref, ng_ref, y_ref, st_scr):
    L = TILE
    H = GLA_HEADS

    @pl.when(pl.program_id(1) == 0)
    def _():
        st_scr[...] = jnp.zeros_like(st_scr)

    la = _log_sigmoid(_dot(g_ref[...].astype(BF16), wa_ref[...]) + ba_ref[...]) * (1.0 / GLA_TAU)
    r_io = lax.broadcasted_iota(I32, (L, L), 0)
    c_io = lax.broadcasted_iota(I32, (L, L), 1)
    tri_l = (c_io <= r_io).astype(BF16)
    bcum = _dot_f32_rhs(tri_l, la)
    btot = bcum[L - 1:L, :]
    bmid = bcum[L // 2 - 1:L // 2, :]

    q = q_ref[...].astype(F32) * (GLA_DK ** -0.5)
    k = k_ref[...].astype(F32)
    v = v_ref[...]
    q_in = (q * jnp.exp(bcum - bmid)).astype(BF16)
    k_in = k * jnp.exp(bmid - bcum)
    q_st = (q * jnp.exp(bcum)).astype(BF16)
    k_st = (k * jnp.exp(btot - bcum)).astype(BF16)

    kt = k_in.T
    kt4 = jnp.concatenate([kt] * H, axis=1)
    rr = lax.broadcasted_iota(I32, (GLA_WK, H * L), 0)
    cc = lax.broadcasted_iota(I32, (GLA_WK, H * L), 1)
    k_bd = jnp.where(rr // GLA_DK == cc // L, kt4, 0.0).astype(BF16)
    att = _dot(q_in, k_bd)
    ar = lax.broadcasted_iota(I32, (L, H * L), 0)
    ac = lax.broadcasted_iota(I32, (L, H * L), 1)
    att = jnp.where(ac % L <= ar, att, 0.0).astype(BF16)
    v4 = jnp.concatenate([v] * H, axis=0)
    vr = lax.broadcasted_iota(I32, (H * L, GLA_W), 0)
    vc = lax.broadcasted_iota(I32, (H * L, GLA_W), 1)
    v_bd = jnp.where(vr // L == vc // GLA_DV, v4, jnp.zeros_like(v4))
    o = _dot(att, v_bd) + _dot_nt(q_st, st_scr[...].astype(BF16))

    sr = lax.broadcasted_iota(I32, (GLA_W, GLA_WK), 0)
    sc = lax.broadcasted_iota(I32, (GLA_W, GLA_WK), 1)
    upd = _dot(v.astype(F32).T.astype(BF16), k_st)
    st_scr[...] = st_scr[...] * jnp.exp(btot) + jnp.where(sr // GLA_DV == sc // GLA_DK, upd, 0.0)

    for h in range(H):
        hs = slice(h * GLA_DV, (h + 1) * GLA_DV)
        oh = o[:, hs]
        oh = oh * lax.rsqrt(jnp.mean(oh * oh, axis=1, keepdims=True) + EPS) * ng_ref[:, hs]
        y_ref[:, hs] = (oh * _silu(r_ref[:, hs].astype(F32))).astype(BF16)


def _gla(p, g, wa_pad, b_a, norm_g, b, s):
    nt = s // TILE
    t = b * s

    def pcol(width, off):
        assert off % width == 0
        return pl.BlockSpec((TILE, width), lambda bi, ti: (bi * nt + ti, off // width))

    full = lambda shape: pl.BlockSpec(shape, lambda bi, ti: (0,) * len(shape))
    return pl.pallas_call(
        _gla_kernel,
        grid=(b, nt),
        in_specs=[pcol(GLA_WK, P_GLA_Q), pcol(GLA_WK, P_GLA_K), pcol(GLA_W, P_GLA_V), pcol(GLA_W, P_GLA_R),
                  pl.BlockSpec((TILE, LANES), lambda bi, ti: (bi * nt + ti, 0)),
                  full((LANES, GLA_WK)), full((1, GLA_WK)), full((1, GLA_W))],
        out_specs=pl.BlockSpec((TILE, GLA_W), lambda bi, ti: (bi * nt + ti, 0)),
        out_shape=jax.ShapeDtypeStruct((t, GLA_W), BF16),
        scratch_shapes=[pltpu.VMEM((GLA_W, GLA_WK), F32)],
        compiler_params=_cparams(("arbitrary", "arbitrary"), 32),
        name="gla",
    )(p, p, p, p, g, wa_pad, b_a, norm_g)


def _dsa_prep_kernel(q_ref, k_ref, v_ref, iq_ref, g_ref, qg_ref, kg_ref,
                     qt_ref, kn_ref, vt_ref, iqt_ref, ik_ref, iwt_ref):
    qn = []
    for h in range(DSA_HEADS):
        hs = slice(h * DSA_HD, (h + 1) * DSA_HD)
        q = q_ref[:, hs].astype(F32)
        k = k_ref[:, hs].astype(F32)
        qh = q * lax.rsqrt(jnp.mean(q * q, axis=1, keepdims=True) + EPS) * qg_ref[...]
        kh = k * lax.rsqrt(jnp.mean(k * k, axis=1, keepdims=True) + EPS) * kg_ref[...]
        qn.append(qh * (DSA_HD ** -0.5 * LOG2E))
        kn_ref[:, hs] = kh.astype(BF16)
        r0 = h * DSA_VROWS
        vt_ref[r0:r0 + DSA_HD, :] = v_ref[:, hs].astype(F32).T.astype(BF16)
        vt_ref[r0 + DSA_HD:r0 + DSA_VROWS, :] = jnp.ones((DSA_VROWS - DSA_HD, vt_ref.shape[1]), BF16)
    qt_ref[...] = jnp.concatenate(qn, axis=1).T.astype(BF16)
    iqt_ref[...] = iq_ref[...].astype(F32).T.astype(BF16)
    g = g_ref[...]
    ik_ref[...] = g[:, G_IDXK:G_IDXK + IDX_DIM].astype(BF16)
    iw_scale = (IDX_HEADS ** -0.5) * (IDX_DIM ** -0.5)
    iwt_ref[...] = g.T[G_IDXW:G_IDXW + IDX_HEADS, :] * iw_scale


def _dsa_prep(p, g, q_g, k_g, b, s):
    t = b * s
    tm = min(512, s)
    nt = s // tm
    full = lambda shape: pl.BlockSpec(shape, lambda bi, ti: (0,) * len(shape))
    tok = lambda width, cb: pl.BlockSpec((tm, width), lambda bi, ti: (bi * nt + ti, cb))
    feat = lambda rows: pl.BlockSpec((None, rows, tm), lambda bi, ti: (bi, 0, ti))
    return pl.pallas_call(
        _dsa_prep_kernel,
        grid=(b, nt),
        in_specs=[tok(DSA_W, P_DSA_Q // DSA_W), tok(DSA_W, P_DSA_K // DSA_W), tok(DSA_W, P_DSA_V // DSA_W),
                  tok(IDX_W, P_IDXQ // IDX_W), tok(LANES, 0), full((1, DSA_HD)), full((1, DSA_HD))],
        out_specs=[feat(DSA_W), tok(DSA_W, 0), feat(DSA_HEADS * DSA_VROWS), feat(IDX_W), tok(IDX_DIM, 0),
                   feat(IDX_HEADS)],
        out_shape=[jax.ShapeDtypeStruct((b, DSA_W, s), BF16),
                   jax.ShapeDtypeStruct((t, DSA_W), BF16),
                   jax.ShapeDtypeStruct((b, DSA_HEADS * DSA_VROWS, s), BF16),
                   jax.ShapeDtypeStruct((b, IDX_W, s), BF16),
                   jax.ShapeDtypeStruct((t, IDX_DIM), BF16),
                   jax.ShapeDtypeStruct((b, IDX_HEADS, s), F32)],
        compiler_params=_cparams(("arbitrary", "arbitrary"), 40),
        name="dsa_prep",
    )(p, p, p, p, g, q_g, k_g)


def _dsa_kernel(qt_ref, iqt_ref, iwt_ref, z_ref, kn_ref, vt_ref, ik_ref, bias_ref, y_ref,
                keys_scr, acc_scr, m_scr, iq2_scr, iw2_scr, s_scr, ml_scr, *, topk, idx_bits):
    Q, TK, SUB = TILE, DSA_TK, DSA_SUB
    i = pl.program_id(1)
    n_it = (i + SUB) // SUB
    qpos = i * Q + lax.broadcasted_iota(I32, (1, Q), 1)
    limit = (qpos // CHUNK + 1) * CHUNK
    krow = lax.broadcasted_iota(I32, (TK, Q), 0)

    def koff(kt):
        return pl.multiple_of(kt * TK, TK)

    for j in range(IDX_HEADS // 2):
        for e in range(2):
            h = 2 * j + e
            iq2_scr[j, :, e * Q:(e + 1) * Q] = iqt_ref[h * IDX_DIM:(h + 1) * IDX_DIM, :]
            iw2_scr[j, :, e * Q:(e + 1) * Q] = iwt_ref[h:h + 1, :]

    CR = 8 * SUBLANES

    def to_key(x):
        bits = lax.bitcast_convert_type(x, I32)
        return bits ^ ((bits >> 31) & 0x7FFFFFFF)

    def score_tile(kt, last):
        off = koff(kt)
        ikt = ik_ref[pl.ds(off, TK), :]
        sc = jnp.zeros((TK, Q), F32)
        for j in range(IDX_HEADS // 2):
            w = jnp.maximum(_dot(ikt, iq2_scr[j]), 0.0) * iw2_scr[j]
            sc = sc + (w[:, 0:Q] + w[:, Q:2 * Q])
        keys = to_key(sc)
        if last:
            keys = jnp.where(off + krow >= limit, INT_MIN, keys)
        keys_scr[pl.ds(off, TK), :] = keys

    def score_body(kt, carry):
        score_tile(kt, False)
        return carry

    lax.fori_loop(0, n_it - 1, score_body, 0)
    score_tile(n_it - 1, True)

    def count(pred):
        def body(kt, acc):
            off = koff(kt)
            ind = jnp.where(pred(keys_scr[pl.ds(off, TK), :], off), 1.0, 0.0)
            return acc + jnp.sum(ind.reshape(TK // CR, CR, Q), axis=0)
        acc = lax.fori_loop(0, n_it, body, jnp.zeros((CR, Q), F32))
        return jnp.sum(acc, axis=0, keepdims=True)

    take_all = limit <= topk

    def search_bits(t, b0, b1):
        def bit(b, t):
            cand = t + lax.shift_left(jnp.int32(1), 31 - b)
            return jnp.where(count(lambda kk, off: kk >= cand) >= topk, cand, t)
        return lax.fori_loop(b0, b1, bit, t)

    def settled(t):
        above = count(lambda kk, off: kk > t)
        return jnp.min(jnp.where(take_all | (above < topk), 1.0, 0.0)) > 0.0

    def search_from(t, stops):
        if len(stops) == 1:
            return search_bits(t, stops[0], 32)
        t = search_bits(t, stops[0], stops[1])
        return lax.cond(settled(t), lambda: t, lambda: search_from(t, stops[1:]))

    thr = search_from(jnp.full((1, Q), INT_MIN, I32), DSA_SEARCH_STOPS)
    thr = jnp.where(take_all, INT_MIN + 1, thr)
    n_ge = count(lambda kk, off: kk >= thr)

    @pl.when(jnp.max(n_ge) > topk)
    def _():
        need = topk - count(lambda kk, off: kk > thr)

        def idx_bit(b, j):
            cand = j + lax.shift_left(jnp.int32(1), idx_bits - 1 - b)
            n_before = count(lambda kk, off: (kk == thr) & (off + krow < cand))
            return jnp.where(n_before < need, cand, j)

        last = lax.fori_loop(0, idx_bits, idx_bit, jnp.zeros((1, Q), I32))

        def drop(kt, carry):
            off = koff(kt)
            kk = keys_scr[pl.ds(off, TK), :]
            keys_scr[pl.ds(off, TK), :] = jnp.where((kk == thr) & (off + krow > last), INT_MIN, kk)
            return carry

        lax.fori_loop(0, n_it, drop, 0)

    m_scr[...] = jnp.full(m_scr.shape, NEG_BIG, F32)
    acc_scr[...] = jnp.zeros_like(acc_scr)

    G = DSA_GROUP
    n_grp = (n_it + G - 1) // G

    def logits(kt, buf, thr_kt):
        off = koff(kt)
        sel = keys_scr[pl.ds(off, TK), :] >= thr_kt
        bsel = [jnp.clip(i - (kt * SUB + j), 0, 2) for j in range(SUB)]
        for h in range(DSA_HEADS):
            hs = slice(h * DSA_HD, (h + 1) * DSA_HD)
            s = _dot(kn_ref[pl.ds(off, TK), hs], qt_ref[hs, :])
            s = s + jnp.concatenate([bias_ref[h, bsel[j]] for j in range(SUB)], axis=0)
            s = jnp.where(sel, s, NEG_BIG)
            s_scr[buf, h] = s
            ml_scr[buf, h] = jnp.max(s, axis=0, keepdims=True)

    def values(kt, buf):
        off = koff(kt)
        for h in range(DSA_HEADS):
            m_old = m_scr[h]
            m_new = jnp.maximum(m_old, ml_scr[buf, h])
            p = jnp.exp2(s_scr[buf, h] - m_new).astype(BF16)
            v1 = vt_ref[h * DSA_VROWS:(h + 1) * DSA_VROWS, pl.ds(off, TK)]
            acc_scr[h] = jnp.exp2(m_old - m_new) * acc_scr[h] + _dot(v1, p)
            m_scr[h] = m_new

    def att_group(gi, carry):
        steps = []
        for e in range(G):
            kt = gi * G + e
            steps.append((jnp.minimum(kt, n_it - 1), jnp.where(kt < n_it, thr, jnp.int32(2 ** 31 - 1))))
        for e in range(G):
            logits(steps[e][0], e, steps[e][1])
        for e in range(G):
            values(steps[e][0], e)
        return carry

    lax.fori_loop(0, n_grp, att_group, 0)
    for h in range(DSA_HEADS):
        hs = slice(h * DSA_HD, (h + 1) * DSA_HD)
        acc = acc_scr[h]
        out = (acc[0:DSA_HD, :] / acc[DSA_HD:DSA_HD + 1, :]).T
        y_ref[:, hs] = (out * _silu(z_ref[:, hs].astype(F32))).astype(BF16)


def _dsa(p, qt, kn, vt, iqt, ik, iwt, bias_tiles, b, s):
    assert s % DSA_TK == 0
    nt = s // TILE
    t = b * s
    topk = min(TOPK_MAX, s // 4)
    idx_bits = max(1, (s - 1).bit_length())
    kernel = functools.partial(_dsa_kernel, topk=topk, idx_bits=idx_bits)
    once = pl.Buffered(1)
    qfeat = lambda rows: pl.BlockSpec((None, rows, TILE), lambda bi, ti: (bi, 0, ti))
    return pl.pallas_call(
        kernel,
        grid=(b, nt),
        in_specs=[qfeat(DSA_W), qfeat(IDX_W), qfeat(IDX_HEADS),
                  pl.BlockSpec((TILE, DSA_W), lambda bi, ti: (bi * nt + ti, P_DSA_Z // DSA_W)),
                  pl.BlockSpec((s, DSA_W), lambda bi, ti: (bi, 0), pipeline_mode=once),
                  pl.BlockSpec((None, DSA_HEADS * DSA_VROWS, s), lambda bi, ti: (bi, 0, 0), pipeline_mode=once),
                  pl.BlockSpec((s, IDX_DIM), lambda bi, ti: (bi, 0), pipeline_mode=once),
                  pl.BlockSpec((DSA_HEADS, 3, TILE, TILE), lambda bi, ti: (0, 0, 0, 0))],
        out_specs=pl.BlockSpec((TILE, DSA_W), lambda bi, ti: (bi * nt + ti, 0)),
        out_shape=jax.ShapeDtypeStruct((t, DSA_W), BF16),
        scratch_shapes=[pltpu.VMEM((s, TILE), I32),
                        pltpu.VMEM((DSA_HEADS, DSA_VROWS, TILE), F32),
                        pltpu.VMEM((DSA_HEADS, 1, TILE), F32),
                        pltpu.VMEM((IDX_HEADS // 2, IDX_DIM, 2 * TILE), BF16),
                        pltpu.VMEM((IDX_HEADS // 2, 1, 2 * TILE), F32),
                        pltpu.VMEM((DSA_GROUP, DSA_HEADS, DSA_TK, TILE), F32),
                        pltpu.VMEM((DSA_GROUP, DSA_HEADS, 1, TILE), F32)],
        compiler_params=_cparams(("arbitrary", "arbitrary"), 48),
        name="dsa",
    )(qt, iqt, iwt, p, kn, vt, ik, bias_tiles)


def _out_proj_kernel(yml_ref, ygla_ref, ydsa_ref, w_ref, x_ref, o_ref):
    y = _dot(yml_ref[...], w_ref[0:ML_W, :])
    y = y + _dot(ygla_ref[...], w_ref[ML_W:ML_W + GLA_W, :])
    y = y + _dot(ydsa_ref[...], w_ref[ML_W + GLA_W:D_MIX, :])
    o_ref[...] = x_ref[...] + y


def _out_proj(y_ml, y_gla, y_dsa, w_out, x2d):
    t = x2d.shape[0]
    tm = min(512, t)
    row = lambda width: pl.BlockSpec((tm, width), lambda i: (i, 0))
    return pl.pallas_call(
        _out_proj_kernel,
        grid=(t // tm,),
        in_specs=[row(ML_W), row(GLA_W), row(DSA_W),
                  pl.BlockSpec((D_MIX, D_MODEL), lambda i: (0, 0)), row(D_MODEL)],
        out_specs=row(D_MODEL),
        out_shape=jax.ShapeDtypeStruct((t, D_MODEL), F32),
        compiler_params=_cparams(("arbitrary",), 40),
        name="out_proj",
    )(y_ml, y_gla, y_dsa, w_out, x2d)


_COLUMN_LAYOUT = (
    ('ml_q', ML_W), ('ml_k', ML_W), ('ml_v', ML_W), ('ml_o', ML_W), ('ml_z', ML_W),
    ('ml_i', ML_HEADS), ('ml_f', ML_HEADS),
    ('gla_q', GLA_WK), ('gla_k', GLA_WK), ('gla_v', GLA_W), ('gla_a', GLA_RANK), ('gla_r', GLA_W),
    ('dsa_q', DSA_W), ('dsa_k', DSA_W), ('dsa_v', DSA_W), ('dsa_z', DSA_W),
    ('idx_q', IDX_W), ('idx_k', IDX_DIM), ('idx_w', IDX_HEADS),
)


def _split_w_in(w_in):
    cols, off = {}, 0
    for name, width in _COLUMN_LAYOUT:
        cols[name] = w_in[:, off:off + width]
        off += width
    return cols


def _pack_w_in(w_in):
    c = _split_w_in(w_in)
    d = w_in.shape[0]
    wbig = jnp.concatenate([c['ml_q'], c['ml_k'], c['ml_v'], c['ml_o'], c['ml_z'], c['idx_q'],
                            c['gla_q'], c['gla_k'], c['gla_v'], c['gla_r'],
                            c['dsa_q'], c['dsa_k'], c['dsa_v'], c['dsa_z']], axis=1)
    assert wbig.shape[1] == P_COLS
    z = lambda n: jnp.zeros((d, n), w_in.dtype)
    wsmall = jnp.concatenate([c['idx_k'], c['ml_i'], z(4), c['ml_f'], z(4), c['gla_a'], c['idx_w'],
                              z(LANES - G_IDXW - IDX_HEADS)], axis=1)
    assert wsmall.shape[1] == LANES
    return wbig.astype(BF16), wsmall.astype(BF16)


def _rel_bucket(rel):
    half = REL_BUCKETS // 2
    max_exact = half // 2
    ret = jnp.where(rel > 0, half, 0)
    n = jnp.abs(rel)
    nf = jnp.maximum(n, 1).astype(F32)
    large = max_exact + (jnp.log(nf / max_exact) / math.log(REL_MAX_DIST / max_exact)
                         * (half - max_exact)).astype(I32)
    large = jnp.minimum(large, half - 1)
    return ret + jnp.where(n < max_exact, n, large)


def _bias_tiles(rel_bias):
    assert REL_MAX_DIST <= TILE
    kl = jnp.arange(TILE, dtype=I32)[:, None]
    ql = jnp.arange(TILE, dtype=I32)[None, :]
    far = jnp.full((TILE, TILE), -(TILE + 1), I32)
    rel = jnp.stack([kl - ql, kl - TILE - ql, far])
    tab = jnp.transpose(rel_bias.astype(F32)[_rel_bucket(rel)], (3, 0, 1, 2))
    return (tab - tab[:, 2:3]) * LOG2E


def _layer(x2d, b, s, norm_g, w_in, ml_conv_w, ml_conv_b, ml_b_i, ml_b_f, ml_norm_g,
           gla_w_a, gla_b_a, gla_norm_g, dsa_q_g, dsa_k_g, w_out, bias_tiles):
    wbig, wsmall = _pack_w_in(w_in)
    p, g = _in_proj(x2d, norm_g.reshape(1, D_MODEL), wbig, wsmall)

    gbias = jnp.zeros((1, LANES), F32)
    gbias = gbias.at[0, G_MLI:G_MLI + ML_HEADS].set(ml_b_i).at[0, G_MLF:G_MLF + ML_HEADS].set(ml_b_f)
    y_ml = _mlstm(p, g, gbias, ml_conv_w, ml_conv_b.reshape(1, -1), ml_norm_g.reshape(1, -1), b, s)

    wa_pad = jnp.zeros((LANES, GLA_WK), F32).at[G_GLAA:G_GLAA + GLA_RANK].set(gla_w_a).astype(BF16)
    y_gla = _gla(p, g, wa_pad, gla_b_a.reshape(1, -1), gla_norm_g.reshape(1, -1), b, s)

    qt, kn, vt, iqt, ik, iwt = _dsa_prep(p, g, dsa_q_g.reshape(1, -1), dsa_k_g.reshape(1, -1), b, s)
    y_dsa = _dsa(p, qt, kn, vt, iqt, ik, iwt, bias_tiles, b, s)

    return _out_proj(y_ml, y_gla, y_dsa, w_out.astype(BF16), x2d)


def kernel(x, norm_g, w_in, ml_conv_w, ml_conv_b, ml_b_i, ml_b_f, ml_norm_g, gla_w_a, gla_b_a, gla_norm_g,
           dsa_q_g, dsa_k_g, w_out, rel_bias):
    b, s, d = x.shape
    assert d == D_MODEL and s % TILE == 0
    depth = norm_g.shape[0]
    bias_tiles = _bias_tiles(rel_bias)
    x2d = x.reshape(b * s, d)
    for layer in range(depth):
        x2d = _layer(x2d, b, s, norm_g[layer], w_in[layer], ml_conv_w[layer], ml_conv_b[layer],
                     ml_b_i[layer], ml_b_f[layer], ml_norm_g[layer], gla_w_a[layer], gla_b_a[layer],
                     gla_norm_g[layer], dsa_q_g[layer], dsa_k_g[layer], w_out[layer], bias_tiles)
    return x2d.reshape(b, s, d)
```

```python
import functools
import math

import jax
import jax.numpy as jnp
from jax import lax
from jax.experimental import pallas as pl
from jax.experimental.pallas import tpu as pltpu

F32 = jnp.float32
BF16 = jnp.bfloat16
I32 = jnp.int32

D_MODEL = 1024
CHUNK = 64
EPS = 1e-6
ML_HEADS, ML_HD = 4, 256
ML_W = ML_HEADS * ML_HD
ML_CONV = 4
GLA_HEADS, GLA_DK, GLA_DV = 4, 64, 128
GLA_WK = GLA_HEADS * GLA_DK
GLA_W = GLA_HEADS * GLA_DV
GLA_RANK = 16
GLA_TAU = 16.0
DSA_HEADS, DSA_HD = 4, 128
DSA_W = DSA_HEADS * DSA_HD
IDX_HEADS, IDX_DIM = 8, 64
IDX_W = IDX_HEADS * IDX_DIM
TOPK_MAX = 256
REL_BUCKETS, REL_MAX_DIST = 32, 128
D_MIX = ML_W + GLA_W + DSA_W

LANES = 128
SUBLANES = 8

TILE = 128
DSA_TK = 512
DSA_SUB = DSA_TK // TILE
DSA_GROUP = 2
DSA_VROWS = DSA_HD + 16
LOG2E = 1.4426950408889634

P_ML = 0
P_IDXQ = 5 * ML_W
P_GLA_Q = P_IDXQ + IDX_W
P_GLA_K = P_GLA_Q + GLA_WK
P_GLA_V = P_GLA_K + GLA_WK
P_GLA_R = P_GLA_V + GLA_W
P_DSA_Q = P_GLA_R + GLA_W
P_DSA_K = P_DSA_Q + DSA_W
P_DSA_V = P_DSA_K + DSA_W
P_DSA_Z = P_DSA_V + DSA_W
P_COLS = P_DSA_Z + DSA_W
G_IDXK = 0
G_MLI = 64
G_MLF = 72
G_GLAA = 80
G_IDXW = 96

INT_MIN = -(2 ** 31)
NEG_BIG = -1e30


def _cparams(sem, vmem_mb):
    return pltpu.CompilerParams(dimension_semantics=sem, vmem_limit_bytes=vmem_mb << 20)


def _dot(a, b):
    return jnp.dot(a, b, preferred_element_type=F32)


def _dot_nt(a, b):
    return lax.dot_general(a, b, (((1,), (1,)), ((), ())), preferred_element_type=F32)


def _split3(x):
    hi = x.astype(BF16)
    r1 = x - hi.astype(F32)
    mid = r1.astype(BF16)
    lo = (r1 - mid.astype(F32)).astype(BF16)
    return hi, mid, lo


def _dot_f32_lhs(x, m01):
    hi, mid, lo = _split3(x)
    return _dot(hi, m01) + _dot(mid, m01) + _dot(lo, m01)


def _dot_f32_rhs(m01, x):
    hi, mid, lo = _split3(x)
    return _dot(m01, hi) + _dot(m01, mid) + _dot(m01, lo)


def _log_sigmoid(x):
    return jnp.minimum(x, 0.0) - jnp.log(1.0 + jnp.exp(-jnp.abs(x)))


def _sigmoid(x):
    return 1.0 / (1.0 + jnp.exp(-x))


def _silu(x):
    return x * _sigmoid(x)


def _in_proj_kernel(x_ref, g_ref, wb_ref, ws_ref, p_ref, gate_ref, xn_ref):
    @pl.when(pl.program_id(1) == 0)
    def _():
        x = x_ref[...]
        ms = jnp.mean(x * x, axis=-1, keepdims=True)
        xn = (x * lax.rsqrt(ms + EPS) * g_ref[...]).astype(BF16)
        xn_ref[...] = xn
        gate_ref[...] = _dot(xn, ws_ref[...])

    p_ref[...] = _dot(xn_ref[...], wb_ref[...]).astype(BF16)


def _in_proj(x2d, g, wbig, wsmall):
    t = x2d.shape[0]
    tm = min(1024, t)
    tn = P_COLS // 4
    return pl.pallas_call(
        _in_proj_kernel,
        grid=(t // tm, P_COLS // tn),
        in_specs=[
            pl.BlockSpec((tm, D_MODEL), lambda i, j: (i, 0)),
            pl.BlockSpec((1, D_MODEL), lambda i, j: (0, 0)),
            pl.BlockSpec((D_MODEL, tn), lambda i, j: (0, j)),
            pl.BlockSpec((D_MODEL, LANES), lambda i, j: (0, 0)),
        ],
        out_specs=[
            pl.BlockSpec((tm, tn), lambda i, j: (i, j)),
            pl.BlockSpec((tm, LANES), lambda i, j: (i, 0)),
        ],
        out_shape=[jax.ShapeDtypeStruct((t, P_COLS), BF16), jax.ShapeDtypeStruct((t, LANES), F32)],
        scratch_shapes=[pltpu.VMEM((tm, D_MODEL), BF16)],
        compiler_params=_cparams(("arbitrary", "arbitrary"), 56),
        name="in_proj",
    )(x2d, g, wbig, wsmall)


def _mlstm_kernel(q_ref, k_ref, v_ref, o_ref, z_ref, g_ref, gb_ref, cw_ref, cb_ref, ng_ref,
                  y_ref, c_scr, n_scr, m_scr, xbuf):
    L = TILE

    @pl.when(pl.program_id(1) == 0)
    def _():
        c_scr[...] = jnp.zeros_like(c_scr)
        n_scr[...] = jnp.zeros_like(n_scr)
        m_scr[...] = jnp.zeros_like(m_scr)
        xbuf[0:SUBLANES, :] = jnp.zeros((SUBLANES, 2 * ML_W), F32)

    xbuf[SUBLANES:SUBLANES + L, 0:ML_W] = q_ref[...].astype(F32)
    xbuf[SUBLANES:SUBLANES + L, ML_W:2 * ML_W] = k_ref[...].astype(F32)
    conv = jnp.broadcast_to(cb_ref[...], (L, 2 * ML_W))
    for j in range(ML_CONV):
        off = SUBLANES - (ML_CONV - 1) + j
        conv = conv + cw_ref[j:j + 1, :] * xbuf[off:off + L, :]
    xbuf[0:SUBLANES, :] = xbuf[L:L + SUBLANES, :]
    qk = _silu(conv)

    gate = g_ref[...] + gb_ref[...]
    lf_cols = _log_sigmoid(gate)
    gate_t = gate.T
    li_rows = gate_t[G_MLI:G_MLI + SUBLANES, :]
    lf_rows = _log_sigmoid(gate_t[G_MLF:G_MLF + SUBLANES, :])
    r_io = lax.broadcasted_iota(I32, (L, L), 0)
    c_io = lax.broadcasted_iota(I32, (L, L), 1)
    causal = c_io <= r_io
    tri_u = (r_io <= c_io).astype(BF16)
    tri_l = causal.astype(BF16)
    bc_rows = _dot_f32_lhs(lf_rows, tri_u)
    bc_cols = _dot_f32_rhs(tri_l, lf_cols)

    for h in range(ML_HEADS):
        hs = slice(h * ML_HD, (h + 1) * ML_HD)
        qh = qk[:, hs]
        kh = qk[:, ML_W + h * ML_HD:ML_W + (h + 1) * ML_HD] * (ML_HD ** -0.5)
        vh = v_ref[:, hs]
        qb = qh.astype(BF16)
        kb = kh.astype(BF16)
        bc_col = bc_cols[:, G_MLF + h:G_MLF + h + 1]
        bc_row = bc_rows[h:h + 1, :]
        li_row = li_rows[h:h + 1, :]
        m_prev = m_scr[h, :, 0:1]

        dmat = jnp.where(causal, bc_col - (bc_row - li_row), -jnp.inf)
        inter = bc_col + m_prev
        m_row = jnp.maximum(inter, jnp.max(dmat, axis=1, keepdims=True))
        w_intra = jnp.exp(dmat - m_row)
        w_inter = jnp.exp(inter - m_row)
        s_qk = _dot_nt(qb, kb) * w_intra
        num = _dot(s_qk.astype(BF16), vh) + w_inter * _dot(qb, c_scr[h].astype(BF16))
        qn = jnp.sum(qh * n_scr[h], axis=1, keepdims=True)
        den = jnp.sum(s_qk, axis=1, keepdims=True) + w_inter * qn
        hh = num / jnp.maximum(jnp.abs(den), jnp.exp(-m_row))

        g_tot = bc_row[:, L - 1:L]
        a_row = g_tot - bc_row + li_row
        m_new = jnp.maximum(g_tot + m_prev, jnp.max(a_row, axis=1, keepdims=True))
        decay = jnp.exp(g_tot + m_prev - m_new)
        wa_row = jnp.exp(a_row - m_new)
        ktw = (kh.T * wa_row).astype(BF16)
        c_scr[h] = decay * c_scr[h] + _dot(ktw, vh)
        wa8 = jnp.broadcast_to(wa_row, (SUBLANES, L)).astype(BF16)
        n_scr[h] = decay * n_scr[h] + _dot(wa8, kb)[0:1, :]
        m_scr[h] = jnp.broadcast_to(m_new, (1, LANES))

        hm = hh * _sigmoid(o_ref[:, hs].astype(F32))
        hm = hm * lax.rsqrt(jnp.mean(hm * hm, axis=1, keepdims=True) + EPS) * ng_ref[:, hs]
        y_ref[:, hs] = (hm * _silu(z_ref[:, hs].astype(F32))).astype(BF16)


def _mlstm(p, g, gbias, conv_w, conv_b, norm_g, b, s):
    nt = s // TILE
    t = b * s

    def pcol(c):
        return pl.BlockSpec((TILE, ML_W), lambda bi, ti, c=c: (bi * nt + ti, c))

    full = lambda shape: pl.BlockSpec(shape, lambda bi, ti: (0,) * len(shape))
    return pl.pallas_call(
        _mlstm_kernel,
        grid=(b, nt),
        in_specs=[pcol(0), pcol(1), pcol(2), pcol(3), pcol(4),
                  pl.BlockSpec((TILE, LANES), lambda bi, ti: (bi * nt + ti, 0)),
                  full((1, LANES)), full((ML_CONV, 2 * ML_W)), full((1, 2 * ML_W)), full((1, ML_W))],
        out_specs=pl.BlockSpec((TILE, ML_W), lambda bi, ti: (bi * nt + ti, 0)),
        out_shape=jax.ShapeDtypeStruct((t, ML_W), BF16),
        scratch_shapes=[pltpu.VMEM((ML_HEADS, ML_HD, ML_HD), F32),
                        pltpu.VMEM((ML_HEADS, 1, ML_HD), F32),
                        pltpu.VMEM((ML_HEADS, 1, LANES), F32),
                        pltpu.VMEM((TILE + SUBLANES, 2 * ML_W), F32)],
        compiler_params=_cparams(("arbitrary", "arbitrary"), 40),
        name="mlstm",
    )(p, p, p, p, p, g, gbias, conv_w, conv_b, norm_g)


def _gla_kernel(q_ref, k_ref, v_ref, r_ref, g_ref, wa_ref, ba_ref, ng_ref, y_ref, st_scr):
    L = TILE
    H = GLA_HEADS

    @pl.when(pl.program_id(1) == 0)
    def _():
        st_scr[...] = jnp.zeros_like(st_scr)

    la = _log_sigmoid(_dot(g_ref[...].astype(BF16), wa_ref[...]) + ba_ref[...]) * (1.0 / GLA_TAU)
    r_io = lax.broadcasted_iota(I32, (L, L), 0)
    c_io = lax.broadcasted_iota(I32, (L, L), 1)
    tri_l = (c_io <= r_io).astype(BF16)
    bcum = _dot_f32_rhs(tri_l, la)
    btot = bcum[L - 1:L, :]
    bmid = bcum[L // 2 - 1:L // 2, :]

    q = q_ref[...].astype(F32) * (GLA_DK ** -0.5)
    k = k_ref[...].astype(F32)
    v = v_ref[...]
    q_in = (q * jnp.exp(bcum - bmid)).astype(BF16)
    k_in = k * jnp.exp(bmid - bcum)
    q_st = (q * jnp.exp(bcum)).astype(BF16)
    k_st = (k * jnp.exp(btot - bcum)).astype(BF16)

    kt = k_in.T
    kt4 = jnp.concatenate([kt] * H, axis=1)
    rr = lax.broadcasted_iota(I32, (GLA_WK, H * L), 0)
    cc = lax.broadcasted_iota(I32, (GLA_WK, H * L), 1)
    k_bd = jnp.where(rr // GLA_DK == cc // L, kt4, 0.0).astype(BF16)
    att = _dot(q_in, k_bd)
    ar = lax.broadcasted_iota(I32, (L, H * L), 0)
    ac = lax.broadcasted_iota(I32, (L, H * L), 1)
    att = jnp.where(ac % L <= ar, att, 0.0).astype(BF16)
    v4 = jnp.concatenate([v] * H, axis=0)
    vr = lax.broadcasted_iota(I32, (H * L, GLA_W), 0)
    vc = lax.broadcasted_iota(I32, (H * L, GLA_W), 1)
    v_bd = jnp.where(vr // L == vc // GLA_DV, v4, jnp.zeros_like(v4))
    o = _dot(att, v_bd) + _dot_nt(q_st, st_scr[...].astype(BF16))

    sr = lax.broadcasted_iota(I32, (GLA_W, GLA_WK), 0)
    sc = lax.broadcasted_iota(I32, (GLA_W, GLA_WK), 1)
    upd = _dot(v.astype(F32).T.astype(BF16), k_st)
    st_scr[...] = st_scr[...] * jnp.exp(btot) + jnp.where(sr // GLA_DV == sc // GLA_DK, upd, 0.0)

    for h in range(H):
        hs = slice(h * GLA_DV, (h + 1) * GLA_DV)
        oh = o[:, hs]
        oh = oh * lax.rsqrt(jnp.mean(oh * oh, axis=1, keepdims=True) + EPS) * ng_ref[:, hs]
        y_ref[:, hs] = (oh * _silu(r_ref[:, hs].astype(F32))).astype(BF16)


def _gla(p, g, wa_pad, b_a, norm_g, b, s):
    nt = s // TILE
    t = b * s

    def pcol(width, off):
        assert off % width == 0
        return pl.BlockSpec((TILE, width), lambda bi, ti: (bi * nt + ti, off // width))

    full = lambda shape: pl.BlockSpec(shape, lambda bi, ti: (0,) * len(shape))
    return pl.pallas_call(
        _gla_kernel,
        grid=(b, nt),
        in_specs=[pcol(GLA_WK, P_GLA_Q), pcol(GLA_WK, P_GLA_K), pcol(GLA_W, P_GLA_V), pcol(GLA_W, P_GLA_R),
                  pl.BlockSpec((TILE, LANES), lambda bi, ti: (bi * nt + ti, 0)),
                  full((LANES, GLA_WK)), full((1, GLA_WK)), full((1, GLA_W))],
        out_specs=pl.BlockSpec((TILE, GLA_W), lambda bi, ti: (bi * nt + ti, 0)),
        out_shape=jax.ShapeDtypeStruct((t, GLA_W), BF16),
        scratch_shapes=[pltpu.VMEM((GLA_W, GLA_WK), F32)],
        compiler_params=_cparams(("arbitrary", "arbitrary"), 32),
        name="gla",
    )(p, p, p, p, g, wa_pad, b_a, norm_g)


def _dsa_prep_kernel(q_ref, k_ref, v_ref, iq_ref, g_ref, qg_ref, kg_ref,
                     qt_ref, kn_ref, vt_ref, iqt_ref, ik_ref, iwt_ref):
    qn = []
    for h in range(DSA_HEADS):
        hs = slice(h * DSA_HD, (h + 1) * DSA_HD)
        q = q_ref[:, hs].astype(F32)
        k = k_ref[:, hs].astype(F32)
        qh = q * lax.rsqrt(jnp.mean(q * q, axis=1, keepdims=True) + EPS) * qg_ref[...]
        kh = k * lax.rsqrt(jnp.mean(k * k, axis=1, keepdims=True) + EPS) * kg_ref[...]
        qn.append(qh * (DSA_HD ** -0.5 * LOG2E))
        kn_ref[:, hs] = kh.astype(BF16)
        r0 = h * DSA_VROWS
        vt_ref[r0:r0 + DSA_HD, :] = v_ref[:, hs].astype(F32).T.astype(BF16)
        vt_ref[r0 + DSA_HD:r0 + DSA_VROWS, :] = jnp.ones((DSA_VROWS - DSA_HD, vt_ref.shape[1]), BF16)
    qt_ref[...] = jnp.concatenate(qn, axis=1).T.astype(BF16)
    iqt_ref[...] = iq_ref[...].astype(F32).T.astype(BF16)
    g = g_ref[...]
    ik_ref[...] = g[:, G_IDXK:G_IDXK + IDX_DIM].astype(BF16)
    iw_scale = (IDX_HEADS ** -0.5) * (IDX_DIM ** -0.5)
    iwt_ref[...] = g.T[G_IDXW:G_IDXW + IDX_HEADS, :] * iw_scale


def _dsa_prep(p, g, q_g, k_g, b, s):
    t = b * s
    tm = min(512, s)
    nt = s // tm
    full = lambda shape: pl.BlockSpec(shape, lambda bi, ti: (0,) * len(shape))
    tok = lambda width, cb: pl.BlockSpec((tm, width), lambda bi, ti: (bi * nt + ti, cb))
    feat = lambda rows: pl.BlockSpec((None, rows, tm), lambda bi, ti: (bi, 0, ti))
    return pl.pallas_call(
        _dsa_prep_kernel,
        grid=(b, nt),
        in_specs=[tok(DSA_W, P_DSA_Q // DSA_W), tok(DSA_W, P_DSA_K // DSA_W), tok(DSA_W, P_DSA_V // DSA_W),
                  tok(IDX_W, P_IDXQ // IDX_W), tok(LANES, 0), full((1, DSA_HD)), full((1, DSA_HD))],
        out_specs=[feat(DSA_W), tok(DSA_W, 0), feat(DSA_HEADS * DSA_VROWS), feat(IDX_W), tok(IDX_DIM, 0),
                   feat(IDX_HEADS)],
        out_shape=[jax.ShapeDtypeStruct((b, DSA_W, s), BF16),
                   jax.ShapeDtypeStruct((t, DSA_W), BF16),
                   jax.ShapeDtypeStruct((b, DSA_HEADS * DSA_VROWS, s), BF16),
                   jax.ShapeDtypeStruct((b, IDX_W, s), BF16),
                   jax.ShapeDtypeStruct((t, IDX_DIM), BF16),
                   jax.ShapeDtypeStruct((b, IDX_HEADS, s), F32)],
        compiler_params=_cparams(("arbitrary", "arbitrary"), 40),
        name="dsa_prep",
    )(p, p, p, p, g, q_g, k_g)


def _dsa_kernel(qt_ref, iqt_ref, iwt_ref, z_ref, kn_ref, vt_ref, ik_ref, bias_ref, y_ref,
                keys_scr, acc_scr, m_scr, iq2_scr, iw2_scr, s_scr, ml_scr, *, topk, idx_bits):
    Q, TK, SUB = TILE, DSA_TK, DSA_SUB
    i = pl.program_id(1)
    n_it = (i + SUB) // SUB
    qpos = i * Q + lax.broadcasted_iota(I32, (1, Q), 1)
    limit = (qpos // CHUNK + 1) * CHUNK
    krow = lax.broadcasted_iota(I32, (TK, Q), 0)

    def koff(kt):
        return pl.multiple_of(kt * TK, TK)

    for j in range(IDX_HEADS // 2):
        for e in range(2):
            h = 2 * j + e
            iq2_scr[j, :, e * Q:(e + 1) * Q] = iqt_ref[h * IDX_DIM:(h + 1) * IDX_DIM, :]
            iw2_scr[j, :, e * Q:(e + 1) * Q] = iwt_ref[h:h + 1, :]

    CR = 8 * SUBLANES

    def to_key(x):
        bits = lax.bitcast_convert_type(x, I32)
        return bits ^ ((bits >> 31) & 0x7FFFFFFF)

    def score_tile(kt, last):
        off = koff(kt)
        ikt = ik_ref[pl.ds(off, TK), :]
        sc = jnp.zeros((TK, Q), F32)
        for j in range(IDX_HEADS // 2):
            w = jnp.maximum(_dot(ikt, iq2_scr[j]), 0.0) * iw2_scr[j]
            sc = sc + (w[:, 0:Q] + w[:, Q:2 * Q])
        keys = to_key(sc)
        if last:
            keys = jnp.where(off + krow >= limit, INT_MIN, keys)
        keys_scr[pl.ds(off, TK), :] = keys

    def score_body(kt, carry):
        score_tile(kt, False)
        return carry

    lax.fori_loop(0, n_it - 1, score_body, 0)
    score_tile(n_it - 1, True)

    def count(pred):
        def body(kt, acc):
            off = koff(kt)
            ind = jnp.where(pred(keys_scr[pl.ds(off, TK), :], off), 1.0, 0.0)
            return acc + jnp.sum(ind.reshape(TK // CR, CR, Q), axis=0)
        acc = lax.fori_loop(0, n_it, body, jnp.zeros((CR, Q), F32))
        return jnp.sum(acc, axis=0, keepdims=True)

    take_all = limit <= topk

    def thr_bit(b, t):
        cand = t + lax.shift_left(jnp.int32(1), 31 - b)
        return jnp.where(count(lambda kk, off: kk >= cand) >= topk, cand, t)

    thr = lax.fori_loop(0, 32, thr_bit, jnp.full((1, Q), INT_MIN, I32))
    thr = jnp.where(take_all, INT_MIN + 1, thr)
    n_ge = count(lambda kk, off: kk >= thr)

    @pl.when(jnp.max(n_ge) > topk)
    def _():
        need = topk - count(lambda kk, off: kk > thr)

        def idx_bit(b, j):
            cand = j + lax.shift_left(jnp.int32(1), idx_bits - 1 - b)
            n_before = count(lambda kk, off: (kk == thr) & (off + krow < cand))
            return jnp.where(n_before < need, cand, j)

        last = lax.fori_loop(0, idx_bits, idx_bit, jnp.zeros((1, Q), I32))

        def drop(kt, carry):
            off = koff(kt)
            kk = keys_scr[pl.ds(off, TK), :]
            keys_scr[pl.ds(off, TK), :] = jnp.where((kk == thr) & (off + krow > last), INT_MIN, kk)
            return carry

        lax.fori_loop(0, n_it, drop, 0)

    m_scr[...] = jnp.full(m_scr.shape, NEG_BIG, F32)
    acc_scr[...] = jnp.zeros_like(acc_scr)

    G = DSA_GROUP
    n_grp = (n_it + G - 1) // G

    def logits(kt, buf, thr_kt):
        off = koff(kt)
        sel = keys_scr[pl.ds(off, TK), :] >= thr_kt
        bsel = [jnp.clip(i - (kt * SUB + j), 0, 2) for j in range(SUB)]
        for h in range(DSA_HEADS):
            hs = slice(h * DSA_HD, (h + 1) * DSA_HD)
            s = _dot(kn_ref[pl.ds(off, TK), hs], qt_ref[hs, :])
            s = s + jnp.concatenate([bias_ref[h, bsel[j]] for j in range(SUB)], axis=0)
            s = jnp.where(sel, s, NEG_BIG)
            s_scr[buf, h] = s
            ml_scr[buf, h] = jnp.max(s, axis=0, keepdims=True)

    def values(kt, buf):
        off = koff(kt)
        for h in range(DSA_HEADS):
            m_old = m_scr[h]
            m_new = jnp.maximum(m_old, ml_scr[buf, h])
            p = jnp.exp2(s_scr[buf, h] - m_new).astype(BF16)
            v1 = vt_ref[h * DSA_VROWS:(h + 1) * DSA_VROWS, pl.ds(off, TK)]
            acc_scr[h] = jnp.exp2(m_old - m_new) * acc_scr[h] + _dot(v1, p)
            m_scr[h] = m_new

    def att_group(gi, carry):
        steps = []
        for e in range(G):
            kt = gi * G + e
            steps.append((jnp.minimum(kt, n_it - 1), jnp.where(kt < n_it, thr, jnp.int32(2 ** 31 - 1))))
        for e in range(G):
            logits(steps[e][0], e, steps[e][1])
        for e in range(G):
            values(steps[e][0], e)
        return carry

    lax.fori_loop(0, n_grp, att_group, 0)
    for h in range(DSA_HEADS):
        hs = slice(h * DSA_HD, (h + 1) * DSA_HD)
        acc = acc_scr[h]
        out = (acc[0:DSA_HD, :] / acc[DSA_HD:DSA_HD + 1, :]).T
        y_ref[:, hs] = (out * _silu(z_ref[:, hs].astype(F32))).astype(BF16)


def _dsa(p, qt, kn, vt, iqt, ik, iwt, bias_tiles, b, s):
    assert s % DSA_TK == 0
    nt = s // TILE
    t = b * s
    topk = min(TOPK_MAX, s // 4)
    idx_bits = max(1, (s - 1).bit_length())
    kernel = functools.partial(_dsa_kernel, topk=topk, idx_bits=idx_bits)
    once = pl.Buffered(1)
    qfeat = lambda rows: pl.BlockSpec((None, rows, TILE), lambda bi, ti: (bi, 0, ti))
    return pl.pallas_call(
        kernel,
        grid=(b, nt),
        in_specs=[qfeat(DSA_W), qfeat(IDX_W), qfeat(IDX_HEADS),
                  pl.BlockSpec((TILE, DSA_W), lambda bi, ti: (bi * nt + ti, P_DSA_Z // DSA_W)),
                  pl.BlockSpec((s, DSA_W), lambda bi, ti: (bi, 0), pipeline_mode=once),
                  pl.BlockSpec((None, DSA_HEADS * DSA_VROWS, s), lambda bi, ti: (bi, 0, 0), pipeline_mode=once),
                  pl.BlockSpec((s, IDX_DIM), lambda bi, ti: (bi, 0), pipeline_mode=once),
                  pl.BlockSpec((DSA_HEADS, 3, TILE, TILE), lambda bi, ti: (0, 0, 0, 0))],
        out_specs=pl.BlockSpec((TILE, DSA_W), lambda bi, ti: (bi * nt + ti, 0)),
        out_shape=jax.ShapeDtypeStruct((t, DSA_W), BF16),
        scratch_shapes=[pltpu.VMEM((s, TILE), I32),
                        pltpu.VMEM((DSA_HEADS, DSA_VROWS, TILE), F32),
                        pltpu.VMEM((DSA_HEADS, 1, TILE), F32),
                        pltpu.VMEM((IDX_HEADS // 2, IDX_DIM, 2 * TILE), BF16),
                        pltpu.VMEM((IDX_HEADS // 2, 1, 2 * TILE), F32),
                        pltpu.VMEM((DSA_GROUP, DSA_HEADS, DSA_TK, TILE), F32),
                        pltpu.VMEM((DSA_GROUP, DSA_HEADS, 1, TILE), F32)],
        compiler_params=_cparams(("arbitrary", "arbitrary"), 48),
        name="dsa",
    )(qt, iqt, iwt, p, kn, vt, ik, bias_tiles)


def _out_proj_kernel(yml_ref, ygla_ref, ydsa_ref, w_ref, x_ref, o_ref):
    y = _dot(yml_ref[...], w_ref[0:ML_W, :])
    y = y + _dot(ygla_ref[...], w_ref[ML_W:ML_W + GLA_W, :])
    y = y + _dot(ydsa_ref[...], w_ref[ML_W + GLA_W:D_MIX, :])
    o_ref[...] = x_ref[...] + y


def _out_proj(y_ml, y_gla, y_dsa, w_out, x2d):
    t = x2d.shape[0]
    tm = min(1024, t)
    row = lambda width: pl.BlockSpec((tm, width), lambda i: (i, 0))
    return pl.pallas_call(
        _out_proj_kernel,
        grid=(t // tm,),
        in_specs=[row(ML_W), row(GLA_W), row(DSA_W),
                  pl.BlockSpec((D_MIX, D_MODEL), lambda i: (0, 0)), row(D_MODEL)],
        out_specs=row(D_MODEL),
        out_shape=jax.ShapeDtypeStruct((t, D_MODEL), F32),
        compiler_params=_cparams(("arbitrary",), 40),
        name="out_proj",
    )(y_ml, y_gla, y_dsa, w_out, x2d)


_COLUMN_LAYOUT = (
    ('ml_q', ML_W), ('ml_k', ML_W), ('ml_v', ML_W), ('ml_o', ML_W), ('ml_z', ML_W),
    ('ml_i', ML_HEADS), ('ml_f', ML_HEADS),
    ('gla_q', GLA_WK), ('gla_k', GLA_WK), ('gla_v', GLA_W), ('gla_a', GLA_RANK), ('gla_r', GLA_W),
    ('dsa_q', DSA_W), ('dsa_k', DSA_W), ('dsa_v', DSA_W), ('dsa_z', DSA_W),
    ('idx_q', IDX_W), ('idx_k', IDX_DIM), ('idx_w', IDX_HEADS),
)


def _split_w_in(w_in):
    cols, off = {}, 0
    for name, width in _COLUMN_LAYOUT:
        cols[name] = w_in[:, off:off + width]
        off += width
    return cols


def _pack_w_in(w_in):
    c = _split_w_in(w_in)
    d = w_in.shape[0]
    wbig = jnp.concatenate([c['ml_q'], c['ml_k'], c['ml_v'], c['ml_o'], c['ml_z'], c['idx_q'],
                            c['gla_q'], c['gla_k'], c['gla_v'], c['gla_r'],
                            c['dsa_q'], c['dsa_k'], c['dsa_v'], c['dsa_z']], axis=1)
    assert wbig.shape[1] == P_COLS
    z = lambda n: jnp.zeros((d, n), w_in.dtype)
    wsmall = jnp.concatenate([c['idx_k'], c['ml_i'], z(4), c['ml_f'], z(4), c['gla_a'], c['idx_w'],
                              z(LANES - G_IDXW - IDX_HEADS)], axis=1)
    assert wsmall.shape[1] == LANES
    return wbig.astype(BF16), wsmall.astype(BF16)


def _rel_bucket(rel):
    half = REL_BUCKETS // 2
    max_exact = half // 2
    ret = jnp.where(rel > 0, half, 0)
    n = jnp.abs(rel)
    nf = jnp.maximum(n, 1).astype(F32)
    large = max_exact + (jnp.log(nf / max_exact) / math.log(REL_MAX_DIST / max_exact)
                         * (half - max_exact)).astype(I32)
    large = jnp.minimum(large, half - 1)
    return ret + jnp.where(n < max_exact, n, large)


def _bias_tiles(rel_bias):
    assert REL_MAX_DIST <= TILE
    kl = jnp.arange(TILE, dtype=I32)[:, None]
    ql = jnp.arange(TILE, dtype=I32)[None, :]
    far = jnp.full((TILE, TILE), -(TILE + 1), I32)
    rel = jnp.stack([kl - ql, kl - TILE - ql, far])
    tab = jnp.transpose(rel_bias.astype(F32)[_rel_bucket(rel)], (3, 0, 1, 2))
    return (tab - tab[:, 2:3]) * LOG2E


def _layer(x2d, b, s, norm_g, w_in, ml_conv_w, ml_conv_b, ml_b_i, ml_b_f, ml_norm_g,
           gla_w_a, gla_b_a, gla_norm_g, dsa_q_g, dsa_k_g, w_out, bias_tiles):
    wbig, wsmall = _pack_w_in(w_in)
    p, g = _in_proj(x2d, norm_g.reshape(1, D_MODEL), wbig, wsmall)

    gbias = jnp.zeros((1, LANES), F32)
    gbias = gbias.at[0, G_MLI:G_MLI + ML_HEADS].set(ml_b_i).at[0, G_MLF:G_MLF + ML_HEADS].set(ml_b_f)
    y_ml = _mlstm(p, g, gbias, ml_conv_w, ml_conv_b.reshape(1, -1), ml_norm_g.reshape(1, -1), b, s)

    wa_pad = jnp.zeros((LANES, GLA_WK), F32).at[G_GLAA:G_GLAA + GLA_RANK].set(gla_w_a).astype(BF16)
    y_gla = _gla(p, g, wa_pad, gla_b_a.reshape(1, -1), gla_norm_g.reshape(1, -1), b, s)

    qt, kn, vt, iqt, ik, iwt = _dsa_prep(p, g, dsa_q_g.reshape(1, -1), dsa_k_g.reshape(1, -1), b, s)
    y_dsa = _dsa(p, qt, kn, vt, iqt, ik, iwt, bias_tiles, b, s)

    return _out_proj(y_ml, y_gla, y_dsa, w_out.astype(BF16), x2d)


def kernel(x, norm_g, w_in, ml_conv_w, ml_conv_b, ml_b_i, ml_b_f, ml_norm_g, gla_w_a, gla_b_a, gla_norm_g,
           dsa_q_g, dsa_k_g, w_out, rel_bias):
    b, s, d = x.shape
    assert d == D_MODEL and s % TILE == 0
    depth = norm_g.shape[0]
    bias_tiles = _bias_tiles(rel_bias)
    x2d = x.reshape(b * s, d)
    for layer in range(depth):
        x2d = _layer(x2d, b, s, norm_g[layer], w_in[layer], ml_conv_w[layer], ml_conv_b[layer],
                     ml_b_i[layer], ml_b_f[layer], ml_norm_g[layer], gla_w_a[layer], gla_b_a[layer],
                     gla_norm_g[layer], dsa_q_g[layer], dsa_k_g[layer], w_out[layer], bias_tiles)
    return x2d.reshape(b, s, d)
```

```python
import functools
import math

import jax
import jax.numpy as jnp
from jax import lax
from jax.experimental import pallas as pl
from jax.experimental.pallas import tpu as pltpu

F32 = jnp.float32
BF16 = jnp.bfloat16
I32 = jnp.int32

D_MODEL = 1024
CHUNK = 64
EPS = 1e-6
ML_HEADS, ML_HD = 4, 256
ML_W = ML_HEADS * ML_HD
ML_CONV = 4
GLA_HEADS, GLA_DK, GLA_DV = 4, 64, 128
GLA_WK = GLA_HEADS * GLA_DK
GLA_W = GLA_HEADS * GLA_DV
GLA_RANK = 16
GLA_TAU = 16.0
DSA_HEADS, DSA_HD = 4, 128
DSA_W = DSA_HEADS * DSA_HD
IDX_HEADS, IDX_DIM = 8, 64
IDX_W = IDX_HEADS * IDX_DIM
TOPK_MAX = 256
REL_BUCKETS, REL_MAX_DIST = 32, 128
D_MIX = ML_W + GLA_W + DSA_W

LANES = 128
SUBLANES = 8

TILE = 128
DSA_TK = 512
DSA_SUB = DSA_TK // TILE
DSA_GROUP = 2
DSA_VROWS = DSA_HD + 16
LOG2E = 1.4426950408889634

P_ML = 0
P_IDXQ = 5 * ML_W
P_GLA_Q = P_IDXQ + IDX_W
P_GLA_K = P_GLA_Q + GLA_WK
P_GLA_V = P_GLA_K + GLA_WK
P_GLA_R = P_GLA_V + GLA_W
P_DSA_Q = P_GLA_R + GLA_W
P_DSA_K = P_DSA_Q + DSA_W
P_DSA_V = P_DSA_K + DSA_W
P_DSA_Z = P_DSA_V + DSA_W
P_COLS = P_DSA_Z + DSA_W
G_IDXK = 0
G_MLI = 64
G_MLF = 72
G_GLAA = 80
G_IDXW = 96

INT_MIN = -(2 ** 31)
NEG_BIG = -1e30


def _cparams(sem, vmem_mb):
    return pltpu.CompilerParams(dimension_semantics=sem, vmem_limit_bytes=vmem_mb << 20)


def _dot(a, b):
    return jnp.dot(a, b, preferred_element_type=F32)


def _dot_nt(a, b):
    return lax.dot_general(a, b, (((1,), (1,)), ((), ())), preferred_element_type=F32)


def _split3(x):
    hi = x.astype(BF16)
    r1 = x - hi.astype(F32)
    mid = r1.astype(BF16)
    lo = (r1 - mid.astype(F32)).astype(BF16)
    return hi, mid, lo


def _dot_f32_lhs(x, m01):
    hi, mid, lo = _split3(x)
    return _dot(hi, m01) + _dot(mid, m01) + _dot(lo, m01)


def _dot_f32_rhs(m01, x):
    hi, mid, lo = _split3(x)
    return _dot(m01, hi) + _dot(m01, mid) + _dot(m01, lo)


def _log_sigmoid(x):
    return jnp.minimum(x, 0.0) - jnp.log(1.0 + jnp.exp(-jnp.abs(x)))


def _sigmoid(x):
    return 1.0 / (1.0 + jnp.exp(-x))


def _silu(x):
    return x * _sigmoid(x)


def _in_proj_kernel(x_ref, g_ref, wb_ref, ws_ref, p_ref, gate_ref, xn_ref):
    @pl.when(pl.program_id(1) == 0)
    def _():
        x = x_ref[...]
        ms = jnp.mean(x * x, axis=-1, keepdims=True)
        xn = (x * lax.rsqrt(ms + EPS) * g_ref[...]).astype(BF16)
        xn_ref[...] = xn
        gate_ref[...] = _dot(xn, ws_ref[...])

    p_ref[...] = _dot(xn_ref[...], wb_ref[...]).astype(BF16)


def _in_proj(x2d, g, wbig, wsmall):
    t = x2d.shape[0]
    tm = min(1024, t)
    tn = P_COLS // 4
    return pl.pallas_call(
        _in_proj_kernel,
        grid=(t // tm, P_COLS // tn),
        in_specs=[
            pl.BlockSpec((tm, D_MODEL), lambda i, j: (i, 0)),
            pl.BlockSpec((1, D_MODEL), lambda i, j: (0, 0)),
            pl.BlockSpec((D_MODEL, tn), lambda i, j: (0, j)),
            pl.BlockSpec((D_MODEL, LANES), lambda i, j: (0, 0)),
        ],
        out_specs=[
            pl.BlockSpec((tm, tn), lambda i, j: (i, j)),
            pl.BlockSpec((tm, LANES), lambda i, j: (i, 0)),
        ],
        out_shape=[jax.ShapeDtypeStruct((t, P_COLS), BF16), jax.ShapeDtypeStruct((t, LANES), F32)],
        scratch_shapes=[pltpu.VMEM((tm, D_MODEL), BF16)],
        compiler_params=_cparams(("arbitrary", "arbitrary"), 56),
        name="in_proj",
    )(x2d, g, wbig, wsmall)


def _mlstm_kernel(q_ref, k_ref, v_ref, o_ref, z_ref, g_ref, gb_ref, cw_ref, cb_ref, ng_ref,
                  y_ref, c_scr, n_scr, m_scr, xbuf):
    L = TILE

    @pl.when(pl.program_id(1) == 0)
    def _():
        c_scr[...] = jnp.zeros_like(c_scr)
        n_scr[...] = jnp.zeros_like(n_scr)
        m_scr[...] = jnp.zeros_like(m_scr)
        xbuf[0:SUBLANES, :] = jnp.zeros((SUBLANES, 2 * ML_W), F32)

    xbuf[SUBLANES:SUBLANES + L, 0:ML_W] = q_ref[...].astype(F32)
    xbuf[SUBLANES:SUBLANES + L, ML_W:2 * ML_W] = k_ref[...].astype(F32)
    conv = jnp.broadcast_to(cb_ref[...], (L, 2 * ML_W))
    for j in range(ML_CONV):
        off = SUBLANES - (ML_CONV - 1) + j
        conv = conv + cw_ref[j:j + 1, :] * xbuf[off:off + L, :]
    xbuf[0:SUBLANES, :] = xbuf[L:L + SUBLANES, :]
    qk = _silu(conv)

    gate = g_ref[...] + gb_ref[...]
    lf_cols = _log_sigmoid(gate)
    gate_t = gate.T
    li_rows = gate_t[G_MLI:G_MLI + SUBLANES, :]
    lf_rows = _log_sigmoid(gate_t[G_MLF:G_MLF + SUBLANES, :])
    r_io = lax.broadcasted_iota(I32, (L, L), 0)
    c_io = lax.broadcasted_iota(I32, (L, L), 1)
    causal = c_io <= r_io
    tri_u = (r_io <= c_io).astype(BF16)
    tri_l = causal.astype(BF16)
    bc_rows = _dot_f32_lhs(lf_rows, tri_u)
    bc_cols = _dot_f32_rhs(tri_l, lf_cols)

    for h in range(ML_HEADS):
        hs = slice(h * ML_HD, (h + 1) * ML_HD)
        qh = qk[:, hs]
        kh = qk[:, ML_W + h * ML_HD:ML_W + (h + 1) * ML_HD] * (ML_HD ** -0.5)
        vh = v_ref[:, hs]
        qb = qh.astype(BF16)
        kb = kh.astype(BF16)
        bc_col = bc_cols[:, G_MLF + h:G_MLF + h + 1]
        bc_row = bc_rows[h:h + 1, :]
        li_row = li_rows[h:h + 1, :]
        m_prev = m_scr[h, :, 0:1]

        dmat = jnp.where(causal, bc_col - (bc_row - li_row), -jnp.inf)
        inter = bc_col + m_prev
        m_row = jnp.maximum(inter, jnp.max(dmat, axis=1, keepdims=True))
        w_intra = jnp.exp(dmat - m_row)
        w_inter = jnp.exp(inter - m_row)
        s_qk = _dot_nt(qb, kb) * w_intra
        num = _dot(s_qk.astype(BF16), vh) + w_inter * _dot(qb, c_scr[h].astype(BF16))
        qn = jnp.sum(qh * n_scr[h], axis=1, keepdims=True)
        den = jnp.sum(s_qk, axis=1, keepdims=True) + w_inter * qn
        hh = num / jnp.maximum(jnp.abs(den), jnp.exp(-m_row))

        g_tot = bc_row[:, L - 1:L]
        a_row = g_tot - bc_row + li_row
        m_new = jnp.maximum(g_tot + m_prev, jnp.max(a_row, axis=1, keepdims=True))
        decay = jnp.exp(g_tot + m_prev - m_new)
        wa_row = jnp.exp(a_row - m_new)
        ktw = (kh.T * wa_row).astype(BF16)
        c_scr[h] = decay * c_scr[h] + _dot(ktw, vh)
        wa8 = jnp.broadcast_to(wa_row, (SUBLANES, L)).astype(BF16)
        n_scr[h] = decay * n_scr[h] + _dot(wa8, kb)[0:1, :]
        m_scr[h] = jnp.broadcast_to(m_new, (1, LANES))

        hm = hh * _sigmoid(o_ref[:, hs].astype(F32))
        hm = hm * lax.rsqrt(jnp.mean(hm * hm, axis=1, keepdims=True) + EPS) * ng_ref[:, hs]
        y_ref[:, hs] = (hm * _silu(z_ref[:, hs].astype(F32))).astype(BF16)


def _mlstm(p, g, gbias, conv_w, conv_b, norm_g, b, s):
    nt = s // TILE
    t = b * s

    def pcol(c):
        return pl.BlockSpec((TILE, ML_W), lambda bi, ti, c=c: (bi * nt + ti, c))

    full = lambda shape: pl.BlockSpec(shape, lambda bi, ti: (0,) * len(shape))
    return pl.pallas_call(
        _mlstm_kernel,
        grid=(b, nt),
        in_specs=[pcol(0), pcol(1), pcol(2), pcol(3), pcol(4),
                  pl.BlockSpec((TILE, LANES), lambda bi, ti: (bi * nt + ti, 0)),
                  full((1, LANES)), full((ML_CONV, 2 * ML_W)), full((1, 2 * ML_W)), full((1, ML_W))],
        out_specs=pl.BlockSpec((TILE, ML_W), lambda bi, ti: (bi * nt + ti, 0)),
        out_shape=jax.ShapeDtypeStruct((t, ML_W), BF16),
        scratch_shapes=[pltpu.VMEM((ML_HEADS, ML_HD, ML_HD), F32),
                        pltpu.VMEM((ML_HEADS, 1, ML_HD), F32),
                        pltpu.VMEM((ML_HEADS, 1, LANES), F32),
                        pltpu.VMEM((TILE + SUBLANES, 2 * ML_W), F32)],
        compiler_params=_cparams(("arbitrary", "arbitrary"), 40),
        name="mlstm",
    )(p, p, p, p, p, g, gbias, conv_w, conv_b, norm_g)


def _gla_kernel(q_ref, k_ref, v_ref, r_ref, g_ref, wa_ref, ba_ref, ng_ref, y_ref, st_scr):
    L = TILE
    H = GLA_HEADS

    @pl.when(pl.program_id(1) == 0)
    def _():
        st_scr[...] = jnp.zeros_like(st_scr)

    la = _log_sigmoid(_dot(g_ref[...].astype(BF16), wa_ref[...]) + ba_ref[...]) * (1.0 / GLA_TAU)
    r_io = lax.broadcasted_iota(I32, (L, L), 0)
    c_io = lax.broadcasted_iota(I32, (L, L), 1)
    tri_l = (c_io <= r_io).astype(BF16)
    bcum = _dot_f32_rhs(tri_l, la)
    btot = bcum[L - 1:L, :]
    bmid = bcum[L // 2 - 1:L // 2, :]

    q = q_ref[...].astype(F32) * (GLA_DK ** -0.5)
    k = k_ref[...].astype(F32)
    v = v_ref[...]
    q_in = (q * jnp.exp(bcum - bmid)).astype(BF16)
    k_in = k * jnp.exp(bmid - bcum)
    q_st = (q * jnp.exp(bcum)).astype(BF16)
    k_st = (k * jnp.exp(btot - bcum)).astype(BF16)

    kt = k_in.T
    kt4 = jnp.concatenate([kt] * H, axis=1)
    rr = lax.broadcasted_iota(I32, (GLA_WK, H * L), 0)
    cc = lax.broadcasted_iota(I32, (GLA_WK, H * L), 1)
    k_bd = jnp.where(rr // GLA_DK == cc // L, kt4, 0.0).astype(BF16)
    att = _dot(q_in, k_bd)
    ar = lax.broadcasted_iota(I32, (L, H * L), 0)
    ac = lax.broadcasted_iota(I32, (L, H * L), 1)
    att = jnp.where(ac % L <= ar, att, 0.0).astype(BF16)
    v4 = jnp.concatenate([v] * H, axis=0)
    vr = lax.broadcasted_iota(I32, (H * L, GLA_W), 0)
    vc = lax.broadcasted_iota(I32, (H * L, GLA_W), 1)
    v_bd = jnp.where(vr // L == vc // GLA_DV, v4, jnp.zeros_like(v4))
    o = _dot(att, v_bd) + _dot_nt(q_st, st_scr[...].astype(BF16))

    sr = lax.broadcasted_iota(I32, (GLA_W, GLA_WK), 0)
    sc = lax.broadcasted_iota(I32, (GLA_W, GLA_WK), 1)
    upd = _dot(v.astype(F32).T.astype(BF16), k_st)
    st_scr[...] = st_scr[...] * jnp.exp(btot) + jnp.where(sr // GLA_DV == sc // GLA_DK, upd, 0.0)

    for h in range(H):
        hs = slice(h * GLA_DV, (h + 1) * GLA_DV)
        oh = o[:, hs]
        oh = oh * lax.rsqrt(jnp.mean(oh * oh, axis=1, keepdims=True) + EPS) * ng_ref[:, hs]
        y_ref[:, hs] = (oh * _silu(r_ref[:, hs].astype(F32))).astype(BF16)


def _gla(p, g, wa_pad, b_a, norm_g, b, s):
    nt = s // TILE
    t = b * s

    def pcol(width, off):
        assert off % width == 0
        return pl.BlockSpec((TILE, width), lambda bi, ti: (bi * nt + ti, off // width))

    full = lambda shape: pl.BlockSpec(shape, lambda bi, ti: (0,) * len(shape))
    return pl.pallas_call(
        _gla_kernel,
        grid=(b, nt),
        in_specs=[pcol(GLA_WK, P_GLA_Q), pcol(GLA_WK, P_GLA_K), pcol(GLA_W, P_GLA_V), pcol(GLA_W, P_GLA_R),
                  pl.BlockSpec((TILE, LANES), lambda bi, ti: (bi * nt + ti, 0)),
                  full((LANES, GLA_WK)), full((1, GLA_WK)), full((1, GLA_W))],
        out_specs=pl.BlockSpec((TILE, GLA_W), lambda bi, ti: (bi * nt + ti, 0)),
        out_shape=jax.ShapeDtypeStruct((t, GLA_W), BF16),
        scratch_shapes=[pltpu.VMEM((GLA_W, GLA_WK), F32)],
        compiler_params=_cparams(("arbitrary", "arbitrary"), 32),
        name="gla",
    )(p, p, p, p, g, wa_pad, b_a, norm_g)


def _dsa_prep_kernel(q_ref, k_ref, v_ref, iq_ref, g_ref, qg_ref, kg_ref,
                     qt_ref, kn_ref, vt_ref, iqt_ref, ik_ref, iwt_ref):
    qn = []
    for h in range(DSA_HEADS):
        hs = slice(h * DSA_HD, (h + 1) * DSA_HD)
        q = q_ref[:, hs].astype(F32)
        k = k_ref[:, hs].astype(F32)
        qh = q * lax.rsqrt(jnp.mean(q * q, axis=1, keepdims=True) + EPS) * qg_ref[...]
        kh = k * lax.rsqrt(jnp.mean(k * k, axis=1, keepdims=True) + EPS) * kg_ref[...]
        qn.append(qh * (DSA_HD ** -0.5 * LOG2E))
        kn_ref[:, hs] = kh.astype(BF16)
        r0 = h * DSA_VROWS
        vt_ref[r0:r0 + DSA_HD, :] = v_ref[:, hs].astype(F32).T.astype(BF16)
        vt_ref[r0 + DSA_HD:r0 + DSA_VROWS, :] = jnp.ones((DSA_VROWS - DSA_HD, vt_ref.shape[1]), BF16)
    qt_ref[...] = jnp.concatenate(qn, axis=1).T.astype(BF16)
    iqt_ref[...] = iq_ref[...].astype(F32).T.astype(BF16)
    g = g_ref[...]
    ik_ref[...] = g[:, G_IDXK:G_IDXK + IDX_DIM].astype(BF16)
    iw_scale = (IDX_HEADS ** -0.5) * (IDX_DIM ** -0.5)
    iwt_ref[...] = g.T[G_IDXW:G_IDXW + IDX_HEADS, :] * iw_scale


def _dsa_prep(p, g, q_g, k_g, b, s):
    t = b * s
    tm = min(512, s)
    nt = s // tm
    full = lambda shape: pl.BlockSpec(shape, lambda bi, ti: (0,) * len(shape))
    tok = lambda width, cb: pl.BlockSpec((tm, width), lambda bi, ti: (bi * nt + ti, cb))
    feat = lambda rows: pl.BlockSpec((None, rows, tm), lambda bi, ti: (bi, 0, ti))
    return pl.pallas_call(
        _dsa_prep_kernel,
        grid=(b, nt),
        in_specs=[tok(DSA_W, P_DSA_Q // DSA_W), tok(DSA_W, P_DSA_K // DSA_W), tok(DSA_W, P_DSA_V // DSA_W),
                  tok(IDX_W, P_IDXQ // IDX_W), tok(LANES, 0), full((1, DSA_HD)), full((1, DSA_HD))],
        out_specs=[feat(DSA_W), tok(DSA_W, 0), feat(DSA_HEADS * DSA_VROWS), feat(IDX_W), tok(IDX_DIM, 0),
                   feat(IDX_HEADS)],
        out_shape=[jax.ShapeDtypeStruct((b, DSA_W, s), BF16),
                   jax.ShapeDtypeStruct((t, DSA_W), BF16),
                   jax.ShapeDtypeStruct((b, DSA_HEADS * DSA_VROWS, s), BF16),
                   jax.ShapeDtypeStruct((b, IDX_W, s), BF16),
                   jax.ShapeDtypeStruct((t, IDX_DIM), BF16),
                   jax.ShapeDtypeStruct((b, IDX_HEADS, s), F32)],
        compiler_params=_cparams(("arbitrary", "arbitrary"), 40),
        name="dsa_prep",
    )(p, p, p, p, g, q_g, k_g)


def _dsa_kernel(qt_ref, iqt_ref, iwt_ref, z_ref, kn_ref, vt_ref, ik_ref, bias_ref, y_ref,
                keys_scr, acc_scr, m_scr, iq2_scr, iw2_scr, s_scr, ml_scr, *, topk, idx_bits):
    Q, TK, SUB = TILE, DSA_TK, DSA_SUB
    i = pl.program_id(1)
    n_it = (i + SUB) // SUB
    qpos = i * Q + lax.broadcasted_iota(I32, (1, Q), 1)
    limit = (qpos // CHUNK + 1) * CHUNK
    krow = lax.broadcasted_iota(I32, (TK, Q), 0)

    def koff(kt):
        return pl.multiple_of(kt * TK, TK)

    for j in range(IDX_HEADS // 2):
        for e in range(2):
            h = 2 * j + e
            iq2_scr[j, :, e * Q:(e + 1) * Q] = iqt_ref[h * IDX_DIM:(h + 1) * IDX_DIM, :]
            iw2_scr[j, :, e * Q:(e + 1) * Q] = iwt_ref[h:h + 1, :]

    CR = 8 * SUBLANES

    def to_key(x):
        bits = lax.bitcast_convert_type(x, I32)
        return bits ^ ((bits >> 31) & 0x7FFFFFFF)

    def score_tile(kt, last):
        off = koff(kt)
        ikt = ik_ref[pl.ds(off, TK), :]
        sc = jnp.zeros((TK, Q), F32)
        for j in range(IDX_HEADS // 2):
            w = jnp.maximum(_dot(ikt, iq2_scr[j]), 0.0) * iw2_scr[j]
            sc = sc + (w[:, 0:Q] + w[:, Q:2 * Q])
        keys = to_key(sc)
        if last:
            keys = jnp.where(off + krow >= limit, INT_MIN, keys)
        keys_scr[pl.ds(off, TK), :] = keys

    def score_pair(jp, carry):
        score_tile(2 * jp, False)
        score_tile(2 * jp + 1, False)
        return carry

    n_full = n_it - 1
    lax.fori_loop(0, n_full // 2, score_pair, 0)

    @pl.when(n_full % 2 == 1)
    def _():
        score_tile(n_full - 1, False)

    score_tile(n_it - 1, True)

    def count(pred):
        def body(kt, acc):
            off = koff(kt)
            ind = jnp.where(pred(keys_scr[pl.ds(off, TK), :], off), 1.0, 0.0)
            return acc + jnp.sum(ind.reshape(TK // CR, CR, Q), axis=0)
        acc = lax.fori_loop(0, n_it, body, jnp.zeros((CR, Q), F32))
        return jnp.sum(acc, axis=0, keepdims=True)

    take_all = limit <= topk

    def thr_bit(b, t):
        cand = t + lax.shift_left(jnp.int32(1), 31 - b)
        return jnp.where(count(lambda kk, off: kk >= cand) >= topk, cand, t)

    thr = lax.fori_loop(0, 32, thr_bit, jnp.full((1, Q), INT_MIN, I32))
    thr = jnp.where(take_all, INT_MIN + 1, thr)
    n_ge = count(lambda kk, off: kk >= thr)

    @pl.when(jnp.max(n_ge) > topk)
    def _():
        need = topk - count(lambda kk, off: kk > thr)

        def idx_bit(b, j):
            cand = j + lax.shift_left(jnp.int32(1), idx_bits - 1 - b)
            n_before = count(lambda kk, off: (kk == thr) & (off + krow < cand))
            return jnp.where(n_before < need, cand, j)

        last = lax.fori_loop(0, idx_bits, idx_bit, jnp.zeros((1, Q), I32))

        def drop(kt, carry):
            off = koff(kt)
            kk = keys_scr[pl.ds(off, TK), :]
            keys_scr[pl.ds(off, TK), :] = jnp.where((kk == thr) & (off + krow > last), INT_MIN, kk)
            return carry

        lax.fori_loop(0, n_it, drop, 0)

    m_scr[...] = jnp.full(m_scr.shape, NEG_BIG, F32)
    acc_scr[...] = jnp.zeros_like(acc_scr)

    G = DSA_GROUP
    n_grp = (n_it + G - 1) // G

    def logits(kt, buf, thr_kt):
        off = koff(kt)
        sel = keys_scr[pl.ds(off, TK), :] >= thr_kt
        bsel = [jnp.clip(i - (kt * SUB + j), 0, 2) for j in range(SUB)]
        for h in range(DSA_HEADS):
            hs = slice(h * DSA_HD, (h + 1) * DSA_HD)
            s = _dot(kn_ref[pl.ds(off, TK), hs], qt_ref[hs, :])
            s = s + jnp.concatenate([bias_ref[h, bsel[j]] for j in range(SUB)], axis=0)
            s = jnp.where(sel, s, NEG_BIG)
            s_scr[buf, h] = s
            ml_scr[buf, h] = jnp.max(s, axis=0, keepdims=True)

    def values(kt, buf):
        off = koff(kt)
        for h in range(DSA_HEADS):
            m_old = m_scr[h]
            m_new = jnp.maximum(m_old, ml_scr[buf, h])
            p = jnp.exp2(s_scr[buf, h] - m_new).astype(BF16)
            v1 = vt_ref[h * DSA_VROWS:(h + 1) * DSA_VROWS, pl.ds(off, TK)]
            acc_scr[h] = jnp.exp2(m_old - m_new) * acc_scr[h] + _dot(v1, p)
            m_scr[h] = m_new

    def att_group(gi, carry):
        steps = []
        for e in range(G):
            kt = gi * G + e
            steps.append((jnp.minimum(kt, n_it - 1), jnp.where(kt < n_it, thr, jnp.int32(2 ** 31 - 1))))
        for e in range(G):
            logits(steps[e][0], e, steps[e][1])
        for e in range(G):
            values(steps[e][0], e)
        return carry

    lax.fori_loop(0, n_grp, att_group, 0)
    for h in range(DSA_HEADS):
        hs = slice(h * DSA_HD, (h + 1) * DSA_HD)
        acc = acc_scr[h]
        out = (acc[0:DSA_HD, :] / acc[DSA_HD:DSA_HD + 1, :]).T
        y_ref[:, hs] = (out * _silu(z_ref[:, hs].astype(F32))).astype(BF16)


def _dsa(p, qt, kn, vt, iqt, ik, iwt, bias_tiles, b, s):
    assert s % DSA_TK == 0
    nt = s // TILE
    t = b * s
    topk = min(TOPK_MAX, s // 4)
    idx_bits = max(1, (s - 1).bit_length())
    kernel = functools.partial(_dsa_kernel, topk=topk, idx_bits=idx_bits)
    once = pl.Buffered(1)
    qfeat = lambda rows: pl.BlockSpec((None, rows, TILE), lambda bi, ti: (bi, 0, ti))
    return pl.pallas_call(
        kernel,
        grid=(b, nt),
        in_specs=[qfeat(DSA_W), qfeat(IDX_W), qfeat(IDX_HEADS),
                  pl.BlockSpec((TILE, DSA_W), lambda bi, ti: (bi * nt + ti, P_DSA_Z // DSA_W)),
                  pl.BlockSpec((s, DSA_W), lambda bi, ti: (bi, 0), pipeline_mode=once),
                  pl.BlockSpec((None, DSA_HEADS * DSA_VROWS, s), lambda bi, ti: (bi, 0, 0), pipeline_mode=once),
                  pl.BlockSpec((s, IDX_DIM), lambda bi, ti: (bi, 0), pipeline_mode=once),
                  pl.BlockSpec((DSA_HEADS, 3, TILE, TILE), lambda bi, ti: (0, 0, 0, 0))],
        out_specs=pl.BlockSpec((TILE, DSA_W), lambda bi, ti: (bi * nt + ti, 0)),
        out_shape=jax.ShapeDtypeStruct((t, DSA_W), BF16),
        scratch_shapes=[pltpu.VMEM((s, TILE), I32),
                        pltpu.VMEM((DSA_HEADS, DSA_VROWS, TILE), F32),
                        pltpu.VMEM((DSA_HEADS, 1, TILE), F32),
                        pltpu.VMEM((IDX_HEADS // 2, IDX_DIM, 2 * TILE), BF16),
                        pltpu.VMEM((IDX_HEADS // 2, 1, 2 * TILE), F32),
                        pltpu.VMEM((DSA_GROUP, DSA_HEADS, DSA_TK, TILE), F32),
                        pltpu.VMEM((DSA_GROUP, DSA_HEADS, 1, TILE), F32)],
        compiler_params=_cparams(("arbitrary", "arbitrary"), 48),
        name="dsa",
    )(qt, iqt, iwt, p, kn, vt, ik, bias_tiles)


def _out_proj_kernel(yml_ref, ygla_ref, ydsa_ref, w_ref, x_ref, o_ref):
    y = _dot(yml_ref[...], w_ref[0:ML_W, :])
    y = y + _dot(ygla_ref[...], w_ref[ML_W:ML_W + GLA_W, :])
    y = y + _dot(ydsa_ref[...], w_ref[ML_W + GLA_W:D_MIX, :])
    o_ref[...] = x_ref[...] + y


def _out_proj(y_ml, y_gla, y_dsa, w_out, x2d):
    t = x2d.shape[0]
    tm = min(1024, t)
    row = lambda width: pl.BlockSpec((tm, width), lambda i: (i, 0))
    return pl.pallas_call(
        _out_proj_kernel,
        grid=(t // tm,),
        in_specs=[row(ML_W), row(GLA_W), row(DSA_W),
                  pl.BlockSpec((D_MIX, D_MODEL), lambda i: (0, 0)), row(D_MODEL)],
        out_specs=row(D_MODEL),
        out_shape=jax.ShapeDtypeStruct((t, D_MODEL), F32),
        compiler_params=_cparams(("arbitrary",), 40),
        name="out_proj",
    )(y_ml, y_gla, y_dsa, w_out, x2d)


_COLUMN_LAYOUT = (
    ('ml_q', ML_W), ('ml_k', ML_W), ('ml_v', ML_W), ('ml_o', ML_W), ('ml_z', ML_W),
    ('ml_i', ML_HEADS), ('ml_f', ML_HEADS),
    ('gla_q', GLA_WK), ('gla_k', GLA_WK), ('gla_v', GLA_W), ('gla_a', GLA_RANK), ('gla_r', GLA_W),
    ('dsa_q', DSA_W), ('dsa_k', DSA_W), ('dsa_v', DSA_W), ('dsa_z', DSA_W),
    ('idx_q', IDX_W), ('idx_k', IDX_DIM), ('idx_w', IDX_HEADS),
)


def _split_w_in(w_in):
    cols, off = {}, 0
    for name, width in _COLUMN_LAYOUT:
        cols[name] = w_in[:, off:off + width]
        off += width
    return cols


def _pack_w_in(w_in):
    c = _split_w_in(w_in)
    d = w_in.shape[0]
    wbig = jnp.concatenate([c['ml_q'], c['ml_k'], c['ml_v'], c['ml_o'], c['ml_z'], c['idx_q'],
                            c['gla_q'], c['gla_k'], c['gla_v'], c['gla_r'],
                            c['dsa_q'], c['dsa_k'], c['dsa_v'], c['dsa_z']], axis=1)
    assert wbig.shape[1] == P_COLS
    z = lambda n: jnp.zeros((d, n), w_in.dtype)
    wsmall = jnp.concatenate([c['idx_k'], c['ml_i'], z(4), c['ml_f'], z(4), c['gla_a'], c['idx_w'],
                              z(LANES - G_IDXW - IDX_HEADS)], axis=1)
    assert wsmall.shape[1] == LANES
    return wbig.astype(BF16), wsmall.astype(BF16)


def _rel_bucket(rel):
    half = REL_BUCKETS // 2
    max_exact = half // 2
    ret = jnp.where(rel > 0, half, 0)
    n = jnp.abs(rel)
    nf = jnp.maximum(n, 1).astype(F32)
    large = max_exact + (jnp.log(nf / max_exact) / math.log(REL_MAX_DIST / max_exact)
                         * (half - max_exact)).astype(I32)
    large = jnp.minimum(large, half - 1)
    return ret + jnp.where(n < max_exact, n, large)


def _bias_tiles(rel_bias):
    assert REL_MAX_DIST <= TILE
    kl = jnp.arange(TILE, dtype=I32)[:, None]
    ql = jnp.arange(TILE, dtype=I32)[None, :]
    far = jnp.full((TILE, TILE), -(TILE + 1), I32)
    rel = jnp.stack([kl - ql, kl - TILE - ql, far])
    tab = jnp.transpose(rel_bias.astype(F32)[_rel_bucket(rel)], (3, 0, 1, 2))
    return (tab - tab[:, 2:3]) * LOG2E


def _layer(x2d, b, s, norm_g, w_in, ml_conv_w, ml_conv_b, ml_b_i, ml_b_f, ml_norm_g,
           gla_w_a, gla_b_a, gla_norm_g, dsa_q_g, dsa_k_g, w_out, bias_tiles):
    wbig, wsmall = _pack_w_in(w_in)
    p, g = _in_proj(x2d, norm_g.reshape(1, D_MODEL), wbig, wsmall)

    gbias = jnp.zeros((1, LANES), F32)
    gbias = gbias.at[0, G_MLI:G_MLI + ML_HEADS].set(ml_b_i).at[0, G_MLF:G_MLF + ML_HEADS].set(ml_b_f)
    y_ml = _mlstm(p, g, gbias, ml_conv_w, ml_conv_b.reshape(1, -1), ml_norm_g.reshape(1, -1), b, s)

    wa_pad = jnp.zeros((LANES, GLA_WK), F32).at[G_GLAA:G_GLAA + GLA_RANK].set(gla_w_a).astype(BF16)
    y_gla = _gla(p, g, wa_pad, gla_b_a.reshape(1, -1), gla_norm_g.reshape(1, -1), b, s)

    qt, kn, vt, iqt, ik, iwt = _dsa_prep(p, g, dsa_q_g.reshape(1, -1), dsa_k_g.reshape(1, -1), b, s)
    y_dsa = _dsa(p, qt, kn, vt, iqt, ik, iwt, bias_tiles, b, s)

    return _out_proj(y_ml, y_gla, y_dsa, w_out.astype(BF16), x2d)


def kernel(x, norm_g, w_in, ml_conv_w, ml_conv_b, ml_b_i, ml_b_f, ml_norm_g, gla_w_a, gla_b_a, gla_norm_g,
           dsa_q_g, dsa_k_g, w_out, rel_bias):
    b, s, d = x.shape
    assert d == D_MODEL and s % TILE == 0
    depth = norm_g.shape[0]
    bias_tiles = _bias_tiles(rel_bias)
    x2d = x.reshape(b * s, d)
    for layer in range(depth):
        x2d = _layer(x2d, b, s, norm_g[layer], w_in[layer], ml_conv_w[layer], ml_conv_b[layer],
                     ml_b_i[layer], ml_b_f[layer], ml_norm_g[layer], gla_w_a[layer], gla_b_a[layer],
                     gla_norm_g[layer], dsa_q_g[layer], dsa_k_g[layer], w_out[layer], bias_tiles)
    return x2d.reshape(b, s, d)
```

```python
import functools
import math

import jax
import jax.numpy as jnp
from jax import lax
from jax.experimental import pallas as pl
from jax.experimental.pallas import tpu as pltpu

F32 = jnp.float32
BF16 = jnp.bfloat16
I32 = jnp.int32

D_MODEL = 1024
CHUNK = 64
EPS = 1e-6
ML_HEADS, ML_HD = 4, 256
ML_W = ML_HEADS * ML_HD
ML_CONV = 4
GLA_HEADS, GLA_DK, GLA_DV = 4, 64, 128
GLA_WK = GLA_HEADS * GLA_DK
GLA_W = GLA_HEADS * GLA_DV
GLA_RANK = 16
GLA_TAU = 16.0
DSA_HEADS, DSA_HD = 4, 128
DSA_W = DSA_HEADS * DSA_HD
IDX_HEADS, IDX_DIM = 8, 64
IDX_W = IDX_HEADS * IDX_DIM
TOPK_MAX = 256
REL_BUCKETS, REL_MAX_DIST = 32, 128
D_MIX = ML_W + GLA_W + DSA_W

LANES = 128
SUBLANES = 8

TILE = 128
DSA_TK = 512
DSA_SUB = DSA_TK // TILE
DSA_GROUP = 4
DSA_VROWS = DSA_HD + 16
LOG2E = 1.4426950408889634

P_ML = 0
P_IDXQ = 5 * ML_W
P_GLA_Q = P_IDXQ + IDX_W
P_GLA_K = P_GLA_Q + GLA_WK
P_GLA_V = P_GLA_K + GLA_WK
P_GLA_R = P_GLA_V + GLA_W
P_DSA_Q = P_GLA_R + GLA_W
P_DSA_K = P_DSA_Q + DSA_W
P_DSA_V = P_DSA_K + DSA_W
P_DSA_Z = P_DSA_V + DSA_W
P_COLS = P_DSA_Z + DSA_W
G_IDXK = 0
G_MLI = 64
G_MLF = 72
G_GLAA = 80
G_IDXW = 96

INT_MIN = -(2 ** 31)
NEG_BIG = -1e30


def _cparams(sem, vmem_mb):
    return pltpu.CompilerParams(dimension_semantics=sem, vmem_limit_bytes=vmem_mb << 20)


def _dot(a, b):
    return jnp.dot(a, b, preferred_element_type=F32)


def _dot_nt(a, b):
    return lax.dot_general(a, b, (((1,), (1,)), ((), ())), preferred_element_type=F32)


def _split3(x):
    hi = x.astype(BF16)
    r1 = x - hi.astype(F32)
    mid = r1.astype(BF16)
    lo = (r1 - mid.astype(F32)).astype(BF16)
    return hi, mid, lo


def _dot_f32_lhs(x, m01):
    hi, mid, lo = _split3(x)
    return _dot(hi, m01) + _dot(mid, m01) + _dot(lo, m01)


def _dot_f32_rhs(m01, x):
    hi, mid, lo = _split3(x)
    return _dot(m01, hi) + _dot(m01, mid) + _dot(m01, lo)


def _log_sigmoid(x):
    return jnp.minimum(x, 0.0) - jnp.log(1.0 + jnp.exp(-jnp.abs(x)))


def _sigmoid(x):
    return 1.0 / (1.0 + jnp.exp(-x))


def _silu(x):
    return x * _sigmoid(x)


def _in_proj_kernel(x_ref, g_ref, wb_ref, ws_ref, p_ref, gate_ref, xn_ref):
    @pl.when(pl.program_id(1) == 0)
    def _():
        x = x_ref[...]
        ms = jnp.mean(x * x, axis=-1, keepdims=True)
        xn = (x * lax.rsqrt(ms + EPS) * g_ref[...]).astype(BF16)
        xn_ref[...] = xn
        gate_ref[...] = _dot(xn, ws_ref[...])

    p_ref[...] = _dot(xn_ref[...], wb_ref[...]).astype(BF16)


def _in_proj(x2d, g, wbig, wsmall):
    t = x2d.shape[0]
    tm = min(1024, t)
    tn = P_COLS // 4
    return pl.pallas_call(
        _in_proj_kernel,
        grid=(t // tm, P_COLS // tn),
        in_specs=[
            pl.BlockSpec((tm, D_MODEL), lambda i, j: (i, 0)),
            pl.BlockSpec((1, D_MODEL), lambda i, j: (0, 0)),
            pl.BlockSpec((D_MODEL, tn), lambda i, j: (0, j)),
            pl.BlockSpec((D_MODEL, LANES), lambda i, j: (0, 0)),
        ],
        out_specs=[
            pl.BlockSpec((tm, tn), lambda i, j: (i, j)),
            pl.BlockSpec((tm, LANES), lambda i, j: (i, 0)),
        ],
        out_shape=[jax.ShapeDtypeStruct((t, P_COLS), BF16), jax.ShapeDtypeStruct((t, LANES), F32)],
        scratch_shapes=[pltpu.VMEM((tm, D_MODEL), BF16)],
        compiler_params=_cparams(("arbitrary", "arbitrary"), 56),
        name="in_proj",
    )(x2d, g, wbig, wsmall)


def _mlstm_kernel(q_ref, k_ref, v_ref, o_ref, z_ref, g_ref, gb_ref, cw_ref, cb_ref, ng_ref,
                  y_ref, c_scr, n_scr, m_scr, xbuf):
    L = TILE

    @pl.when(pl.program_id(1) == 0)
    def _():
        c_scr[...] = jnp.zeros_like(c_scr)
        n_scr[...] = jnp.zeros_like(n_scr)
        m_scr[...] = jnp.zeros_like(m_scr)
        xbuf[0:SUBLANES, :] = jnp.zeros((SUBLANES, 2 * ML_W), F32)

    xbuf[SUBLANES:SUBLANES + L, 0:ML_W] = q_ref[...].astype(F32)
    xbuf[SUBLANES:SUBLANES + L, ML_W:2 * ML_W] = k_ref[...].astype(F32)
    conv = jnp.broadcast_to(cb_ref[...], (L, 2 * ML_W))
    for j in range(ML_CONV):
        off = SUBLANES - (ML_CONV - 1) + j
        conv = conv + cw_ref[j:j + 1, :] * xbuf[off:off + L, :]
    xbuf[0:SUBLANES, :] = xbuf[L:L + SUBLANES, :]
    qk = _silu(conv)

    gate = g_ref[...] + gb_ref[...]
    lf_cols = _log_sigmoid(gate)
    gate_t = gate.T
    li_rows = gate_t[G_MLI:G_MLI + SUBLANES, :]
    lf_rows = _log_sigmoid(gate_t[G_MLF:G_MLF + SUBLANES, :])
    r_io = lax.broadcasted_iota(I32, (L, L), 0)
    c_io = lax.broadcasted_iota(I32, (L, L), 1)
    causal = c_io <= r_io
    tri_u = (r_io <= c_io).astype(BF16)
    tri_l = causal.astype(BF16)
    bc_rows = _dot_f32_lhs(lf_rows, tri_u)
    bc_cols = _dot_f32_rhs(tri_l, lf_cols)

    for h in range(ML_HEADS):
        hs = slice(h * ML_HD, (h + 1) * ML_HD)
        qh = qk[:, hs]
        kh = qk[:, ML_W + h * ML_HD:ML_W + (h + 1) * ML_HD] * (ML_HD ** -0.5)
        vh = v_ref[:, hs]
        qb = qh.astype(BF16)
        kb = kh.astype(BF16)
        bc_col = bc_cols[:, G_MLF + h:G_MLF + h + 1]
        bc_row = bc_rows[h:h + 1, :]
        li_row = li_rows[h:h + 1, :]
        m_prev = m_scr[h, :, 0:1]

        dmat = jnp.where(causal, bc_col - (bc_row - li_row), -jnp.inf)
        inter = bc_col + m_prev
        m_row = jnp.maximum(inter, jnp.max(dmat, axis=1, keepdims=True))
        w_intra = jnp.exp(dmat - m_row)
        w_inter = jnp.exp(inter - m_row)
        s_qk = _dot_nt(qb, kb) * w_intra
        num = _dot(s_qk.astype(BF16), vh) + w_inter * _dot(qb, c_scr[h].astype(BF16))
        qn = jnp.sum(qh * n_scr[h], axis=1, keepdims=True)
        den = jnp.sum(s_qk, axis=1, keepdims=True) + w_inter * qn
        hh = num / jnp.maximum(jnp.abs(den), jnp.exp(-m_row))

        g_tot = bc_row[:, L - 1:L]
        a_row = g_tot - bc_row + li_row
        m_new = jnp.maximum(g_tot + m_prev, jnp.max(a_row, axis=1, keepdims=True))
        decay = jnp.exp(g_tot + m_prev - m_new)
        wa_row = jnp.exp(a_row - m_new)
        ktw = (kh.T * wa_row).astype(BF16)
        c_scr[h] = decay * c_scr[h] + _dot(ktw, vh)
        wa8 = jnp.broadcast_to(wa_row, (SUBLANES, L)).astype(BF16)
        n_scr[h] = decay * n_scr[h] + _dot(wa8, kb)[0:1, :]
        m_scr[h] = jnp.broadcast_to(m_new, (1, LANES))

        hm = hh * _sigmoid(o_ref[:, hs].astype(F32))
        hm = hm * lax.rsqrt(jnp.mean(hm * hm, axis=1, keepdims=True) + EPS) * ng_ref[:, hs]
        y_ref[:, hs] = (hm * _silu(z_ref[:, hs].astype(F32))).astype(BF16)


def _mlstm(p, g, gbias, conv_w, conv_b, norm_g, b, s):
    nt = s // TILE
    t = b * s

    def pcol(c):
        return pl.BlockSpec((TILE, ML_W), lambda bi, ti, c=c: (bi * nt + ti, c))

    full = lambda shape: pl.BlockSpec(shape, lambda bi, ti: (0,) * len(shape))
    return pl.pallas_call(
        _mlstm_kernel,
        grid=(b, nt),
        in_specs=[pcol(0), pcol(1), pcol(2), pcol(3), pcol(4),
                  pl.BlockSpec((TILE, LANES), lambda bi, ti: (bi * nt + ti, 0)),
                  full((1, LANES)), full((ML_CONV, 2 * ML_W)), full((1, 2 * ML_W)), full((1, ML_W))],
        out_specs=pl.BlockSpec((TILE, ML_W), lambda bi, ti: (bi * nt + ti, 0)),
        out_shape=jax.ShapeDtypeStruct((t, ML_W), BF16),
        scratch_shapes=[pltpu.VMEM((ML_HEADS, ML_HD, ML_HD), F32),
                        pltpu.VMEM((ML_HEADS, 1, ML_HD), F32),
                        pltpu.VMEM((ML_HEADS, 1, LANES), F32),
                        pltpu.VMEM((TILE + SUBLANES, 2 * ML_W), F32)],
        compiler_params=_cparams(("arbitrary", "arbitrary"), 40),
        name="mlstm",
    )(p, p, p, p, p, g, gbias, conv_w, conv_b, norm_g)


def _gla_kernel(q_ref, k_ref, v_ref, r_ref, g_ref, wa_ref, ba_ref, ng_ref, y_ref, st_scr):
    L = TILE
    H = GLA_HEADS

    @pl.when(pl.program_id(1) == 0)
    def _():
        st_scr[...] = jnp.zeros_like(st_scr)

    la = _log_sigmoid(_dot(g_ref[...].astype(BF16), wa_ref[...]) + ba_ref[...]) * (1.0 / GLA_TAU)
    r_io = lax.broadcasted_iota(I32, (L, L), 0)
    c_io = lax.broadcasted_iota(I32, (L, L), 1)
    tri_l = (c_io <= r_io).astype(BF16)
    bcum = _dot_f32_rhs(tri_l, la)
    btot = bcum[L - 1:L, :]
    bmid = bcum[L // 2 - 1:L // 2, :]

    q = q_ref[...].astype(F32) * (GLA_DK ** -0.5)
    k = k_ref[...].astype(F32)
    v = v_ref[...]
    q_in = (q * jnp.exp(bcum - bmid)).astype(BF16)
    k_in = k * jnp.exp(bmid - bcum)
    q_st = (q * jnp.exp(bcum)).astype(BF16)
    k_st = (k * jnp.exp(btot - bcum)).astype(BF16)

    kt = k_in.T
    kt4 = jnp.concatenate([kt] * H, axis=1)
    rr = lax.broadcasted_iota(I32, (GLA_WK, H * L), 0)
    cc = lax.broadcasted_iota(I32, (GLA_WK, H * L), 1)
    k_bd = jnp.where(rr // GLA_DK == cc // L, kt4, 0.0).astype(BF16)
    att = _dot(q_in, k_bd)
    ar = lax.broadcasted_iota(I32, (L, H * L), 0)
    ac = lax.broadcasted_iota(I32, (L, H * L), 1)
    att = jnp.where(ac % L <= ar, att, 0.0).astype(BF16)
    v4 = jnp.concatenate([v] * H, axis=0)
    vr = lax.broadcasted_iota(I32, (H * L, GLA_W), 0)
    vc = lax.broadcasted_iota(I32, (H * L, GLA_W), 1)
    v_bd = jnp.where(vr // L == vc // GLA_DV, v4, jnp.zeros_like(v4))
    o = _dot(att, v_bd) + _dot_nt(q_st, st_scr[...].astype(BF16))

    sr = lax.broadcasted_iota(I32, (GLA_W, GLA_WK), 0)
    sc = lax.broadcasted_iota(I32, (GLA_W, GLA_WK), 1)
    upd = _dot(v.astype(F32).T.astype(BF16), k_st)
    st_scr[...] = st_scr[...] * jnp.exp(btot) + jnp.where(sr // GLA_DV == sc // GLA_DK, upd, 0.0)

    for h in range(H):
        hs = slice(h * GLA_DV, (h + 1) * GLA_DV)
        oh = o[:, hs]
        oh = oh * lax.rsqrt(jnp.mean(oh * oh, axis=1, keepdims=True) + EPS) * ng_ref[:, hs]
        y_ref[:, hs] = (oh * _silu(r_ref[:, hs].astype(F32))).astype(BF16)


def _gla(p, g, wa_pad, b_a, norm_g, b, s):
    nt = s // TILE
    t = b * s

    def pcol(width, off):
        assert off % width == 0
        return pl.BlockSpec((TILE, width), lambda bi, ti: (bi * nt + ti, off // width))

    full = lambda shape: pl.BlockSpec(shape, lambda bi, ti: (0,) * len(shape))
    return pl.pallas_call(
        _gla_kernel,
        grid=(b, nt),
        in_specs=[pcol(GLA_WK, P_GLA_Q), pcol(GLA_WK, P_GLA_K), pcol(GLA_W, P_GLA_V), pcol(GLA_W, P_GLA_R),
                  pl.BlockSpec((TILE, LANES), lambda bi, ti: (bi * nt + ti, 0)),
                  full((LANES, GLA_WK)), full((1, GLA_WK)), full((1, GLA_W))],
        out_specs=pl.BlockSpec((TILE, GLA_W), lambda bi, ti: (bi * nt + ti, 0)),
        out_shape=jax.ShapeDtypeStruct((t, GLA_W), BF16),
        scratch_shapes=[pltpu.VMEM((GLA_W, GLA_WK), F32)],
        compiler_params=_cparams(("arbitrary", "arbitrary"), 32),
        name="gla",
    )(p, p, p, p, g, wa_pad, b_a, norm_g)


def _dsa_prep_kernel(q_ref, k_ref, v_ref, iq_ref, g_ref, qg_ref, kg_ref,
                     qt_ref, kn_ref, vt_ref, iqt_ref, ik_ref, iwt_ref):
    qn = []
    for h in range(DSA_HEADS):
        hs = slice(h * DSA_HD, (h + 1) * DSA_HD)
        q = q_ref[:, hs].astype(F32)
        k = k_ref[:, hs].astype(F32)
        qh = q * lax.rsqrt(jnp.mean(q * q, axis=1, keepdims=True) + EPS) * qg_ref[...]
        kh = k * lax.rsqrt(jnp.mean(k * k, axis=1, keepdims=True) + EPS) * kg_ref[...]
        qn.append(qh * (DSA_HD ** -0.5 * LOG2E))
        kn_ref[:, hs] = kh.astype(BF16)
        r0 = h * DSA_VROWS
        vt_ref[r0:r0 + DSA_HD, :] = v_ref[:, hs].astype(F32).T.astype(BF16)
        vt_ref[r0 + DSA_HD:r0 + DSA_VROWS, :] = jnp.ones((DSA_VROWS - DSA_HD, vt_ref.shape[1]), BF16)
    qt_ref[...] = jnp.concatenate(qn, axis=1).T.astype(BF16)
    iqt_ref[...] = iq_ref[...].astype(F32).T.astype(BF16)
    g = g_ref[...]
    ik_ref[...] = g[:, G_IDXK:G_IDXK + IDX_DIM].astype(BF16)
    iw_scale = (IDX_HEADS ** -0.5) * (IDX_DIM ** -0.5)
    iwt_ref[...] = g.T[G_IDXW:G_IDXW + IDX_HEADS, :] * iw_scale


def _dsa_prep(p, g, q_g, k_g, b, s):
    t = b * s
    tm = min(512, s)
    nt = s // tm
    full = lambda shape: pl.BlockSpec(shape, lambda bi, ti: (0,) * len(shape))
    tok = lambda width, cb: pl.BlockSpec((tm, width), lambda bi, ti: (bi * nt + ti, cb))
    feat = lambda rows: pl.BlockSpec((None, rows, tm), lambda bi, ti: (bi, 0, ti))
    return pl.pallas_call(
        _dsa_prep_kernel,
        grid=(b, nt),
        in_specs=[tok(DSA_W, P_DSA_Q // DSA_W), tok(DSA_W, P_DSA_K // DSA_W), tok(DSA_W, P_DSA_V // DSA_W),
                  tok(IDX_W, P_IDXQ // IDX_W), tok(LANES, 0), full((1, DSA_HD)), full((1, DSA_HD))],
        out_specs=[feat(DSA_W), tok(DSA_W, 0), feat(DSA_HEADS * DSA_VROWS), feat(IDX_W), tok(IDX_DIM, 0),
                   feat(IDX_HEADS)],
        out_shape=[jax.ShapeDtypeStruct((b, DSA_W, s), BF16),
                   jax.ShapeDtypeStruct((t, DSA_W), BF16),
                   jax.ShapeDtypeStruct((b, DSA_HEADS * DSA_VROWS, s), BF16),
                   jax.ShapeDtypeStruct((b, IDX_W, s), BF16),
                   jax.ShapeDtypeStruct((t, IDX_DIM), BF16),
                   jax.ShapeDtypeStruct((b, IDX_HEADS, s), F32)],
        compiler_params=_cparams(("arbitrary", "arbitrary"), 40),
        name="dsa_prep",
    )(p, p, p, p, g, q_g, k_g)


def _dsa_kernel(qt_ref, iqt_ref, iwt_ref, z_ref, kn_ref, vt_ref, ik_ref, bias_ref, y_ref,
                keys_scr, acc_scr, m_scr, iq2_scr, iw2_scr, s_scr, ml_scr, *, topk, idx_bits):
    Q, TK, SUB = TILE, DSA_TK, DSA_SUB
    i = pl.program_id(1)
    n_it = (i + SUB) // SUB
    qpos = i * Q + lax.broadcasted_iota(I32, (1, Q), 1)
    limit = (qpos // CHUNK + 1) * CHUNK
    krow = lax.broadcasted_iota(I32, (TK, Q), 0)

    def koff(kt):
        return pl.multiple_of(kt * TK, TK)

    for j in range(IDX_HEADS // 2):
        for e in range(2):
            h = 2 * j + e
            iq2_scr[j, :, e * Q:(e + 1) * Q] = iqt_ref[h * IDX_DIM:(h + 1) * IDX_DIM, :]
            iw2_scr[j, :, e * Q:(e + 1) * Q] = iwt_ref[h:h + 1, :]

    CR = 8 * SUBLANES

    def to_key(x):
        bits = lax.bitcast_convert_type(x, I32)
        return bits ^ ((bits >> 31) & 0x7FFFFFFF)

    def score_tile(kt, last):
        off = koff(kt)
        ikt = ik_ref[pl.ds(off, TK), :]
        sc = jnp.zeros((TK, Q), F32)
        for j in range(IDX_HEADS // 2):
            w = jnp.maximum(_dot(ikt, iq2_scr[j]), 0.0) * iw2_scr[j]
            sc = sc + (w[:, 0:Q] + w[:, Q:2 * Q])
        keys = to_key(sc)
        if last:
            keys = jnp.where(off + krow >= limit, INT_MIN, keys)
        keys_scr[pl.ds(off, TK), :] = keys

    def score_pair(jp, carry):
        score_tile(2 * jp, False)
        score_tile(2 * jp + 1, False)
        return carry

    n_full = n_it - 1
    lax.fori_loop(0, n_full // 2, score_pair, 0)

    @pl.when(n_full % 2 == 1)
    def _():
        score_tile(n_full - 1, False)

    score_tile(n_it - 1, True)

    def count(pred):
        def one(kt, acc):
            off = koff(kt)
            ind = jnp.where(pred(keys_scr[pl.ds(off, TK), :], off), 1.0, 0.0)
            return acc + jnp.sum(ind.reshape(TK // CR, CR, Q), axis=0)

        def two(jp, acc):
            return one(2 * jp + 1, one(2 * jp, acc))

        acc = lax.fori_loop(0, n_it // 2, two, jnp.zeros((CR, Q), F32))
        acc = lax.cond(n_it % 2 == 1, lambda a: one(n_it - 1, a), lambda a: a, acc)
        return jnp.sum(acc, axis=0, keepdims=True)

    take_all = limit <= topk

    def thr_bit(b, t):
        cand = t + lax.shift_left(jnp.int32(1), 31 - b)
        return jnp.where(count(lambda kk, off: kk >= cand) >= topk, cand, t)

    thr = lax.fori_loop(0, 32, thr_bit, jnp.full((1, Q), INT_MIN, I32))
    thr = jnp.where(take_all, INT_MIN + 1, thr)
    n_ge = count(lambda kk, off: kk >= thr)

    @pl.when(jnp.max(n_ge) > topk)
    def _():
        need = topk - count(lambda kk, off: kk > thr)

        def idx_bit(b, j):
            cand = j + lax.shift_left(jnp.int32(1), idx_bits - 1 - b)
            n_before = count(lambda kk, off: (kk == thr) & (off + krow < cand))
            return jnp.where(n_before < need, cand, j)

        last = lax.fori_loop(0, idx_bits, idx_bit, jnp.zeros((1, Q), I32))

        def drop(kt, carry):
            off = koff(kt)
            kk = keys_scr[pl.ds(off, TK), :]
            keys_scr[pl.ds(off, TK), :] = jnp.where((kk == thr) & (off + krow > last), INT_MIN, kk)
            return carry

        lax.fori_loop(0, n_it, drop, 0)

    m_scr[...] = jnp.full(m_scr.shape, NEG_BIG, F32)
    acc_scr[...] = jnp.zeros_like(acc_scr)

    def logits(kt, buf, thr_kt):
        off = koff(kt)
        sel = keys_scr[pl.ds(off, TK), :] >= thr_kt
        bsel = [jnp.clip(i - (kt * SUB + j), 0, 2) for j in range(SUB)]
        for h in range(DSA_HEADS):
            hs = slice(h * DSA_HD, (h + 1) * DSA_HD)
            s = _dot(kn_ref[pl.ds(off, TK), hs], qt_ref[hs, :])
            s = s + jnp.concatenate([bias_ref[h, bsel[j]] for j in range(SUB)], axis=0)
            s = jnp.where(sel, s, NEG_BIG)
            s_scr[buf, h] = s
            ml_scr[buf, h] = jnp.max(s, axis=0, keepdims=True)

    def values(kt, buf):
        off = koff(kt)
        for h in range(DSA_HEADS):
            m_old = m_scr[h]
            m_new = jnp.maximum(m_old, ml_scr[buf, h])
            p = jnp.exp2(s_scr[buf, h] - m_new).astype(BF16)
            v1 = vt_ref[h * DSA_VROWS:(h + 1) * DSA_VROWS, pl.ds(off, TK)]
            acc_scr[h] = jnp.exp2(m_old - m_new) * acc_scr[h] + _dot(v1, p)
            m_scr[h] = m_new

    def att_group(first, size):
        steps = []
        for e in range(size):
            kt = first + e
            steps.append((jnp.minimum(kt, n_it - 1), jnp.where(kt < n_it, thr, jnp.int32(2 ** 31 - 1))))
        for e in range(size):
            logits(steps[e][0], e, steps[e][1])
        for e in range(size):
            values(steps[e][0], e)

    def att_big(gi, carry):
        att_group(gi * DSA_GROUP, DSA_GROUP)
        return carry

    def att_small(gi, carry):
        att_group(n_big * DSA_GROUP + 2 * gi, 2)
        return carry

    n_big = n_it // DSA_GROUP
    lax.fori_loop(0, n_big, att_big, 0)
    lax.fori_loop(0, (n_it - n_big * DSA_GROUP + 1) // 2, att_small, 0)
    for h in range(DSA_HEADS):
        hs = slice(h * DSA_HD, (h + 1) * DSA_HD)
        acc = acc_scr[h]
        out = (acc[0:DSA_HD, :] / acc[DSA_HD:DSA_HD + 1, :]).T
        y_ref[:, hs] = (out * _silu(z_ref[:, hs].astype(F32))).astype(BF16)


def _dsa(p, qt, kn, vt, iqt, ik, iwt, bias_tiles, b, s):
    assert s % DSA_TK == 0
    nt = s // TILE
    t = b * s
    topk = min(TOPK_MAX, s // 4)
    idx_bits = max(1, (s - 1).bit_length())
    kernel = functools.partial(_dsa_kernel, topk=topk, idx_bits=idx_bits)
    once = pl.Buffered(1)
    qfeat = lambda rows: pl.BlockSpec((None, rows, TILE), lambda bi, ti: (bi, 0, ti))
    return pl.pallas_call(
        kernel,
        grid=(b, nt),
        in_specs=[qfeat(DSA_W), qfeat(IDX_W), qfeat(IDX_HEADS),
                  pl.BlockSpec((TILE, DSA_W), lambda bi, ti: (bi * nt + ti, P_DSA_Z // DSA_W)),
                  pl.BlockSpec((s, DSA_W), lambda bi, ti: (bi, 0), pipeline_mode=once),
                  pl.BlockSpec((None, DSA_HEADS * DSA_VROWS, s), lambda bi, ti: (bi, 0, 0), pipeline_mode=once),
                  pl.BlockSpec((s, IDX_DIM), lambda bi, ti: (bi, 0), pipeline_mode=once),
                  pl.BlockSpec((DSA_HEADS, 3, TILE, TILE), lambda bi, ti: (0, 0, 0, 0))],
        out_specs=pl.BlockSpec((TILE, DSA_W), lambda bi, ti: (bi * nt + ti, 0)),
        out_shape=jax.ShapeDtypeStruct((t, DSA_W), BF16),
        scratch_shapes=[pltpu.VMEM((s, TILE), I32),
                        pltpu.VMEM((DSA_HEADS, DSA_VROWS, TILE), F32),
                        pltpu.VMEM((DSA_HEADS, 1, TILE), F32),
                        pltpu.VMEM((IDX_HEADS // 2, IDX_DIM, 2 * TILE), BF16),
                        pltpu.VMEM((IDX_HEADS // 2, 1, 2 * TILE), F32),
                        pltpu.VMEM((DSA_GROUP, DSA_HEADS, DSA_TK, TILE), F32),
                        pltpu.VMEM((DSA_GROUP, DSA_HEADS, 1, TILE), F32)],
        compiler_params=_cparams(("arbitrary", "arbitrary"), 48),
        name="dsa",
    )(qt, iqt, iwt, p, kn, vt, ik, bias_tiles)


def _out_proj_kernel(yml_ref, ygla_ref, ydsa_ref, w_ref, x_ref, o_ref):
    y = _dot(yml_ref[...], w_ref[0:ML_W, :])
    y = y + _dot(ygla_ref[...], w_ref[ML_W:ML_W + GLA_W, :])
    y = y + _dot(ydsa_ref[...], w_ref[ML_W + GLA_W:D_MIX, :])
    o_ref[...] = x_ref[...] + y


def _out_proj(y_ml, y_gla, y_dsa, w_out, x2d):
    t = x2d.shape[0]
    tm = min(1024, t)
    row = lambda width: pl.BlockSpec((tm, width), lambda i: (i, 0))
    return pl.pallas_call(
        _out_proj_kernel,
        grid=(t // tm,),
        in_specs=[row(ML_W), row(GLA_W), row(DSA_W),
                  pl.BlockSpec((D_MIX, D_MODEL), lambda i: (0, 0)), row(D_MODEL)],
        out_specs=row(D_MODEL),
        out_shape=jax.ShapeDtypeStruct((t, D_MODEL), F32),
        compiler_params=_cparams(("arbitrary",), 40),
        name="out_proj",
    )(y_ml, y_gla, y_dsa, w_out, x2d)


_COLUMN_LAYOUT = (
    ('ml_q', ML_W), ('ml_k', ML_W), ('ml_v', ML_W), ('ml_o', ML_W), ('ml_z', ML_W),
    ('ml_i', ML_HEADS), ('ml_f', ML_HEADS),
    ('gla_q', GLA_WK), ('gla_k', GLA_WK), ('gla_v', GLA_W), ('gla_a', GLA_RANK), ('gla_r', GLA_W),
    ('dsa_q', DSA_W), ('dsa_k', DSA_W), ('dsa_v', DSA_W), ('dsa_z', DSA_W),
    ('idx_q', IDX_W), ('idx_k', IDX_DIM), ('idx_w', IDX_HEADS),
)


def _split_w_in(w_in):
    cols, off = {}, 0
    for name, width in _COLUMN_LAYOUT:
        cols[name] = w_in[:, off:off + width]
        off += width
    return cols


def _pack_w_in(w_in):
    c = _split_w_in(w_in)
    d = w_in.shape[0]
    wbig = jnp.concatenate([c['ml_q'], c['ml_k'], c['ml_v'], c['ml_o'], c['ml_z'], c['idx_q'],
                            c['gla_q'], c['gla_k'], c['gla_v'], c['gla_r'],
                            c['dsa_q'], c['dsa_k'], c['dsa_v'], c['dsa_z']], axis=1)
    assert wbig.shape[1] == P_COLS
    z = lambda n: jnp.zeros((d, n), w_in.dtype)
    wsmall = jnp.concatenate([c['idx_k'], c['ml_i'], z(4), c['ml_f'], z(4), c['gla_a'], c['idx_w'],
                              z(LANES - G_IDXW - IDX_HEADS)], axis=1)
    assert wsmall.shape[1] == LANES
    return wbig.astype(BF16), wsmall.astype(BF16)


def _rel_bucket(rel):
    half = REL_BUCKETS // 2
    max_exact = half // 2
    ret = jnp.where(rel > 0, half, 0)
    n = jnp.abs(rel)
    nf = jnp.maximum(n, 1).astype(F32)
    large = max_exact + (jnp.log(nf / max_exact) / math.log(REL_MAX_DIST / max_exact)
                         * (half - max_exact)).astype(I32)
    large = jnp.minimum(large, half - 1)
    return ret + jnp.where(n < max_exact, n, large)


def _bias_tiles(rel_bias):
    assert REL_MAX_DIST <= TILE
    kl = jnp.arange(TILE, dtype=I32)[:, None]
    ql = jnp.arange(TILE, dtype=I32)[None, :]
    far = jnp.full((TILE, TILE), -(TILE + 1), I32)
    rel = jnp.stack([kl - ql, kl - TILE - ql, far])
    tab = jnp.transpose(rel_bias.astype(F32)[_rel_bucket(rel)], (3, 0, 1, 2))
    return (tab - tab[:, 2:3]) * LOG2E


def _layer(x2d, b, s, norm_g, w_in, ml_conv_w, ml_conv_b, ml_b_i, ml_b_f, ml_norm_g,
           gla_w_a, gla_b_a, gla_norm_g, dsa_q_g, dsa_k_g, w_out, bias_tiles):
    wbig, wsmall = _pack_w_in(w_in)
    p, g = _in_proj(x2d, norm_g.reshape(1, D_MODEL), wbig, wsmall)

    gbias = jnp.zeros((1, LANES), F32)
    gbias = gbias.at[0, G_MLI:G_MLI + ML_HEADS].set(ml_b_i).at[0, G_MLF:G_MLF + ML_HEADS].set(ml_b_f)
    y_ml = _mlstm(p, g, gbias, ml_conv_w, ml_conv_b.reshape(1, -1), ml_norm_g.reshape(1, -1), b, s)

    wa_pad = jnp.zeros((LANES, GLA_WK), F32).at[G_GLAA:G_GLAA + GLA_RANK].set(gla_w_a).astype(BF16)
    y_gla = _gla(p, g, wa_pad, gla_b_a.reshape(1, -1), gla_norm_g.reshape(1, -1), b, s)

    qt, kn, vt, iqt, ik, iwt = _dsa_prep(p, g, dsa_q_g.reshape(1, -1), dsa_k_g.reshape(1, -1), b, s)
    y_dsa = _dsa(p, qt, kn, vt, iqt, ik, iwt, bias_tiles, b, s)

    return _out_proj(y_ml, y_gla, y_dsa, w_out.astype(BF16), x2d)


def kernel(x, norm_g, w_in, ml_conv_w, ml_conv_b, ml_b_i, ml_b_f, ml_norm_g, gla_w_a, gla_b_a, gla_norm_g,
           dsa_q_g, dsa_k_g, w_out, rel_bias):
    b, s, d = x.shape
    assert d == D_MODEL and s % TILE == 0
    depth = norm_g.shape[0]
    bias_tiles = _bias_tiles(rel_bias)
    x2d = x.reshape(b * s, d)
    for layer in range(depth):
        x2d = _layer(x2d, b, s, norm_g[layer], w_in[layer], ml_conv_w[layer], ml_conv_b[layer],
                     ml_b_i[layer], ml_b_f[layer], ml_norm_g[layer], gla_w_a[layer], gla_b_a[layer],
                     gla_norm_g[layer], dsa_q_g[layer], dsa_k_g[layer], w_out[layer], bias_tiles)
    return x2d.reshape(b, s, d)
```

```python
import functools
import math

import jax
import jax.numpy as jnp
from jax import lax
from jax.experimental import pallas as pl
from jax.experimental.pallas import tpu as pltpu

F32 = jnp.float32
BF16 = jnp.bfloat16
I32 = jnp.int32

D_MODEL = 1024
CHUNK = 64
EPS = 1e-6
ML_HEADS, ML_HD = 4, 256
ML_W = ML_HEADS * ML_HD
ML_CONV = 4
GLA_HEADS, GLA_DK, GLA_DV = 4, 64, 128
GLA_WK = GLA_HEADS * GLA_DK
GLA_W = GLA_HEADS * GLA_DV
GLA_RANK = 16
GLA_TAU = 16.0
DSA_HEADS, DSA_HD = 4, 128
DSA_W = DSA_HEADS * DSA_HD
IDX_HEADS, IDX_DIM = 8, 64
IDX_W = IDX_HEADS * IDX_DIM
TOPK_MAX = 256
REL_BUCKETS, REL_MAX_DIST = 32, 128
D_MIX = ML_W + GLA_W + DSA_W

LANES = 128
SUBLANES = 8

TILE = 128
DSA_TK = 512
DSA_SUB = DSA_TK // TILE
DSA_GROUP = 4
DSA_VROWS = DSA_HD + 16
LOG2E = 1.4426950408889634

P_ML = 0
P_IDXQ = 5 * ML_W
P_GLA_Q = P_IDXQ + IDX_W
P_GLA_K = P_GLA_Q + GLA_WK
P_GLA_V = P_GLA_K + GLA_WK
P_GLA_R = P_GLA_V + GLA_W
P_DSA_Q = P_GLA_R + GLA_W
P_DSA_K = P_DSA_Q + DSA_W
P_DSA_V = P_DSA_K + DSA_W
P_DSA_Z = P_DSA_V + DSA_W
P_COLS = P_DSA_Z + DSA_W
G_IDXK = 0
G_MLI = 64
G_MLF = 72
G_GLAA = 80
G_IDXW = 96

INT_MIN = -(2 ** 31)
NEG_BIG = -1e30


def _cparams(sem, vmem_mb):
    return pltpu.CompilerParams(dimension_semantics=sem, vmem_limit_bytes=vmem_mb << 20)


def _dot(a, b):
    return jnp.dot(a, b, preferred_element_type=F32)


def _dot_nt(a, b):
    return lax.dot_general(a, b, (((1,), (1,)), ((), ())), preferred_element_type=F32)


def _split3(x):
    hi = x.astype(BF16)
    r1 = x - hi.astype(F32)
    mid = r1.astype(BF16)
    lo = (r1 - mid.astype(F32)).astype(BF16)
    return hi, mid, lo


def _dot_f32_lhs(x, m01):
    hi, mid, lo = _split3(x)
    return _dot(hi, m01) + _dot(mid, m01) + _dot(lo, m01)


def _dot_f32_rhs(m01, x):
    hi, mid, lo = _split3(x)
    return _dot(m01, hi) + _dot(m01, mid) + _dot(m01, lo)


def _log_sigmoid(x):
    return jnp.minimum(x, 0.0) - jnp.log(1.0 + jnp.exp(-jnp.abs(x)))


def _sigmoid(x):
    return 1.0 / (1.0 + jnp.exp(-x))


def _silu(x):
    return x * _sigmoid(x)


def _in_proj_kernel(x_ref, g_ref, wb_ref, ws_ref, p_ref, gate_ref, xn_ref):
    @pl.when(pl.program_id(1) == 0)
    def _():
        x = x_ref[...]
        ms = jnp.mean(x * x, axis=-1, keepdims=True)
        xn = (x * lax.rsqrt(ms + EPS) * g_ref[...]).astype(BF16)
        xn_ref[...] = xn
        gate_ref[...] = _dot(xn, ws_ref[...])

    p_ref[...] = _dot(xn_ref[...], wb_ref[...]).astype(BF16)


def _in_proj(x2d, g, wbig, wsmall):
    t = x2d.shape[0]
    tm = min(1024, t)
    tn = P_COLS // 4
    return pl.pallas_call(
        _in_proj_kernel,
        grid=(t // tm, P_COLS // tn),
        in_specs=[
            pl.BlockSpec((tm, D_MODEL), lambda i, j: (i, 0)),
            pl.BlockSpec((1, D_MODEL), lambda i, j: (0, 0)),
            pl.BlockSpec((D_MODEL, tn), lambda i, j: (0, j)),
            pl.BlockSpec((D_MODEL, LANES), lambda i, j: (0, 0)),
        ],
        out_specs=[
            pl.BlockSpec((tm, tn), lambda i, j: (i, j)),
            pl.BlockSpec((tm, LANES), lambda i, j: (i, 0)),
        ],
        out_shape=[jax.ShapeDtypeStruct((t, P_COLS), BF16), jax.ShapeDtypeStruct((t, LANES), F32)],
        scratch_shapes=[pltpu.VMEM((tm, D_MODEL), BF16)],
        compiler_params=_cparams(("arbitrary", "arbitrary"), 56),
        name="in_proj",
    )(x2d, g, wbig, wsmall)


def _mlstm_kernel(q_ref, k_ref, v_ref, o_ref, z_ref, g_ref, gb_ref, cw_ref, cb_ref, ng_ref,
                  y_ref, c_scr, n_scr, m_scr, xbuf):
    @pl.when(pl.program_id(1) == 0)
    def _():
        c_scr[...] = jnp.zeros_like(c_scr)
        n_scr[...] = jnp.zeros_like(n_scr)
        m_scr[...] = jnp.zeros_like(m_scr)
        xbuf[:, 0:SUBLANES, :] = jnp.zeros((xbuf.shape[0], SUBLANES, 2 * ML_W), F32)

    for bb in range(q_ref.shape[0]):
        _mlstm_chunk(q_ref.at[bb], k_ref.at[bb], v_ref.at[bb], o_ref.at[bb], z_ref.at[bb], g_ref.at[bb],
                     gb_ref, cw_ref, cb_ref, ng_ref, y_ref.at[bb],
                     c_scr.at[bb], n_scr.at[bb], m_scr.at[bb], xbuf.at[bb])


def _mlstm_chunk(q_ref, k_ref, v_ref, o_ref, z_ref, g_ref, gb_ref, cw_ref, cb_ref, ng_ref,
                 y_ref, c_scr, n_scr, m_scr, xbuf):
    L = TILE
    xbuf[SUBLANES:SUBLANES + L, 0:ML_W] = q_ref[...].astype(F32)
    xbuf[SUBLANES:SUBLANES + L, ML_W:2 * ML_W] = k_ref[...].astype(F32)
    conv = jnp.broadcast_to(cb_ref[...], (L, 2 * ML_W))
    for j in range(ML_CONV):
        off = SUBLANES - (ML_CONV - 1) + j
        conv = conv + cw_ref[j:j + 1, :] * xbuf[off:off + L, :]
    xbuf[0:SUBLANES, :] = xbuf[L:L + SUBLANES, :]
    qk = _silu(conv)

    gate = g_ref[...] + gb_ref[...]
    lf_cols = _log_sigmoid(gate)
    gate_t = gate.T
    li_rows = gate_t[G_MLI:G_MLI + SUBLANES, :]
    lf_rows = _log_sigmoid(gate_t[G_MLF:G_MLF + SUBLANES, :])
    r_io = lax.broadcasted_iota(I32, (L, L), 0)
    c_io = lax.broadcasted_iota(I32, (L, L), 1)
    causal = c_io <= r_io
    tri_u = (r_io <= c_io).astype(BF16)
    tri_l = causal.astype(BF16)
    bc_rows = _dot_f32_lhs(lf_rows, tri_u)
    bc_cols = _dot_f32_rhs(tri_l, lf_cols)

    for h in range(ML_HEADS):
        hs = slice(h * ML_HD, (h + 1) * ML_HD)
        qh = qk[:, hs]
        kh = qk[:, ML_W + h * ML_HD:ML_W + (h + 1) * ML_HD] * (ML_HD ** -0.5)
        vh = v_ref[:, hs]
        qb = qh.astype(BF16)
        kb = kh.astype(BF16)
        bc_col = bc_cols[:, G_MLF + h:G_MLF + h + 1]
        bc_row = bc_rows[h:h + 1, :]
        li_row = li_rows[h:h + 1, :]
        m_prev = m_scr[h, :, 0:1]

        dmat = jnp.where(causal, bc_col - (bc_row - li_row), -jnp.inf)
        inter = bc_col + m_prev
        m_row = jnp.maximum(inter, jnp.max(dmat, axis=1, keepdims=True))
        w_intra = jnp.exp(dmat - m_row)
        w_inter = jnp.exp(inter - m_row)
        s_qk = _dot_nt(qb, kb) * w_intra
        num = _dot(s_qk.astype(BF16), vh) + w_inter * _dot(qb, c_scr[h].astype(BF16))
        qn = jnp.sum(qh * n_scr[h], axis=1, keepdims=True)
        den = jnp.sum(s_qk, axis=1, keepdims=True) + w_inter * qn
        hh = num / jnp.maximum(jnp.abs(den), jnp.exp(-m_row))

        g_tot = bc_row[:, L - 1:L]
        a_row = g_tot - bc_row + li_row
        m_new = jnp.maximum(g_tot + m_prev, jnp.max(a_row, axis=1, keepdims=True))
        decay = jnp.exp(g_tot + m_prev - m_new)
        wa_row = jnp.exp(a_row - m_new)
        ktw = (kh.T * wa_row).astype(BF16)
        c_scr[h] = decay * c_scr[h] + _dot(ktw, vh)
        wa8 = jnp.broadcast_to(wa_row, (SUBLANES, L)).astype(BF16)
        n_scr[h] = decay * n_scr[h] + _dot(wa8, kb)[0:1, :]
        m_scr[h] = jnp.broadcast_to(m_new, (1, LANES))

        hm = hh * _sigmoid(o_ref[:, hs].astype(F32))
        hm = hm * lax.rsqrt(jnp.mean(hm * hm, axis=1, keepdims=True) + EPS) * ng_ref[:, hs]
        y_ref[:, hs] = (hm * _silu(z_ref[:, hs].astype(F32))).astype(BF16)


def _mlstm(p, g, gbias, conv_w, conv_b, norm_g, b, s):
    nt = s // TILE
    nb = 2 if b % 2 == 0 else 1
    p3 = p.reshape(b, s, P_COLS)
    g3 = g.reshape(b, s, LANES)

    def pcol(c):
        return pl.BlockSpec((nb, TILE, ML_W), lambda bi, ti, c=c: (bi, ti, c))

    full = lambda shape: pl.BlockSpec(shape, lambda bi, ti: (0,) * len(shape))
    y = pl.pallas_call(
        _mlstm_kernel,
        grid=(b // nb, nt),
        in_specs=[pcol(0), pcol(1), pcol(2), pcol(3), pcol(4),
                  pl.BlockSpec((nb, TILE, LANES), lambda bi, ti: (bi, ti, 0)),
                  full((1, LANES)), full((ML_CONV, 2 * ML_W)), full((1, 2 * ML_W)), full((1, ML_W))],
        out_specs=pl.BlockSpec((nb, TILE, ML_W), lambda bi, ti: (bi, ti, 0)),
        out_shape=jax.ShapeDtypeStruct((b, s, ML_W), BF16),
        scratch_shapes=[pltpu.VMEM((nb, ML_HEADS, ML_HD, ML_HD), F32),
                        pltpu.VMEM((nb, ML_HEADS, 1, ML_HD), F32),
                        pltpu.VMEM((nb, ML_HEADS, 1, LANES), F32),
                        pltpu.VMEM((nb, TILE + SUBLANES, 2 * ML_W), F32)],
        compiler_params=_cparams(("arbitrary", "arbitrary"), 40),
        name="mlstm",
    )(p3, p3, p3, p3, p3, g3, gbias, conv_w, conv_b, norm_g)
    return y.reshape(b * s, ML_W)


def _gla_kernel(q_ref, k_ref, v_ref, r_ref, g_ref, wa_ref, ba_ref, ng_ref, y_ref, st_scr):
    L = TILE
    H = GLA_HEADS

    @pl.when(pl.program_id(1) == 0)
    def _():
        st_scr[...] = jnp.zeros_like(st_scr)

    la = _log_sigmoid(_dot(g_ref[...].astype(BF16), wa_ref[...]) + ba_ref[...]) * (1.0 / GLA_TAU)
    r_io = lax.broadcasted_iota(I32, (L, L), 0)
    c_io = lax.broadcasted_iota(I32, (L, L), 1)
    tri_l = (c_io <= r_io).astype(BF16)
    bcum = _dot_f32_rhs(tri_l, la)
    btot = bcum[L - 1:L, :]
    bmid = bcum[L // 2 - 1:L // 2, :]

    q = q_ref[...].astype(F32) * (GLA_DK ** -0.5)
    k = k_ref[...].astype(F32)
    v = v_ref[...]
    q_in = (q * jnp.exp(bcum - bmid)).astype(BF16)
    k_in = k * jnp.exp(bmid - bcum)
    q_st = (q * jnp.exp(bcum)).astype(BF16)
    k_st = (k * jnp.exp(btot - bcum)).astype(BF16)

    kt = k_in.T
    kt4 = jnp.concatenate([kt] * H, axis=1)
    rr = lax.broadcasted_iota(I32, (GLA_WK, H * L), 0)
    cc = lax.broadcasted_iota(I32, (GLA_WK, H * L), 1)
    k_bd = jnp.where(rr // GLA_DK == cc // L, kt4, 0.0).astype(BF16)
    att = _dot(q_in, k_bd)
    ar = lax.broadcasted_iota(I32, (L, H * L), 0)
    ac = lax.broadcasted_iota(I32, (L, H * L), 1)
    att = jnp.where(ac % L <= ar, att, 0.0).astype(BF16)
    v4 = jnp.concatenate([v] * H, axis=0)
    vr = lax.broadcasted_iota(I32, (H * L, GLA_W), 0)
    vc = lax.broadcasted_iota(I32, (H * L, GLA_W), 1)
    v_bd = jnp.where(vr // L == vc // GLA_DV, v4, jnp.zeros_like(v4))
    o = _dot(att, v_bd) + _dot_nt(q_st, st_scr[...].astype(BF16))

    sr = lax.broadcasted_iota(I32, (GLA_W, GLA_WK), 0)
    sc = lax.broadcasted_iota(I32, (GLA_W, GLA_WK), 1)
    upd = _dot(v.astype(F32).T.astype(BF16), k_st)
    st_scr[...] = st_scr[...] * jnp.exp(btot) + jnp.where(sr // GLA_DV == sc // GLA_DK, upd, 0.0)

    for h in range(H):
        hs = slice(h * GLA_DV, (h + 1) * GLA_DV)
        oh = o[:, hs]
        oh = oh * lax.rsqrt(jnp.mean(oh * oh, axis=1, keepdims=True) + EPS) * ng_ref[:, hs]
        y_ref[:, hs] = (oh * _silu(r_ref[:, hs].astype(F32))).astype(BF16)


def _gla(p, g, wa_pad, b_a, norm_g, b, s):
    nt = s // TILE
    t = b * s

    def pcol(width, off):
        assert off % width == 0
        return pl.BlockSpec((TILE, width), lambda bi, ti: (bi * nt + ti, off // width))

    full = lambda shape: pl.BlockSpec(shape, lambda bi, ti: (0,) * len(shape))
    return pl.pallas_call(
        _gla_kernel,
        grid=(b, nt),
        in_specs=[pcol(GLA_WK, P_GLA_Q), pcol(GLA_WK, P_GLA_K), pcol(GLA_W, P_GLA_V), pcol(GLA_W, P_GLA_R),
                  pl.BlockSpec((TILE, LANES), lambda bi, ti: (bi * nt + ti, 0)),
                  full((LANES, GLA_WK)), full((1, GLA_WK)), full((1, GLA_W))],
        out_specs=pl.BlockSpec((TILE, GLA_W), lambda bi, ti: (bi * nt + ti, 0)),
        out_shape=jax.ShapeDtypeStruct((t, GLA_W), BF16),
        scratch_shapes=[pltpu.VMEM((GLA_W, GLA_WK), F32)],
        compiler_params=_cparams(("arbitrary", "arbitrary"), 32),
        name="gla",
    )(p, p, p, p, g, wa_pad, b_a, norm_g)


def _dsa_prep_kernel(q_ref, k_ref, v_ref, iq_ref, g_ref, qg_ref, kg_ref,
                     qt_ref, kn_ref, vt_ref, iqt_ref, ik_ref, iwt_ref):
    qn = []
    for h in range(DSA_HEADS):
        hs = slice(h * DSA_HD, (h + 1) * DSA_HD)
        q = q_ref[:, hs].astype(F32)
        k = k_ref[:, hs].astype(F32)
        qh = q * lax.rsqrt(jnp.mean(q * q, axis=1, keepdims=True) + EPS) * qg_ref[...]
        kh = k * lax.rsqrt(jnp.mean(k * k, axis=1, keepdims=True) + EPS) * kg_ref[...]
        qn.append(qh * (DSA_HD ** -0.5 * LOG2E))
        kn_ref[:, hs] = kh.astype(BF16)
        r0 = h * DSA_VROWS
        vt_ref[r0:r0 + DSA_HD, :] = v_ref[:, hs].astype(F32).T.astype(BF16)
        vt_ref[r0 + DSA_HD:r0 + DSA_VROWS, :] = jnp.ones((DSA_VROWS - DSA_HD, vt_ref.shape[1]), BF16)
    qt_ref[...] = jnp.concatenate(qn, axis=1).T.astype(BF16)
    iqt_ref[...] = iq_ref[...].astype(F32).T.astype(BF16)
    g = g_ref[...]
    ik_ref[...] = g[:, G_IDXK:G_IDXK + IDX_DIM].astype(BF16)
    iw_scale = (IDX_HEADS ** -0.5) * (IDX_DIM ** -0.5)
    iwt_ref[...] = g.T[G_IDXW:G_IDXW + IDX_HEADS, :] * iw_scale


def _dsa_prep(p, g, q_g, k_g, b, s):
    t = b * s
    tm = min(512, s)
    nt = s // tm
    full = lambda shape: pl.BlockSpec(shape, lambda bi, ti: (0,) * len(shape))
    tok = lambda width, cb: pl.BlockSpec((tm, width), lambda bi, ti: (bi * nt + ti, cb))
    feat = lambda rows: pl.BlockSpec((None, rows, tm), lambda bi, ti: (bi, 0, ti))
    return pl.pallas_call(
        _dsa_prep_kernel,
        grid=(b, nt),
        in_specs=[tok(DSA_W, P_DSA_Q // DSA_W), tok(DSA_W, P_DSA_K // DSA_W), tok(DSA_W, P_DSA_V // DSA_W),
                  tok(IDX_W, P_IDXQ // IDX_W), tok(LANES, 0), full((1, DSA_HD)), full((1, DSA_HD))],
        out_specs=[feat(DSA_W), tok(DSA_W, 0), feat(DSA_HEADS * DSA_VROWS), feat(IDX_W), tok(IDX_DIM, 0),
                   feat(IDX_HEADS)],
        out_shape=[jax.ShapeDtypeStruct((b, DSA_W, s), BF16),
                   jax.ShapeDtypeStruct((t, DSA_W), BF16),
                   jax.ShapeDtypeStruct((b, DSA_HEADS * DSA_VROWS, s), BF16),
                   jax.ShapeDtypeStruct((b, IDX_W, s), BF16),
                   jax.ShapeDtypeStruct((t, IDX_DIM), BF16),
                   jax.ShapeDtypeStruct((b, IDX_HEADS, s), F32)],
        compiler_params=_cparams(("arbitrary", "arbitrary"), 40),
        name="dsa_prep",
    )(p, p, p, p, g, q_g, k_g)


def _dsa_kernel(qt_ref, iqt_ref, iwt_ref, z_ref, kn_ref, vt_ref, ik_ref, bias_ref, y_ref,
                keys_scr, acc_scr, m_scr, iq2_scr, iw2_scr, s_scr, ml_scr, *, topk, idx_bits):
    Q, TK, SUB = TILE, DSA_TK, DSA_SUB
    i = pl.program_id(1)
    n_it = (i + SUB) // SUB
    qpos = i * Q + lax.broadcasted_iota(I32, (1, Q), 1)
    limit = (qpos // CHUNK + 1) * CHUNK
    krow = lax.broadcasted_iota(I32, (TK, Q), 0)

    def koff(kt):
        return pl.multiple_of(kt * TK, TK)

    for j in range(IDX_HEADS // 2):
        for e in range(2):
            h = 2 * j + e
            iq2_scr[j, :, e * Q:(e + 1) * Q] = iqt_ref[h * IDX_DIM:(h + 1) * IDX_DIM, :]
            iw2_scr[j, :, e * Q:(e + 1) * Q] = iwt_ref[h:h + 1, :]

    CR = 8 * SUBLANES

    def to_key(x):
        bits = lax.bitcast_convert_type(x, I32)
        return bits ^ ((bits >> 31) & 0x7FFFFFFF)

    def score_tile(kt, last):
        off = koff(kt)
        ikt = ik_ref[pl.ds(off, TK), :]
        sc = jnp.zeros((TK, Q), F32)
        for j in range(IDX_HEADS // 2):
            w = jnp.maximum(_dot(ikt, iq2_scr[j]), 0.0) * iw2_scr[j]
            sc = sc + (w[:, 0:Q] + w[:, Q:2 * Q])
        keys = to_key(sc)
        if last:
            keys = jnp.where(off + krow >= limit, INT_MIN, keys)
        keys_scr[pl.ds(off, TK), :] = keys

    def score_pair(jp, carry):
        score_tile(2 * jp, False)
        score_tile(2 * jp + 1, False)
        return carry

    n_full = n_it - 1
    lax.fori_loop(0, n_full // 2, score_pair, 0)

    @pl.when(n_full % 2 == 1)
    def _():
        score_tile(n_full - 1, False)

    score_tile(n_it - 1, True)

    def count(pred):
        def one(kt, acc):
            off = koff(kt)
            ind = jnp.where(pred(keys_scr[pl.ds(off, TK), :], off), 1.0, 0.0)
            return acc + jnp.sum(ind.reshape(TK // CR, CR, Q), axis=0)

        def two(jp, acc):
            return one(2 * jp + 1, one(2 * jp, acc))

        acc = lax.fori_loop(0, n_it // 2, two, jnp.zeros((CR, Q), F32))
        acc = lax.cond(n_it % 2 == 1, lambda a: one(n_it - 1, a), lambda a: a, acc)
        return jnp.sum(acc, axis=0, keepdims=True)

    take_all = limit <= topk

    def thr_bit(b, carry):
        t, n_t = carry
        cand = t + lax.shift_left(jnp.int32(1), 31 - b)
        n_c = count(lambda kk, off: kk >= cand)
        ok = n_c >= topk
        return jnp.where(ok, cand, t), jnp.where(ok, n_c, n_t)

    thr, n_ge = lax.fori_loop(0, 32, thr_bit, (jnp.full((1, Q), INT_MIN, I32), jnp.zeros((1, Q), F32)))
    thr = jnp.where(take_all, INT_MIN + 1, thr)

    @pl.when(jnp.max(n_ge) > topk)
    def _():
        need = topk - count(lambda kk, off: kk > thr)

        def idx_bit(b, j):
            cand = j + lax.shift_left(jnp.int32(1), idx_bits - 1 - b)
            n_before = count(lambda kk, off: (kk == thr) & (off + krow < cand))
            return jnp.where(n_before < need, cand, j)

        last = lax.fori_loop(0, idx_bits, idx_bit, jnp.zeros((1, Q), I32))

        def drop(kt, carry):
            off = koff(kt)
            kk = keys_scr[pl.ds(off, TK), :]
            keys_scr[pl.ds(off, TK), :] = jnp.where((kk == thr) & (off + krow > last), INT_MIN, kk)
            return carry

        lax.fori_loop(0, n_it, drop, 0)

    m_scr[...] = jnp.full(m_scr.shape, NEG_BIG, F32)
    acc_scr[...] = jnp.zeros_like(acc_scr)

    def logits(kt, buf, thr_kt):
        off = koff(kt)
        sel = keys_scr[pl.ds(off, TK), :] >= thr_kt
        bsel = [jnp.clip(i - (kt * SUB + j), 0, 2) for j in range(SUB)]
        for h in range(DSA_HEADS):
            hs = slice(h * DSA_HD, (h + 1) * DSA_HD)
            s = _dot(kn_ref[pl.ds(off, TK), hs], qt_ref[hs, :])
            s = s + jnp.concatenate([bias_ref[h, bsel[j]] for j in range(SUB)], axis=0)
            s = jnp.where(sel, s, NEG_BIG)
            s_scr[buf, h] = s
            ml_scr[buf, h] = jnp.max(s, axis=0, keepdims=True)

    def values(kt, buf):
        off = koff(kt)
        for h in range(DSA_HEADS):
            m_old = m_scr[h]
            m_new = jnp.maximum(m_old, ml_scr[buf, h])
            p = jnp.exp2(s_scr[buf, h] - m_new).astype(BF16)
            v1 = vt_ref[h * DSA_VROWS:(h + 1) * DSA_VROWS, pl.ds(off, TK)]
            acc_scr[h] = jnp.exp2(m_old - m_new) * acc_scr[h] + _dot(v1, p)
            m_scr[h] = m_new

    def att_group(first, size):
        steps = []
        for e in range(size):
            kt = first + e
            steps.append((jnp.minimum(kt, n_it - 1), jnp.where(kt < n_it, thr, jnp.int32(2 ** 31 - 1))))
        for e in range(size):
            logits(steps[e][0], e, steps[e][1])
        for e in range(size):
            values(steps[e][0], e)

    def att_big(gi, carry):
        att_group(gi * DSA_GROUP, DSA_GROUP)
        return carry

    def att_small(gi, carry):
        att_group(n_big * DSA_GROUP + 2 * gi, 2)
        return carry

    n_big = n_it // DSA_GROUP
    lax.fori_loop(0, n_big, att_big, 0)
    lax.fori_loop(0, (n_it - n_big * DSA_GROUP + 1) // 2, att_small, 0)
    for h in range(DSA_HEADS):
        hs = slice(h * DSA_HD, (h + 1) * DSA_HD)
        acc = acc_scr[h]
        out = (acc[0:DSA_HD, :] / acc[DSA_HD:DSA_HD + 1, :]).T
        y_ref[:, hs] = (out * _silu(z_ref[:, hs].astype(F32))).astype(BF16)


def _dsa(p, qt, kn, vt, iqt, ik, iwt, bias_tiles, b, s):
    assert s % DSA_TK == 0
    nt = s // TILE
    t = b * s
    topk = min(TOPK_MAX, s // 4)
    idx_bits = max(1, (s - 1).bit_length())
    kernel = functools.partial(_dsa_kernel, topk=topk, idx_bits=idx_bits)
    once = pl.Buffered(1)
    qfeat = lambda rows: pl.BlockSpec((None, rows, TILE), lambda bi, ti: (bi, 0, ti))
    return pl.pallas_call(
        kernel,
        grid=(b, nt),
        in_specs=[qfeat(DSA_W), qfeat(IDX_W), qfeat(IDX_HEADS),
                  pl.BlockSpec((TILE, DSA_W), lambda bi, ti: (bi * nt + ti, P_DSA_Z // DSA_W)),
                  pl.BlockSpec((s, DSA_W), lambda bi, ti: (bi, 0), pipeline_mode=once),
                  pl.BlockSpec((None, DSA_HEADS * DSA_VROWS, s), lambda bi, ti: (bi, 0, 0), pipeline_mode=once),
                  pl.BlockSpec((s, IDX_DIM), lambda bi, ti: (bi, 0), pipeline_mode=once),
                  pl.BlockSpec((DSA_HEADS, 3, TILE, TILE), lambda bi, ti: (0, 0, 0, 0))],
        out_specs=pl.BlockSpec((TILE, DSA_W), lambda bi, ti: (bi * nt + ti, 0)),
        out_shape=jax.ShapeDtypeStruct((t, DSA_W), BF16),
        scratch_shapes=[pltpu.VMEM((s, TILE), I32),
                        pltpu.VMEM((DSA_HEADS, DSA_VROWS, TILE), F32),
                        pltpu.VMEM((DSA_HEADS, 1, TILE), F32),
                        pltpu.VMEM((IDX_HEADS // 2, IDX_DIM, 2 * TILE), BF16),
                        pltpu.VMEM((IDX_HEADS // 2, 1, 2 * TILE), F32),
                        pltpu.VMEM((DSA_GROUP, DSA_HEADS, DSA_TK, TILE), F32),
                        pltpu.VMEM((DSA_GROUP, DSA_HEADS, 1, TILE), F32)],
        compiler_params=_cparams(("arbitrary", "arbitrary"), 48),
        name="dsa",
    )(qt, iqt, iwt, p, kn, vt, ik, bias_tiles)


def _out_proj_kernel(yml_ref, ygla_ref, ydsa_ref, w_ref, x_ref, o_ref):
    y = _dot(yml_ref[...], w_ref[0:ML_W, :])
    y = y + _dot(ygla_ref[...], w_ref[ML_W:ML_W + GLA_W, :])
    y = y + _dot(ydsa_ref[...], w_ref[ML_W + GLA_W:D_MIX, :])
    o_ref[...] = x_ref[...] + y


def _out_proj(y_ml, y_gla, y_dsa, w_out, x2d):
    t = x2d.shape[0]
    tm = min(1024, t)
    row = lambda width: pl.BlockSpec((tm, width), lambda i: (i, 0))
    return pl.pallas_call(
        _out_proj_kernel,
        grid=(t // tm,),
        in_specs=[row(ML_W), row(GLA_W), row(DSA_W),
                  pl.BlockSpec((D_MIX, D_MODEL), lambda i: (0, 0)), row(D_MODEL)],
        out_specs=row(D_MODEL),
        out_shape=jax.ShapeDtypeStruct((t, D_MODEL), F32),
        compiler_params=_cparams(("arbitrary",), 40),
        name="out_proj",
    )(y_ml, y_gla, y_dsa, w_out, x2d)


_COLUMN_LAYOUT = (
    ('ml_q', ML_W), ('ml_k', ML_W), ('ml_v', ML_W), ('ml_o', ML_W), ('ml_z', ML_W),
    ('ml_i', ML_HEADS), ('ml_f', ML_HEADS),
    ('gla_q', GLA_WK), ('gla_k', GLA_WK), ('gla_v', GLA_W), ('gla_a', GLA_RANK), ('gla_r', GLA_W),
    ('dsa_q', DSA_W), ('dsa_k', DSA_W), ('dsa_v', DSA_W), ('dsa_z', DSA_W),
    ('idx_q', IDX_W), ('idx_k', IDX_DIM), ('idx_w', IDX_HEADS),
)


def _split_w_in(w_in):
    cols, off = {}, 0
    for name, width in _COLUMN_LAYOUT:
        cols[name] = w_in[:, off:off + width]
        off += width
    return cols


def _pack_w_in(w_in):
    c = _split_w_in(w_in)
    d = w_in.shape[0]
    wbig = jnp.concatenate([c['ml_q'], c['ml_k'], c['ml_v'], c['ml_o'], c['ml_z'], c['idx_q'],
                            c['gla_q'], c['gla_k'], c['gla_v'], c['gla_r'],
                            c['dsa_q'], c['dsa_k'], c['dsa_v'], c['dsa_z']], axis=1)
    assert wbig.shape[1] == P_COLS
    z = lambda n: jnp.zeros((d, n), w_in.dtype)
    wsmall = jnp.concatenate([c['idx_k'], c['ml_i'], z(4), c['ml_f'], z(4), c['gla_a'], c['idx_w'],
                              z(LANES - G_IDXW - IDX_HEADS)], axis=1)
    assert wsmall.shape[1] == LANES
    return wbig.astype(BF16), wsmall.astype(BF16)


def _rel_bucket(rel):
    half = REL_BUCKETS // 2
    max_exact = half // 2
    ret = jnp.where(rel > 0, half, 0)
    n = jnp.abs(rel)
    nf = jnp.maximum(n, 1).astype(F32)
    large = max_exact + (jnp.log(nf / max_exact) / math.log(REL_MAX_DIST / max_exact)
                         * (half - max_exact)).astype(I32)
    large = jnp.minimum(large, half - 1)
    return ret + jnp.where(n < max_exact, n, large)


def _bias_tiles(rel_bias):
    assert REL_MAX_DIST <= TILE
    kl = jnp.arange(TILE, dtype=I32)[:, None]
    ql = jnp.arange(TILE, dtype=I32)[None, :]
    far = jnp.full((TILE, TILE), -(TILE + 1), I32)
    rel = jnp.stack([kl - ql, kl - TILE - ql, far])
    tab = jnp.transpose(rel_bias.astype(F32)[_rel_bucket(rel)], (3, 0, 1, 2))
    return (tab - tab[:, 2:3]) * LOG2E


def _layer(x2d, b, s, norm_g, w_in, ml_conv_w, ml_conv_b, ml_b_i, ml_b_f, ml_norm_g,
           gla_w_a, gla_b_a, gla_norm_g, dsa_q_g, dsa_k_g, w_out, bias_tiles):
    wbig, wsmall = _pack_w_in(w_in)
    p, g = _in_proj(x2d, norm_g.reshape(1, D_MODEL), wbig, wsmall)

    gbias = jnp.zeros((1, LANES), F32)
    gbias = gbias.at[0, G_MLI:G_MLI + ML_HEADS].set(ml_b_i).at[0, G_MLF:G_MLF + ML_HEADS].set(ml_b_f)
    y_ml = _mlstm(p, g, gbias, ml_conv_w, ml_conv_b.reshape(1, -1), ml_norm_g.reshape(1, -1), b, s)

    wa_pad = jnp.zeros((LANES, GLA_WK), F32).at[G_GLAA:G_GLAA + GLA_RANK].set(gla_w_a).astype(BF16)
    y_gla = _gla(p, g, wa_pad, gla_b_a.reshape(1, -1), gla_norm_g.reshape(1, -1), b, s)

    qt, kn, vt, iqt, ik, iwt = _dsa_prep(p, g, dsa_q_g.reshape(1, -1), dsa_k_g.reshape(1, -1), b, s)
    y_dsa = _dsa(p, qt, kn, vt, iqt, ik, iwt, bias_tiles, b, s)

    return _out_proj(y_ml, y_gla, y_dsa, w_out.astype(BF16), x2d)


def kernel(x, norm_g, w_in, ml_conv_w, ml_conv_b, ml_b_i, ml_b_f, ml_norm_g, gla_w_a, gla_b_a, gla_norm_g,
           dsa_q_g, dsa_k_g, w_out, rel_bias):
    b, s, d = x.shape
    assert d == D_MODEL and s % TILE == 0
    depth = norm_g.shape[0]
    bias_tiles = _bias_tiles(rel_bias)
    x2d = x.reshape(b * s, d)
    for layer in range(depth):
        x2d = _layer(x2d, b, s, norm_g[layer], w_in[layer], ml_conv_w[layer], ml_conv_b[layer],
                     ml_b_i[layer], ml_b_f[layer], ml_norm_g[layer], gla_w_a[layer], gla_b_a[layer],
                     gla_norm_g[layer], dsa_q_g[layer], dsa_k_g[layer], w_out[layer], bias_tiles)
    return x2d.reshape(b, s, d)
```

```python
import functools
import math

import jax
import jax.numpy as jnp
from jax import lax
from jax.experimental import pallas as pl
from jax.experimental.pallas import tpu as pltpu

F32 = jnp.float32
BF16 = jnp.bfloat16
I32 = jnp.int32

D_MODEL = 1024
CHUNK = 64
EPS = 1e-6
ML_HEADS, ML_HD = 4, 256
ML_W = ML_HEADS * ML_HD
ML_CONV = 4
GLA_HEADS, GLA_DK, GLA_DV = 4, 64, 128
GLA_WK = GLA_HEADS * GLA_DK
GLA_W = GLA_HEADS * GLA_DV
GLA_RANK = 16
GLA_TAU = 16.0
DSA_HEADS, DSA_HD = 4, 128
DSA_W = DSA_HEADS * DSA_HD
IDX_HEADS, IDX_DIM = 8, 64
IDX_W = IDX_HEADS * IDX_DIM
TOPK_MAX = 256
REL_BUCKETS, REL_MAX_DIST = 32, 128
D_MIX = ML_W + GLA_W + DSA_W

LANES = 128
SUBLANES = 8

TILE = 128
DSA_TK = 512
DSA_SUB = DSA_TK // TILE
DSA_GROUP = 4
DSA_VROWS = DSA_HD + 16
LOG2E = 1.4426950408889634

P_ML = 0
P_IDXQ = 5 * ML_W
P_GLA_Q = P_IDXQ + IDX_W
P_GLA_K = P_GLA_Q + GLA_WK
P_GLA_V = P_GLA_K + GLA_WK
P_GLA_R = P_GLA_V + GLA_W
P_DSA_Q = P_GLA_R + GLA_W
P_DSA_K = P_DSA_Q + DSA_W
P_DSA_V = P_DSA_K + DSA_W
P_DSA_Z = P_DSA_V + DSA_W
P_COLS = P_DSA_Z + DSA_W
G_IDXK = 0
G_MLI = 64
G_MLF = 72
G_GLAA = 80
G_IDXW = 96

INT_MIN = -(2 ** 31)
NEG_BIG = -1e30


def _cparams(sem, vmem_mb):
    return pltpu.CompilerParams(dimension_semantics=sem, vmem_limit_bytes=vmem_mb << 20)


def _dot(a, b):
    return jnp.dot(a, b, preferred_element_type=F32)


def _dot_nt(a, b):
    return lax.dot_general(a, b, (((1,), (1,)), ((), ())), preferred_element_type=F32)


def _split3(x):
    hi = x.astype(BF16)
    r1 = x - hi.astype(F32)
    mid = r1.astype(BF16)
    lo = (r1 - mid.astype(F32)).astype(BF16)
    return hi, mid, lo


def _dot_f32_lhs(x, m01):
    hi, mid, lo = _split3(x)
    return _dot(hi, m01) + _dot(mid, m01) + _dot(lo, m01)


def _dot_f32_rhs(m01, x):
    hi, mid, lo = _split3(x)
    return _dot(m01, hi) + _dot(m01, mid) + _dot(m01, lo)


def _log_sigmoid(x):
    return jnp.minimum(x, 0.0) - jnp.log(1.0 + jnp.exp(-jnp.abs(x)))


def _sigmoid(x):
    return 1.0 / (1.0 + jnp.exp(-x))


def _silu(x):
    return x * _sigmoid(x)


def _in_proj_kernel(x_ref, g_ref, wb_ref, ws_ref, p_ref, gate_ref, xn_ref):
    @pl.when(pl.program_id(1) == 0)
    def _():
        x = x_ref[...]
        ms = jnp.mean(x * x, axis=-1, keepdims=True)
        xn = (x * lax.rsqrt(ms + EPS) * g_ref[...]).astype(BF16)
        xn_ref[...] = xn
        gate_ref[...] = _dot(xn, ws_ref[...])

    p_ref[...] = _dot(xn_ref[...], wb_ref[...]).astype(BF16)


def _in_proj(x2d, g, wbig, wsmall):
    t = x2d.shape[0]
    tm = min(1024, t)
    tn = P_COLS // 4
    return pl.pallas_call(
        _in_proj_kernel,
        grid=(t // tm, P_COLS // tn),
        in_specs=[
            pl.BlockSpec((tm, D_MODEL), lambda i, j: (i, 0)),
            pl.BlockSpec((1, D_MODEL), lambda i, j: (0, 0)),
            pl.BlockSpec((D_MODEL, tn), lambda i, j: (0, j)),
            pl.BlockSpec((D_MODEL, LANES), lambda i, j: (0, 0)),
        ],
        out_specs=[
            pl.BlockSpec((tm, tn), lambda i, j: (i, j)),
            pl.BlockSpec((tm, LANES), lambda i, j: (i, 0)),
        ],
        out_shape=[jax.ShapeDtypeStruct((t, P_COLS), BF16), jax.ShapeDtypeStruct((t, LANES), F32)],
        scratch_shapes=[pltpu.VMEM((tm, D_MODEL), BF16)],
        compiler_params=_cparams(("arbitrary", "arbitrary"), 56),
        name="in_proj",
    )(x2d, g, wbig, wsmall)


def _mlstm_kernel(q_ref, k_ref, v_ref, o_ref, z_ref, g_ref, gb_ref, cw_ref, cb_ref, ng_ref,
                  y_ref, c_scr, n_scr, m_scr, xbuf):
    L = TILE

    @pl.when(pl.program_id(1) == 0)
    def _():
        c_scr[...] = jnp.zeros_like(c_scr)
        n_scr[...] = jnp.zeros_like(n_scr)
        m_scr[...] = jnp.zeros_like(m_scr)
        xbuf[0:SUBLANES, :] = jnp.zeros((SUBLANES, 2 * ML_W), F32)

    xbuf[SUBLANES:SUBLANES + L, 0:ML_W] = q_ref[...].astype(F32)
    xbuf[SUBLANES:SUBLANES + L, ML_W:2 * ML_W] = k_ref[...].astype(F32)
    conv = jnp.broadcast_to(cb_ref[...], (L, 2 * ML_W))
    for j in range(ML_CONV):
        off = SUBLANES - (ML_CONV - 1) + j
        conv = conv + cw_ref[j:j + 1, :] * xbuf[off:off + L, :]
    xbuf[0:SUBLANES, :] = xbuf[L:L + SUBLANES, :]
    qk = _silu(conv)

    gate = g_ref[...] + gb_ref[...]
    lf_cols = _log_sigmoid(gate)
    gate_t = gate.T
    li_rows = gate_t[G_MLI:G_MLI + SUBLANES, :]
    lf_rows = _log_sigmoid(gate_t[G_MLF:G_MLF + SUBLANES, :])
    r_io = lax.broadcasted_iota(I32, (L, L), 0)
    c_io = lax.broadcasted_iota(I32, (L, L), 1)
    causal = c_io <= r_io
    tri_u = (r_io <= c_io).astype(BF16)
    tri_l = causal.astype(BF16)
    bc_rows = _dot_f32_lhs(lf_rows, tri_u)
    bc_cols = _dot_f32_rhs(tri_l, lf_cols)

    for h in range(ML_HEADS):
        hs = slice(h * ML_HD, (h + 1) * ML_HD)
        qh = qk[:, hs]
        kh = qk[:, ML_W + h * ML_HD:ML_W + (h + 1) * ML_HD] * (ML_HD ** -0.5)
        vh = v_ref[:, hs]
        qb = qh.astype(BF16)
        kb = kh.astype(BF16)
        bc_col = bc_cols[:, G_MLF + h:G_MLF + h + 1]
        bc_row = bc_rows[h:h + 1, :]
        li_row = li_rows[h:h + 1, :]
        m_prev = m_scr[h, :, 0:1]

        dmat = jnp.where(causal, bc_col - (bc_row - li_row), -jnp.inf)
        inter = bc_col + m_prev
        m_row = jnp.maximum(inter, jnp.max(dmat, axis=1, keepdims=True))
        w_intra = jnp.exp(dmat - m_row)
        w_inter = jnp.exp(inter - m_row)
        s_qk = _dot_nt(qb, kb) * w_intra
        num = _dot(s_qk.astype(BF16), vh) + w_inter * _dot(qb, c_scr[h].astype(BF16))
        qn = jnp.sum(qh * n_scr[h], axis=1, keepdims=True)
        den = jnp.sum(s_qk, axis=1, keepdims=True) + w_inter * qn
        hh = num / jnp.maximum(jnp.abs(den), jnp.exp(-m_row))

        g_tot = bc_row[:, L - 1:L]
        a_row = g_tot - bc_row + li_row
        m_new = jnp.maximum(g_tot + m_prev, jnp.max(a_row, axis=1, keepdims=True))
        decay = jnp.exp(g_tot + m_prev - m_new)
        wa_row = jnp.exp(a_row - m_new)
        ktw = (kh.T * wa_row).astype(BF16)
        c_scr[h] = decay * c_scr[h] + _dot(ktw, vh)
        wa8 = jnp.broadcast_to(wa_row, (SUBLANES, L)).astype(BF16)
        n_scr[h] = decay * n_scr[h] + _dot(wa8, kb)[0:1, :]
        m_scr[h] = jnp.broadcast_to(m_new, (1, LANES))

        hm = hh * _sigmoid(o_ref[:, hs].astype(F32))
        hm = hm * lax.rsqrt(jnp.mean(hm * hm, axis=1, keepdims=True) + EPS) * ng_ref[:, hs]
        y_ref[:, hs] = (hm * _silu(z_ref[:, hs].astype(F32))).astype(BF16)


def _mlstm(p, g, gbias, conv_w, conv_b, norm_g, b, s):
    nt = s // TILE
    t = b * s

    def pcol(c):
        return pl.BlockSpec((TILE, ML_W), lambda bi, ti, c=c: (bi * nt + ti, c))

    full = lambda shape: pl.BlockSpec(shape, lambda bi, ti: (0,) * len(shape))
    return pl.pallas_call(
        _mlstm_kernel,
        grid=(b, nt),
        in_specs=[pcol(0), pcol(1), pcol(2), pcol(3), pcol(4),
                  pl.BlockSpec((TILE, LANES), lambda bi, ti: (bi * nt + ti, 0)),
                  full((1, LANES)), full((ML_CONV, 2 * ML_W)), full((1, 2 * ML_W)), full((1, ML_W))],
        out_specs=pl.BlockSpec((TILE, ML_W), lambda bi, ti: (bi * nt + ti, 0)),
        out_shape=jax.ShapeDtypeStruct((t, ML_W), BF16),
        scratch_shapes=[pltpu.VMEM((ML_HEADS, ML_HD, ML_HD), F32),
                        pltpu.VMEM((ML_HEADS, 1, ML_HD), F32),
                        pltpu.VMEM((ML_HEADS, 1, LANES), F32),
                        pltpu.VMEM((TILE + SUBLANES, 2 * ML_W), F32)],
        compiler_params=_cparams(("arbitrary", "arbitrary"), 40),
        name="mlstm",
    )(p, p, p, p, p, g, gbias, conv_w, conv_b, norm_g)


def _gla_kernel(q_ref, k_ref, v_ref, r_ref, g_ref, wa_ref, ba_ref, ng_ref, y_ref, st_scr):
    L = TILE
    H = GLA_HEADS

    @pl.when(pl.program_id(1) == 0)
    def _():
        st_scr[...] = jnp.zeros_like(st_scr)

    la = _log_sigmoid(_dot(g_ref[...].astype(BF16), wa_ref[...]) + ba_ref[...]) * (1.0 / GLA_TAU)
    r_io = lax.broadcasted_iota(I32, (L, L), 0)
    c_io = lax.broadcasted_iota(I32, (L, L), 1)
    tri_l = (c_io <= r_io).astype(BF16)
    bcum = _dot_f32_rhs(tri_l, la)
    btot = bcum[L - 1:L, :]
    bmid = 0.5 * btot

    q = q_ref[...].astype(F32) * (GLA_DK ** -0.5)
    k = k_ref[...].astype(F32)
    v = v_ref[...]
    q_in = (q * jnp.exp(bcum - bmid)).astype(BF16)
    k_in = k * jnp.exp(bmid - bcum)
    q_st = (q * jnp.exp(bcum)).astype(BF16)
    k_st = (k * jnp.exp(btot - bcum)).astype(BF16)

    kt = k_in.T
    kt4 = jnp.concatenate([kt] * H, axis=1)
    rr = lax.broadcasted_iota(I32, (GLA_WK, H * L), 0)
    cc = lax.broadcasted_iota(I32, (GLA_WK, H * L), 1)
    k_bd = jnp.where(rr // GLA_DK == cc // L, kt4, 0.0).astype(BF16)
    att = _dot(q_in, k_bd)
    ar = lax.broadcasted_iota(I32, (L, H * L), 0)
    ac = lax.broadcasted_iota(I32, (L, H * L), 1)
    att = jnp.where(ac % L <= ar, att, 0.0).astype(BF16)
    v4 = jnp.concatenate([v] * H, axis=0)
    vr = lax.broadcasted_iota(I32, (H * L, GLA_W), 0)
    vc = lax.broadcasted_iota(I32, (H * L, GLA_W), 1)
    v_bd = jnp.where(vr // L == vc // GLA_DV, v4, jnp.zeros_like(v4))
    o = _dot(att, v_bd) + _dot_nt(q_st, st_scr[...].astype(BF16))

    sr = lax.broadcasted_iota(I32, (GLA_W, GLA_WK), 0)
    sc = lax.broadcasted_iota(I32, (GLA_W, GLA_WK), 1)
    upd = _dot(v.astype(F32).T.astype(BF16), k_st)
    st_scr[...] = st_scr[...] * jnp.exp(btot) + jnp.where(sr // GLA_DV == sc // GLA_DK, upd, 0.0)

    for h in range(H):
        hs = slice(h * GLA_DV, (h + 1) * GLA_DV)
        oh = o[:, hs]
        oh = oh * lax.rsqrt(jnp.mean(oh * oh, axis=1, keepdims=True) + EPS) * ng_ref[:, hs]
        y_ref[:, hs] = (oh * _silu(r_ref[:, hs].astype(F32))).astype(BF16)


def _gla(p, g, wa_pad, b_a, norm_g, b, s):
    nt = s // TILE
    t = b * s

    def pcol(width, off):
        assert off % width == 0
        return pl.BlockSpec((TILE, width), lambda bi, ti: (bi * nt + ti, off // width))

    full = lambda shape: pl.BlockSpec(shape, lambda bi, ti: (0,) * len(shape))
    return pl.pallas_call(
        _gla_kernel,
        grid=(b, nt),
        in_specs=[pcol(GLA_WK, P_GLA_Q), pcol(GLA_WK, P_GLA_K), pcol(GLA_W, P_GLA_V), pcol(GLA_W, P_GLA_R),
                  pl.BlockSpec((TILE, LANES), lambda bi, ti: (bi * nt + ti, 0)),
                  full((LANES, GLA_WK)), full((1, GLA_WK)), full((1, GLA_W))],
        out_specs=pl.BlockSpec((TILE, GLA_W), lambda bi, ti: (bi * nt + ti, 0)),
        out_shape=jax.ShapeDtypeStruct((t, GLA_W), BF16),
        scratch_shapes=[pltpu.VMEM((GLA_W, GLA_WK), F32)],
        compiler_params=_cparams(("arbitrary", "arbitrary"), 32),
        name="gla",
    )(p, p, p, p, g, wa_pad, b_a, norm_g)


def _dsa_prep_kernel(q_ref, k_ref, v_ref, iq_ref, g_ref, qg_ref, kg_ref,
                     qt_ref, kn_ref, vt_ref, iqt_ref, ik_ref, iwt_ref):
    qn = []
    for h in range(DSA_HEADS):
        hs = slice(h * DSA_HD, (h + 1) * DSA_HD)
        q = q_ref[:, hs].astype(F32)
        k = k_ref[:, hs].astype(F32)
        qh = q * lax.rsqrt(jnp.mean(q * q, axis=1, keepdims=True) + EPS) * qg_ref[...]
        kh = k * lax.rsqrt(jnp.mean(k * k, axis=1, keepdims=True) + EPS) * kg_ref[...]
        qn.append(qh * (DSA_HD ** -0.5 * LOG2E))
        kn_ref[:, hs] = kh.astype(BF16)
        r0 = h * DSA_VROWS
        vt_ref[r0:r0 + DSA_HD, :] = v_ref[:, hs].astype(F32).T.astype(BF16)
        vt_ref[r0 + DSA_HD:r0 + DSA_VROWS, :] = jnp.ones((DSA_VROWS - DSA_HD, vt_ref.shape[1]), BF16)
    qt_ref[...] = jnp.concatenate(qn, axis=1).T.astype(BF16)
    iqt_ref[...] = iq_ref[...].astype(F32).T.astype(BF16)
    g = g_ref[...]
    ik_ref[...] = g[:, G_IDXK:G_IDXK + IDX_DIM].astype(BF16)
    iw_scale = (IDX_HEADS ** -0.5) * (IDX_DIM ** -0.5)
    iwt_ref[...] = g.T[G_IDXW:G_IDXW + IDX_HEADS, :] * iw_scale


def _dsa_prep(p, g, q_g, k_g, b, s):
    t = b * s
    tm = min(512, s)
    nt = s // tm
    full = lambda shape: pl.BlockSpec(shape, lambda bi, ti: (0,) * len(shape))
    tok = lambda width, cb: pl.BlockSpec((tm, width), lambda bi, ti: (bi * nt + ti, cb))
    feat = lambda rows: pl.BlockSpec((None, rows, tm), lambda bi, ti: (bi, 0, ti))
    return pl.pallas_call(
        _dsa_prep_kernel,
        grid=(b, nt),
        in_specs=[tok(DSA_W, P_DSA_Q // DSA_W), tok(DSA_W, P_DSA_K // DSA_W), tok(DSA_W, P_DSA_V // DSA_W),
                  tok(IDX_W, P_IDXQ // IDX_W), tok(LANES, 0), full((1, DSA_HD)), full((1, DSA_HD))],
        out_specs=[feat(DSA_W), tok(DSA_W, 0), feat(DSA_HEADS * DSA_VROWS), feat(IDX_W), tok(IDX_DIM, 0),
                   feat(IDX_HEADS)],
        out_shape=[jax.ShapeDtypeStruct((b, DSA_W, s), BF16),
                   jax.ShapeDtypeStruct((t, DSA_W), BF16),
                   jax.ShapeDtypeStruct((b, DSA_HEADS * DSA_VROWS, s), BF16),
                   jax.ShapeDtypeStruct((b, IDX_W, s), BF16),
                   jax.ShapeDtypeStruct((t, IDX_DIM), BF16),
                   jax.ShapeDtypeStruct((b, IDX_HEADS, s), F32)],
        compiler_params=_cparams(("arbitrary", "arbitrary"), 40),
        name="dsa_prep",
    )(p, p, p, p, g, q_g, k_g)


def _dsa_kernel(qt_ref, iqt_ref, iwt_ref, z_ref, kn_ref, vt_ref, ik_ref, bias_ref, y_ref,
                keys_scr, acc_scr, m_scr, iq2_scr, iw2_scr, s_scr, ml_scr, *, topk, idx_bits):
    Q, TK, SUB = TILE, DSA_TK, DSA_SUB
    i = pl.program_id(1)
    n_it = (i + SUB) // SUB
    qpos = i * Q + lax.broadcasted_iota(I32, (1, Q), 1)
    limit = (qpos // CHUNK + 1) * CHUNK
    krow = lax.broadcasted_iota(I32, (TK, Q), 0)

    def koff(kt):
        return pl.multiple_of(kt * TK, TK)

    for j in range(IDX_HEADS // 2):
        for e in range(2):
            h = 2 * j + e
            iq2_scr[j, :, e * Q:(e + 1) * Q] = iqt_ref[h * IDX_DIM:(h + 1) * IDX_DIM, :]
            iw2_scr[j, :, e * Q:(e + 1) * Q] = iwt_ref[h:h + 1, :]

    CR = 8 * SUBLANES

    def to_key(x):
        bits = lax.bitcast_convert_type(x, I32)
        return bits ^ ((bits >> 31) & 0x7FFFFFFF)

    def score_tile(kt, last):
        off = koff(kt)
        ikt = ik_ref[pl.ds(off, TK), :]
        sc = jnp.zeros((TK, Q), F32)
        for j in range(IDX_HEADS // 2):
            w = jnp.maximum(_dot(ikt, iq2_scr[j]), 0.0) * iw2_scr[j]
            sc = sc + (w[:, 0:Q] + w[:, Q:2 * Q])
        keys = to_key(sc)
        if last:
            keys = jnp.where(off + krow >= limit, INT_MIN, keys)
        keys_scr[pl.ds(off, TK), :] = keys

    def score_pair(jp, carry):
        score_tile(2 * jp, False)
        score_tile(2 * jp + 1, False)
        return carry

    n_full = n_it - 1
    lax.fori_loop(0, n_full // 2, score_pair, 0)

    @pl.when(n_full % 2 == 1)
    def _():
        score_tile(n_full - 1, False)

    score_tile(n_it - 1, True)

    def count(pred):
        def one(kt, acc):
            off = koff(kt)
            ind = jnp.where(pred(keys_scr[pl.ds(off, TK), :], off), 1.0, 0.0)
            return acc + jnp.sum(ind.reshape(TK // CR, CR, Q), axis=0)

        def two(jp, acc):
            return one(2 * jp + 1, one(2 * jp, acc))

        acc = lax.fori_loop(0, n_it // 2, two, jnp.zeros((CR, Q), F32))
        acc = lax.cond(n_it % 2 == 1, lambda a: one(n_it - 1, a), lambda a: a, acc)
        return jnp.sum(acc, axis=0, keepdims=True)

    take_all = limit <= topk

    def thr_bit(b, carry):
        t, n_t = carry
        cand = t + lax.shift_left(jnp.int32(1), 31 - b)
        n_c = count(lambda kk, off: kk >= cand)
        ok = n_c >= topk
        return jnp.where(ok, cand, t), jnp.where(ok, n_c, n_t)

    thr, n_ge = lax.fori_loop(0, 32, thr_bit, (jnp.full((1, Q), INT_MIN, I32), jnp.zeros((1, Q), F32)))
    thr = jnp.where(take_all, INT_MIN + 1, thr)

    @pl.when(jnp.max(n_ge) > topk)
    def _():
        need = topk - count(lambda kk, off: kk > thr)

        def idx_bit(b, j):
            cand = j + lax.shift_left(jnp.int32(1), idx_bits - 1 - b)
            n_before = count(lambda kk, off: (kk == thr) & (off + krow < cand))
            return jnp.where(n_before < need, cand, j)

        last = lax.fori_loop(0, idx_bits, idx_bit, jnp.zeros((1, Q), I32))

        def drop(kt, carry):
            off = koff(kt)
            kk = keys_scr[pl.ds(off, TK), :]
            keys_scr[pl.ds(off, TK), :] = jnp.where((kk == thr) & (off + krow > last), INT_MIN, kk)
            return carry

        lax.fori_loop(0, n_it, drop, 0)

    m_scr[...] = jnp.full(m_scr.shape, NEG_BIG, F32)
    acc_scr[...] = jnp.zeros_like(acc_scr)

    def logits(kt, buf, thr_kt):
        off = koff(kt)
        sel = keys_scr[pl.ds(off, TK), :] >= thr_kt
        bsel = [jnp.clip(i - (kt * SUB + j), 0, 2) for j in range(SUB)]
        for h in range(DSA_HEADS):
            hs = slice(h * DSA_HD, (h + 1) * DSA_HD)
            s = _dot(kn_ref[pl.ds(off, TK), hs], qt_ref[hs, :])
            s = s + jnp.concatenate([bias_ref[h, bsel[j]] for j in range(SUB)], axis=0)
            s = jnp.where(sel, s, NEG_BIG)
            s_scr[buf, h] = s
            ml_scr[buf, h] = jnp.max(s, axis=0, keepdims=True)

    def values(kt, buf):
        off = koff(kt)
        for h in range(DSA_HEADS):
            m_old = m_scr[h]
            m_new = jnp.maximum(m_old, ml_scr[buf, h])
            p = jnp.exp2(s_scr[buf, h] - m_new).astype(BF16)
            v1 = vt_ref[h * DSA_VROWS:(h + 1) * DSA_VROWS, pl.ds(off, TK)]
            acc_scr[h] = jnp.exp2(m_old - m_new) * acc_scr[h] + _dot(v1, p)
            m_scr[h] = m_new

    def att_group(first, size):
        steps = []
        for e in range(size):
            kt = first + e
            steps.append((jnp.minimum(kt, n_it - 1), jnp.where(kt < n_it, thr, jnp.int32(2 ** 31 - 1))))
        for e in range(size):
            logits(steps[e][0], e, steps[e][1])
        for e in range(size):
            values(steps[e][0], e)

    def att_big(gi, carry):
        att_group(gi * DSA_GROUP, DSA_GROUP)
        return carry

    def att_small(gi, carry):
        att_group(n_big * DSA_GROUP + 2 * gi, 2)
        return carry

    n_big = n_it // DSA_GROUP
    lax.fori_loop(0, n_big, att_big, 0)
    lax.fori_loop(0, (n_it - n_big * DSA_GROUP + 1) // 2, att_small, 0)
    for h in range(DSA_HEADS):
        hs = slice(h * DSA_HD, (h + 1) * DSA_HD)
        acc = acc_scr[h]
        out = (acc[0:DSA_HD, :] / acc[DSA_HD:DSA_HD + 1, :]).T
        y_ref[:, hs] = (out * _silu(z_ref[:, hs].astype(F32))).astype(BF16)


def _dsa(p, qt, kn, vt, iqt, ik, iwt, bias_tiles, b, s):
    assert s % DSA_TK == 0
    nt = s // TILE
    t = b * s
    topk = min(TOPK_MAX, s // 4)
    idx_bits = max(1, (s - 1).bit_length())
    kernel = functools.partial(_dsa_kernel, topk=topk, idx_bits=idx_bits)
    once = pl.Buffered(1)
    qfeat = lambda rows: pl.BlockSpec((None, rows, TILE), lambda bi, ti: (bi, 0, ti))
    return pl.pallas_call(
        kernel,
        grid=(b, nt),
        in_specs=[qfeat(DSA_W), qfeat(IDX_W), qfeat(IDX_HEADS),
                  pl.BlockSpec((TILE, DSA_W), lambda bi, ti: (bi * nt + ti, P_DSA_Z // DSA_W)),
                  pl.BlockSpec((s, DSA_W), lambda bi, ti: (bi, 0), pipeline_mode=once),
                  pl.BlockSpec((None, DSA_HEADS * DSA_VROWS, s), lambda bi, ti: (bi, 0, 0), pipeline_mode=once),
                  pl.BlockSpec((s, IDX_DIM), lambda bi, ti: (bi, 0), pipeline_mode=once),
                  pl.BlockSpec((DSA_HEADS, 3, TILE, TILE), lambda bi, ti: (0, 0, 0, 0))],
        out_specs=pl.BlockSpec((TILE, DSA_W), lambda bi, ti: (bi * nt + ti, 0)),
        out_shape=jax.ShapeDtypeStruct((t, DSA_W), BF16),
        scratch_shapes=[pltpu.VMEM((s, TILE), I32),
                        pltpu.VMEM((DSA_HEADS, DSA_VROWS, TILE), F32),
                        pltpu.VMEM((DSA_HEADS, 1, TILE), F32),
                        pltpu.VMEM((IDX_HEADS // 2, IDX_DIM, 2 * TILE), BF16),
                        pltpu.VMEM((IDX_HEADS // 2, 1, 2 * TILE), F32),
                        pltpu.VMEM((DSA_GROUP, DSA_HEADS, DSA_TK, TILE), F32),
                        pltpu.VMEM((DSA_GROUP, DSA_HEADS, 1, TILE), F32)],
        compiler_params=_cparams(("arbitrary", "arbitrary"), 48),
        name="dsa",
    )(qt, iqt, iwt, p, kn, vt, ik, bias_tiles)


def _out_proj_kernel(yml_ref, ygla_ref, ydsa_ref, w_ref, x_ref, o_ref):
    y = _dot(yml_ref[...], w_ref[0:ML_W, :])
    y = y + _dot(ygla_ref[...], w_ref[ML_W:ML_W + GLA_W, :])
    y = y + _dot(ydsa_ref[...], w_ref[ML_W + GLA_W:D_MIX, :])
    o_ref[...] = x_ref[...] + y


def _out_proj(y_ml, y_gla, y_dsa, w_out, x2d):
    t = x2d.shape[0]
    tm = min(1024, t)
    row = lambda width: pl.BlockSpec((tm, width), lambda i: (i, 0))
    return pl.pallas_call(
        _out_proj_kernel,
        grid=(t // tm,),
        in_specs=[row(ML_W), row(GLA_W), row(DSA_W),
                  pl.BlockSpec((D_MIX, D_MODEL), lambda i: (0, 0)), row(D_MODEL)],
        out_specs=row(D_MODEL),
        out_shape=jax.ShapeDtypeStruct((t, D_MODEL), F32),
        compiler_params=_cparams(("arbitrary",), 40),
        name="out_proj",
    )(y_ml, y_gla, y_dsa, w_out, x2d)


_COLUMN_LAYOUT = (
    ('ml_q', ML_W), ('ml_k', ML_W), ('ml_v', ML_W), ('ml_o', ML_W), ('ml_z', ML_W),
    ('ml_i', ML_HEADS), ('ml_f', ML_HEADS),
    ('gla_q', GLA_WK), ('gla_k', GLA_WK), ('gla_v', GLA_W), ('gla_a', GLA_RANK), ('gla_r', GLA_W),
    ('dsa_q', DSA_W), ('dsa_k', DSA_W), ('dsa_v', DSA_W), ('dsa_z', DSA_W),
    ('idx_q', IDX_W), ('idx_k', IDX_DIM), ('idx_w', IDX_HEADS),
)


def _split_w_in(w_in):
    cols, off = {}, 0
    for name, width in _COLUMN_LAYOUT:
        cols[name] = w_in[:, off:off + width]
        off += width
    return cols


def _pack_w_in(w_in):
    c = _split_w_in(w_in)
    d = w_in.shape[0]
    wbig = jnp.concatenate([c['ml_q'], c['ml_k'], c['ml_v'], c['ml_o'], c['ml_z'], c['idx_q'],
                            c['gla_q'], c['gla_k'], c['gla_v'], c['gla_r'],
                            c['dsa_q'], c['dsa_k'], c['dsa_v'], c['dsa_z']], axis=1)
    assert wbig.shape[1] == P_COLS
    z = lambda n: jnp.zeros((d, n), w_in.dtype)
    wsmall = jnp.concatenate([c['idx_k'], c['ml_i'], z(4), c['ml_f'], z(4), c['gla_a'], c['idx_w'],
                              z(LANES - G_IDXW - IDX_HEADS)], axis=1)
    assert wsmall.shape[1] == LANES
    return wbig.astype(BF16), wsmall.astype(BF16)


def _rel_bucket(rel):
    half = REL_BUCKETS // 2
    max_exact = half // 2
    ret = jnp.where(rel > 0, half, 0)
    n = jnp.abs(rel)
    nf = jnp.maximum(n, 1).astype(F32)
    large = max_exact + (jnp.log(nf / max_exact) / math.log(REL_MAX_DIST / max_exact)
                         * (half - max_exact)).astype(I32)
    large = jnp.minimum(large, half - 1)
    return ret + jnp.where(n < max_exact, n, large)


def _bias_tiles(rel_bias):
    assert REL_MAX_DIST <= TILE
    kl = jnp.arange(TILE, dtype=I32)[:, None]
    ql = jnp.arange(TILE, dtype=I32)[None, :]
    far = jnp.full((TILE, TILE), -(TILE + 1), I32)
    rel = jnp.stack([kl - ql, kl - TILE - ql, far])
    tab = jnp.transpose(rel_bias.astype(F32)[_rel_bucket(rel)], (3, 0, 1, 2))
    return (tab - tab[:, 2:3]) * LOG2E


def _layer(x2d, b, s, norm_g, w_in, ml_conv_w, ml_conv_b, ml_b_i, ml_b_f, ml_norm_g,
           gla_w_a, gla_b_a, gla_norm_g, dsa_q_g, dsa_k_g, w_out, bias_tiles):
    wbig, wsmall = _pack_w_in(w_in)
    p, g = _in_proj(x2d, norm_g.reshape(1, D_MODEL), wbig, wsmall)

    gbias = jnp.zeros((1, LANES), F32)
    gbias = gbias.at[0, G_MLI:G_MLI + ML_HEADS].set(ml_b_i).at[0, G_MLF:G_MLF + ML_HEADS].set(ml_b_f)
    y_ml = _mlstm(p, g, gbias, ml_conv_w, ml_conv_b.reshape(1, -1), ml_norm_g.reshape(1, -1), b, s)

    wa_pad = jnp.zeros((LANES, GLA_WK), F32).at[G_GLAA:G_GLAA + GLA_RANK].set(gla_w_a).astype(BF16)
    y_gla = _gla(p, g, wa_pad, gla_b_a.reshape(1, -1), gla_norm_g.reshape(1, -1), b, s)

    qt, kn, vt, iqt, ik, iwt = _dsa_prep(p, g, dsa_q_g.reshape(1, -1), dsa_k_g.reshape(1, -1), b, s)
    y_dsa = _dsa(p, qt, kn, vt, iqt, ik, iwt, bias_tiles, b, s)

    return _out_proj(y_ml, y_gla, y_dsa, w_out.astype(BF16), x2d)


def kernel(x, norm_g, w_in, ml_conv_w, ml_conv_b, ml_b_i, ml_b_f, ml_norm_g, gla_w_a, gla_b_a, gla_norm_g,
           dsa_q_g, dsa_k_g, w_out, rel_bias):
    b, s, d = x.shape
    assert d == D_MODEL and s % TILE == 0
    depth = norm_g.shape[0]
    bias_tiles = _bias_tiles(rel_bias)
    x2d = x.reshape(b * s, d)
    for layer in range(depth):
        x2d = _layer(x2d, b, s, norm_g[layer], w_in[layer], ml_conv_w[layer], ml_conv_b[layer],
                     ml_b_i[layer], ml_b_f[layer], ml_norm_g[layer], gla_w_a[layer], gla_b_a[layer],
                     gla_norm_g[layer], dsa_q_g[layer], dsa_k_g[layer], w_out[layer], bias_tiles)
    return x2d.reshape(b, s, d)
```

```python
import functools
import math

import jax
import jax.numpy as jnp
from jax import lax
from jax.experimental import pallas as pl
from jax.experimental.pallas import tpu as pltpu

F32 = jnp.float32
BF16 = jnp.bfloat16
I32 = jnp.int32

D_MODEL = 1024
CHUNK = 64
EPS = 1e-6
ML_HEADS, ML_HD = 4, 256
ML_W = ML_HEADS * ML_HD
ML_CONV = 4
GLA_HEADS, GLA_DK, GLA_DV = 4, 64, 128
GLA_WK = GLA_HEADS * GLA_DK
GLA_W = GLA_HEADS * GLA_DV
GLA_RANK = 16
GLA_TAU = 16.0
DSA_HEADS, DSA_HD = 4, 128
DSA_W = DSA_HEADS * DSA_HD
IDX_HEADS, IDX_DIM = 8, 64
IDX_W = IDX_HEADS * IDX_DIM
TOPK_MAX = 256
REL_BUCKETS, REL_MAX_DIST = 32, 128
D_MIX = ML_W + GLA_W + DSA_W

LANES = 128
SUBLANES = 8

TILE = 128
DSA_TK = 512
DSA_SUB = DSA_TK // TILE
DSA_GROUP = 4
DSA_VROWS = DSA_HD + 16
LOG2E = 1.4426950408889634

P_ML = 0
P_IDXQ = 5 * ML_W
P_GLA_Q = P_IDXQ + IDX_W
P_GLA_K = P_GLA_Q + GLA_WK
P_GLA_V = P_GLA_K + GLA_WK
P_GLA_R = P_GLA_V + GLA_W
P_DSA_Q = P_GLA_R + GLA_W
P_DSA_K = P_DSA_Q + DSA_W
P_DSA_V = P_DSA_K + DSA_W
P_DSA_Z = P_DSA_V + DSA_W
P_COLS = P_DSA_Z + DSA_W
G_IDXK = 0
G_MLI = 64
G_MLF = 72
G_GLAA = 80
G_IDXW = 96

INT_MIN = -(2 ** 31)
NEG_BIG = -1e30


def _cparams(sem, vmem_mb):
    return pltpu.CompilerParams(dimension_semantics=sem, vmem_limit_bytes=vmem_mb << 20)


def _dot(a, b):
    return jnp.dot(a, b, preferred_element_type=F32)


def _dot_nt(a, b):
    return lax.dot_general(a, b, (((1,), (1,)), ((), ())), preferred_element_type=F32)


def _split3(x):
    hi = x.astype(BF16)
    r1 = x - hi.astype(F32)
    mid = r1.astype(BF16)
    lo = (r1 - mid.astype(F32)).astype(BF16)
    return hi, mid, lo


def _dot_f32_lhs(x, m01):
    hi, mid, lo = _split3(x)
    return _dot(hi, m01) + _dot(mid, m01) + _dot(lo, m01)


def _dot_f32_rhs(m01, x):
    hi, mid, lo = _split3(x)
    return _dot(m01, hi) + _dot(m01, mid) + _dot(m01, lo)


def _log_sigmoid(x):
    return jnp.minimum(x, 0.0) - jnp.log(1.0 + jnp.exp(-jnp.abs(x)))


def _sigmoid(x):
    return 1.0 / (1.0 + jnp.exp(-x))


def _silu(x):
    return x * _sigmoid(x)


def _in_proj_kernel(x_ref, g_ref, wb_ref, ws_ref, p_ref, gate_ref, xn_ref):
    @pl.when(pl.program_id(1) == 0)
    def _():
        x = x_ref[...]
        ms = jnp.mean(x * x, axis=-1, keepdims=True)
        xn = (x * lax.rsqrt(ms + EPS) * g_ref[...]).astype(BF16)
        xn_ref[...] = xn
        gate_ref[...] = _dot(xn, ws_ref[...])

    p_ref[...] = _dot(xn_ref[...], wb_ref[...]).astype(BF16)


def _in_proj(x2d, g, wbig, wsmall):
    t = x2d.shape[0]
    tm = min(1024, t)
    tn = P_COLS // 4
    return pl.pallas_call(
        _in_proj_kernel,
        grid=(t // tm, P_COLS // tn),
        in_specs=[
            pl.BlockSpec((tm, D_MODEL), lambda i, j: (i, 0)),
            pl.BlockSpec((1, D_MODEL), lambda i, j: (0, 0)),
            pl.BlockSpec((D_MODEL, tn), lambda i, j: (0, j)),
            pl.BlockSpec((D_MODEL, LANES), lambda i, j: (0, 0)),
        ],
        out_specs=[
            pl.BlockSpec((tm, tn), lambda i, j: (i, j)),
            pl.BlockSpec((tm, LANES), lambda i, j: (i, 0)),
        ],
        out_shape=[jax.ShapeDtypeStruct((t, P_COLS), BF16), jax.ShapeDtypeStruct((t, LANES), F32)],
        scratch_shapes=[pltpu.VMEM((tm, D_MODEL), BF16)],
        compiler_params=_cparams(("arbitrary", "arbitrary"), 56),
        name="in_proj",
    )(x2d, g, wbig, wsmall)


def _mlstm_kernel(q_ref, k_ref, v_ref, o_ref, z_ref, g_ref, gb_ref, cw_ref, cb_ref, ng_ref,
                  y_ref, c_scr, n_scr, m_scr, xbuf):
    L = TILE

    @pl.when(pl.program_id(1) == 0)
    def _():
        c_scr[...] = jnp.zeros_like(c_scr)
        n_scr[...] = jnp.zeros_like(n_scr)
        m_scr[...] = jnp.zeros_like(m_scr)
        xbuf[0:SUBLANES, :] = jnp.zeros((SUBLANES, 2 * ML_W), F32)

    xbuf[SUBLANES:SUBLANES + L, 0:ML_W] = q_ref[...].astype(F32)
    xbuf[SUBLANES:SUBLANES + L, ML_W:2 * ML_W] = k_ref[...].astype(F32)
    conv = jnp.broadcast_to(cb_ref[...], (L, 2 * ML_W))
    for j in range(ML_CONV):
        off = SUBLANES - (ML_CONV - 1) + j
        conv = conv + cw_ref[j:j + 1, :] * xbuf[off:off + L, :]
    xbuf[0:SUBLANES, :] = xbuf[L:L + SUBLANES, :]
    qk = _silu(conv)

    gate = g_ref[...] + gb_ref[...]
    lf_cols = _log_sigmoid(gate)
    gate_t = gate.T
    li_rows = gate_t[G_MLI:G_MLI + SUBLANES, :]
    lf_rows = _log_sigmoid(gate_t[G_MLF:G_MLF + SUBLANES, :])
    r_io = lax.broadcasted_iota(I32, (L, L), 0)
    c_io = lax.broadcasted_iota(I32, (L, L), 1)
    causal = c_io <= r_io
    tri_u = (r_io <= c_io).astype(BF16)
    tri_l = causal.astype(BF16)
    bc_rows = _dot_f32_lhs(lf_rows, tri_u)
    bc_cols = _dot_f32_rhs(tri_l, lf_cols)

    for h in range(ML_HEADS):
        hs = slice(h * ML_HD, (h + 1) * ML_HD)
        qh = qk[:, hs]
        kh = qk[:, ML_W + h * ML_HD:ML_W + (h + 1) * ML_HD] * (ML_HD ** -0.5)
        vh = v_ref[:, hs]
        qb = qh.astype(BF16)
        kb = kh.astype(BF16)
        bc_col = bc_cols[:, G_MLF + h:G_MLF + h + 1]
        bc_row = bc_rows[h:h + 1, :]
        li_row = li_rows[h:h + 1, :]
        m_prev = m_scr[h, :, 0:1]

        dmat = jnp.where(causal, bc_col - (bc_row - li_row), -jnp.inf)
        inter = bc_col + m_prev
        m_row = jnp.maximum(inter, jnp.max(dmat, axis=1, keepdims=True))
        w_intra = jnp.exp(dmat - m_row)
        w_inter = jnp.exp(inter - m_row)
        s_qk = _dot_nt(qb, kb) * w_intra
        num = _dot(s_qk.astype(BF16), vh) + w_inter * _dot(qb, c_scr[h].astype(BF16))
        qn = jnp.sum(qh * n_scr[h], axis=1, keepdims=True)
        den = jnp.sum(s_qk, axis=1, keepdims=True) + w_inter * qn
        hh = num / jnp.maximum(jnp.abs(den), jnp.exp(-m_row))

        g_tot = bc_row[:, L - 1:L]
        a_row = g_tot - bc_row + li_row
        m_new = jnp.maximum(g_tot + m_prev, jnp.max(a_row, axis=1, keepdims=True))
        decay = jnp.exp(g_tot + m_prev - m_new)
        wa_row = jnp.exp(a_row - m_new)
        ktw = (kh.T * wa_row).astype(BF16)
        c_scr[h] = decay * c_scr[h] + _dot(ktw, vh)
        wa8 = jnp.broadcast_to(wa_row, (SUBLANES, L)).astype(BF16)
        n_scr[h] = decay * n_scr[h] + _dot(wa8, kb)[0:1, :]
        m_scr[h] = jnp.broadcast_to(m_new, (1, LANES))

        hm = hh * _sigmoid(o_ref[:, hs].astype(F32))
        hm = hm * lax.rsqrt(jnp.mean(hm * hm, axis=1, keepdims=True) + EPS) * ng_ref[:, hs]
        y_ref[:, hs] = (hm * _silu(z_ref[:, hs].astype(F32))).astype(BF16)


def _mlstm(p, g, gbias, conv_w, conv_b, norm_g, b, s):
    nt = s // TILE
    t = b * s

    def pcol(c):
        return pl.BlockSpec((TILE, ML_W), lambda bi, ti, c=c: (bi * nt + ti, c))

    full = lambda shape: pl.BlockSpec(shape, lambda bi, ti: (0,) * len(shape))
    return pl.pallas_call(
        _mlstm_kernel,
        grid=(b, nt),
        in_specs=[pcol(0), pcol(1), pcol(2), pcol(3), pcol(4),
                  pl.BlockSpec((TILE, LANES), lambda bi, ti: (bi * nt + ti, 0)),
                  full((1, LANES)), full((ML_CONV, 2 * ML_W)), full((1, 2 * ML_W)), full((1, ML_W))],
        out_specs=pl.BlockSpec((TILE, ML_W), lambda bi, ti: (bi * nt + ti, 0)),
        out_shape=jax.ShapeDtypeStruct((t, ML_W), BF16),
        scratch_shapes=[pltpu.VMEM((ML_HEADS, ML_HD, ML_HD), F32),
                        pltpu.VMEM((ML_HEADS, 1, ML_HD), F32),
                        pltpu.VMEM((ML_HEADS, 1, LANES), F32),
                        pltpu.VMEM((TILE + SUBLANES, 2 * ML_W), F32)],
        compiler_params=_cparams(("arbitrary", "arbitrary"), 40),
        name="mlstm",
    )(p, p, p, p, p, g, gbias, conv_w, conv_b, norm_g)


def _gla_kernel(q_ref, k_ref, v_ref, r_ref, g_ref, wa_ref, ba_ref, ng_ref, y_ref, st_scr):
    L = TILE
    H = GLA_HEADS

    @pl.when(pl.program_id(1) == 0)
    def _():
        st_scr[...] = jnp.zeros_like(st_scr)

    la = _log_sigmoid(_dot(g_ref[...].astype(BF16), wa_ref[...]) + ba_ref[...]) * (1.0 / GLA_TAU)
    r_io = lax.broadcasted_iota(I32, (L, L), 0)
    c_io = lax.broadcasted_iota(I32, (L, L), 1)
    tri_l = (c_io <= r_io).astype(BF16)
    bcum = _dot_f32_rhs(tri_l, la)
    btot = bcum[L - 1:L, :]
    bmid = 0.5 * btot

    q = q_ref[...].astype(F32) * (GLA_DK ** -0.5)
    k = k_ref[...].astype(F32)
    v = v_ref[...]
    q_in = (q * jnp.exp(bcum - bmid)).astype(BF16)
    k_in = k * jnp.exp(bmid - bcum)
    q_st = (q * jnp.exp(bcum)).astype(BF16)
    k_st = (k * jnp.exp(btot - bcum)).astype(BF16)

    kt = k_in.T
    kt4 = jnp.concatenate([kt] * H, axis=1)
    rr = lax.broadcasted_iota(I32, (GLA_WK, H * L), 0)
    cc = lax.broadcasted_iota(I32, (GLA_WK, H * L), 1)
    k_bd = jnp.where(rr // GLA_DK == cc // L, kt4, 0.0).astype(BF16)
    att = _dot(q_in, k_bd)
    ar = lax.broadcasted_iota(I32, (L, H * L), 0)
    ac = lax.broadcasted_iota(I32, (L, H * L), 1)
    att = jnp.where(ac % L <= ar, att, 0.0).astype(BF16)
    v4 = jnp.concatenate([v] * H, axis=0)
    vr = lax.broadcasted_iota(I32, (H * L, GLA_W), 0)
    vc = lax.broadcasted_iota(I32, (H * L, GLA_W), 1)
    v_bd = jnp.where(vr // L == vc // GLA_DV, v4, jnp.zeros_like(v4))
    o = _dot(att, v_bd) + _dot_nt(q_st, st_scr[...].astype(BF16))

    sr = lax.broadcasted_iota(I32, (GLA_W, GLA_WK), 0)
    sc = lax.broadcasted_iota(I32, (GLA_W, GLA_WK), 1)
    upd = _dot(v.astype(F32).T.astype(BF16), k_st)
    st_scr[...] = st_scr[...] * jnp.exp(btot) + jnp.where(sr // GLA_DV == sc // GLA_DK, upd, 0.0)

    for h in range(H):
        hs = slice(h * GLA_DV, (h + 1) * GLA_DV)
        oh = o[:, hs]
        oh = oh * lax.rsqrt(jnp.mean(oh * oh, axis=1, keepdims=True) + EPS) * ng_ref[:, hs]
        y_ref[:, hs] = (oh * _silu(r_ref[:, hs].astype(F32))).astype(BF16)


def _gla(p, g, wa_pad, b_a, norm_g, b, s):
    nt = s // TILE
    t = b * s

    def pcol(width, off):
        assert off % width == 0
        return pl.BlockSpec((TILE, width), lambda bi, ti: (bi * nt + ti, off // width))

    full = lambda shape: pl.BlockSpec(shape, lambda bi, ti: (0,) * len(shape))
    return pl.pallas_call(
        _gla_kernel,
        grid=(b, nt),
        in_specs=[pcol(GLA_WK, P_GLA_Q), pcol(GLA_WK, P_GLA_K), pcol(GLA_W, P_GLA_V), pcol(GLA_W, P_GLA_R),
                  pl.BlockSpec((TILE, LANES), lambda bi, ti: (bi * nt + ti, 0)),
                  full((LANES, GLA_WK)), full((1, GLA_WK)), full((1, GLA_W))],
        out_specs=pl.BlockSpec((TILE, GLA_W), lambda bi, ti: (bi * nt + ti, 0)),
        out_shape=jax.ShapeDtypeStruct((t, GLA_W), BF16),
        scratch_shapes=[pltpu.VMEM((GLA_W, GLA_WK), F32)],
        compiler_params=_cparams(("arbitrary", "arbitrary"), 32),
        name="gla",
    )(p, p, p, p, g, wa_pad, b_a, norm_g)


def _dsa_prep_kernel(q_ref, k_ref, v_ref, iq_ref, g_ref, qg_ref, kg_ref,
                     qt_ref, kn_ref, vt_ref, iqt_ref, ik_ref, iwt_ref):
    qn = []
    for h in range(DSA_HEADS):
        hs = slice(h * DSA_HD, (h + 1) * DSA_HD)
        q = q_ref[:, hs].astype(F32)
        k = k_ref[:, hs].astype(F32)
        qh = q * lax.rsqrt(jnp.mean(q * q, axis=1, keepdims=True) + EPS) * qg_ref[...]
        kh = k * lax.rsqrt(jnp.mean(k * k, axis=1, keepdims=True) + EPS) * kg_ref[...]
        qn.append(qh * (DSA_HD ** -0.5 * LOG2E))
        kn_ref[:, hs] = kh.astype(BF16)
        r0 = h * DSA_VROWS
        vt_ref[r0:r0 + DSA_HD, :] = v_ref[:, hs].astype(F32).T.astype(BF16)
        vt_ref[r0 + DSA_HD:r0 + DSA_VROWS, :] = jnp.ones((DSA_VROWS - DSA_HD, vt_ref.shape[1]), BF16)
    qt_ref[...] = jnp.concatenate(qn, axis=1).T.astype(BF16)
    iqt_ref[...] = iq_ref[...].astype(F32).T.astype(BF16)
    g = g_ref[...]
    ik_ref[...] = g[:, G_IDXK:G_IDXK + IDX_DIM].astype(BF16)
    iw_scale = (IDX_HEADS ** -0.5) * (IDX_DIM ** -0.5)
    iwt_ref[...] = g.T[G_IDXW:G_IDXW + IDX_HEADS, :] * iw_scale


def _dsa_prep(p, g, q_g, k_g, b, s):
    t = b * s
    tm = min(512, s)
    nt = s // tm
    full = lambda shape: pl.BlockSpec(shape, lambda bi, ti: (0,) * len(shape))
    tok = lambda width, cb: pl.BlockSpec((tm, width), lambda bi, ti: (bi * nt + ti, cb))
    feat = lambda rows: pl.BlockSpec((None, rows, tm), lambda bi, ti: (bi, 0, ti))
    return pl.pallas_call(
        _dsa_prep_kernel,
        grid=(b, nt),
        in_specs=[tok(DSA_W, P_DSA_Q // DSA_W), tok(DSA_W, P_DSA_K // DSA_W), tok(DSA_W, P_DSA_V // DSA_W),
                  tok(IDX_W, P_IDXQ // IDX_W), tok(LANES, 0), full((1, DSA_HD)), full((1, DSA_HD))],
        out_specs=[feat(DSA_W), tok(DSA_W, 0), feat(DSA_HEADS * DSA_VROWS), feat(IDX_W), tok(IDX_DIM, 0),
                   feat(IDX_HEADS)],
        out_shape=[jax.ShapeDtypeStruct((b, DSA_W, s), BF16),
                   jax.ShapeDtypeStruct((t, DSA_W), BF16),
                   jax.ShapeDtypeStruct((b, DSA_HEADS * DSA_VROWS, s), BF16),
                   jax.ShapeDtypeStruct((b, IDX_W, s), BF16),
                   jax.ShapeDtypeStruct((t, IDX_DIM), BF16),
                   jax.ShapeDtypeStruct((b, IDX_HEADS, s), F32)],
        compiler_params=_cparams(("arbitrary", "arbitrary"), 40),
        name="dsa_prep",
    )(p, p, p, p, g, q_g, k_g)


def _dsa_kernel(qt_ref, iqt_ref, iwt_ref, z_ref, kn_ref, vt_ref, ik_ref, bias_ref, y_ref,
                keys_scr, acc_scr, m_scr, iq2_scr, iw2_scr, s_scr, ml_scr, *, topk, idx_bits):
    Q, TK, SUB = TILE, DSA_TK, DSA_SUB
    i = pl.program_id(1)
    n_it = (i + SUB) // SUB
    qpos = i * Q + lax.broadcasted_iota(I32, (1, Q), 1)
    limit = (qpos // CHUNK + 1) * CHUNK
    krow = lax.broadcasted_iota(I32, (TK, Q), 0)

    def koff(kt):
        return pl.multiple_of(kt * TK, TK)

    for j in range(IDX_HEADS // 2):
        for e in range(2):
            h = 2 * j + e
            iq2_scr[j, :, e * Q:(e + 1) * Q] = iqt_ref[h * IDX_DIM:(h + 1) * IDX_DIM, :]
            iw2_scr[j, :, e * Q:(e + 1) * Q] = iwt_ref[h:h + 1, :]

    CR = 8 * SUBLANES

    def to_key(x):
        bits = lax.bitcast_convert_type(x, I32)
        return bits ^ ((bits >> 31) & 0x7FFFFFFF)

    def score_tile(kt, last):
        off = koff(kt)
        ikt = ik_ref[pl.ds(off, TK), :]
        sc = jnp.zeros((TK, Q), F32)
        for j in range(IDX_HEADS // 2):
            w = jnp.maximum(_dot(ikt, iq2_scr[j]), 0.0) * iw2_scr[j]
            sc = sc + (w[:, 0:Q] + w[:, Q:2 * Q])
        keys = to_key(sc)
        if last:
            keys = jnp.where(off + krow >= limit, INT_MIN, keys)
        keys_scr[pl.ds(off, TK), :] = keys

    def score_run(first, size):
        for e in range(size):
            score_tile(first + e, False)

    def score_quad(jq, carry):
        score_run(4 * jq, 4)
        return carry

    n_full = n_it - 1
    n_quad = n_full // 4
    lax.fori_loop(0, n_quad, score_quad, 0)

    @pl.when(n_full % 4 >= 2)
    def _():
        score_run(4 * n_quad, 2)

    @pl.when(n_full % 2 == 1)
    def _():
        score_tile(n_full - 1, False)

    score_tile(n_it - 1, True)

    def count(pred):
        def one(kt, acc):
            off = koff(kt)
            ind = jnp.where(pred(keys_scr[pl.ds(off, TK), :], off), 1.0, 0.0)
            return acc + jnp.sum(ind.reshape(TK // CR, CR, Q), axis=0)

        def two(jp, acc):
            return one(2 * jp + 1, one(2 * jp, acc))

        acc = lax.fori_loop(0, n_it // 2, two, jnp.zeros((CR, Q), F32))
        acc = lax.cond(n_it % 2 == 1, lambda a: one(n_it - 1, a), lambda a: a, acc)
        return jnp.sum(acc, axis=0, keepdims=True)

    take_all = limit <= topk

    def thr_bit(b, carry):
        t, n_t = carry
        cand = t + lax.shift_left(jnp.int32(1), 31 - b)
        n_c = count(lambda kk, off: kk >= cand)
        ok = n_c >= topk
        return jnp.where(ok, cand, t), jnp.where(ok, n_c, n_t)

    thr, n_ge = lax.fori_loop(0, 32, thr_bit, (jnp.full((1, Q), INT_MIN, I32), jnp.zeros((1, Q), F32)))
    thr = jnp.where(take_all, INT_MIN + 1, thr)

    @pl.when(jnp.max(n_ge) > topk)
    def _():
        need = topk - count(lambda kk, off: kk > thr)

        def idx_bit(b, j):
            cand = j + lax.shift_left(jnp.int32(1), idx_bits - 1 - b)
            n_before = count(lambda kk, off: (kk == thr) & (off + krow < cand))
            return jnp.where(n_before < need, cand, j)

        last = lax.fori_loop(0, idx_bits, idx_bit, jnp.zeros((1, Q), I32))

        def drop(kt, carry):
            off = koff(kt)
            kk = keys_scr[pl.ds(off, TK), :]
            keys_scr[pl.ds(off, TK), :] = jnp.where((kk == thr) & (off + krow > last), INT_MIN, kk)
            return carry

        lax.fori_loop(0, n_it, drop, 0)

    m_scr[...] = jnp.full(m_scr.shape, NEG_BIG, F32)
    acc_scr[...] = jnp.zeros_like(acc_scr)

    def logits(kt, buf, thr_kt):
        off = koff(kt)
        sel = keys_scr[pl.ds(off, TK), :] >= thr_kt
        bsel = [jnp.clip(i - (kt * SUB + j), 0, 2) for j in range(SUB)]
        for h in range(DSA_HEADS):
            hs = slice(h * DSA_HD, (h + 1) * DSA_HD)
            s = _dot(kn_ref[pl.ds(off, TK), hs], qt_ref[hs, :])
            s = s + jnp.concatenate([bias_ref[h, bsel[j]] for j in range(SUB)], axis=0)
            s = jnp.where(sel, s, NEG_BIG)
            s_scr[buf, h] = s
            ml_scr[buf, h] = jnp.max(s, axis=0, keepdims=True)

    def values(kt, buf):
        off = koff(kt)
        for h in range(DSA_HEADS):
            m_old = m_scr[h]
            m_new = jnp.maximum(m_old, ml_scr[buf, h])
            p = jnp.exp2(s_scr[buf, h] - m_new).astype(BF16)
            v1 = vt_ref[h * DSA_VROWS:(h + 1) * DSA_VROWS, pl.ds(off, TK)]
            acc_scr[h] = jnp.exp2(m_old - m_new) * acc_scr[h] + _dot(v1, p)
            m_scr[h] = m_new

    def att_group(first, size):
        steps = []
        for e in range(size):
            kt = first + e
            steps.append((jnp.minimum(kt, n_it - 1), jnp.where(kt < n_it, thr, jnp.int32(2 ** 31 - 1))))
        for e in range(size):
            logits(steps[e][0], e, steps[e][1])
        for e in range(size):
            values(steps[e][0], e)

    def att_big(gi, carry):
        att_group(gi * DSA_GROUP, DSA_GROUP)
        return carry

    def att_small(gi, carry):
        att_group(n_big * DSA_GROUP + 2 * gi, 2)
        return carry

    n_big = n_it // DSA_GROUP
    lax.fori_loop(0, n_big, att_big, 0)
    lax.fori_loop(0, (n_it - n_big * DSA_GROUP + 1) // 2, att_small, 0)
    for h in range(DSA_HEADS):
        hs = slice(h * DSA_HD, (h + 1) * DSA_HD)
        acc = acc_scr[h]
        out = (acc[0:DSA_HD, :] / acc[DSA_HD:DSA_HD + 1, :]).T
        y_ref[:, hs] = (out * _silu(z_ref[:, hs].astype(F32))).astype(BF16)


def _dsa(p, qt, kn, vt, iqt, ik, iwt, bias_tiles, b, s):
    assert s % DSA_TK == 0
    nt = s // TILE
    t = b * s
    topk = min(TOPK_MAX, s // 4)
    idx_bits = max(1, (s - 1).bit_length())
    kernel = functools.partial(_dsa_kernel, topk=topk, idx_bits=idx_bits)
    once = pl.Buffered(1)
    qfeat = lambda rows: pl.BlockSpec((None, rows, TILE), lambda bi, ti: (bi, 0, ti))
    return pl.pallas_call(
        kernel,
        grid=(b, nt),
        in_specs=[qfeat(DSA_W), qfeat(IDX_W), qfeat(IDX_HEADS),
                  pl.BlockSpec((TILE, DSA_W), lambda bi, ti: (bi * nt + ti, P_DSA_Z // DSA_W)),
                  pl.BlockSpec((s, DSA_W), lambda bi, ti: (bi, 0), pipeline_mode=once),
                  pl.BlockSpec((None, DSA_HEADS * DSA_VROWS, s), lambda bi, ti: (bi, 0, 0), pipeline_mode=once),
                  pl.BlockSpec((s, IDX_DIM), lambda bi, ti: (bi, 0), pipeline_mode=once),
                  pl.BlockSpec((DSA_HEADS, 3, TILE, TILE), lambda bi, ti: (0, 0, 0, 0))],
        out_specs=pl.BlockSpec((TILE, DSA_W), lambda bi, ti: (bi * nt + ti, 0)),
        out_shape=jax.ShapeDtypeStruct((t, DSA_W), BF16),
        scratch_shapes=[pltpu.VMEM((s, TILE), I32),
                        pltpu.VMEM((DSA_HEADS, DSA_VROWS, TILE), F32),
                        pltpu.VMEM((DSA_HEADS, 1, TILE), F32),
                        pltpu.VMEM((IDX_HEADS // 2, IDX_DIM, 2 * TILE), BF16),
                        pltpu.VMEM((IDX_HEADS // 2, 1, 2 * TILE), F32),
                        pltpu.VMEM((DSA_GROUP, DSA_HEADS, DSA_TK, TILE), F32),
                        pltpu.VMEM((DSA_GROUP, DSA_HEADS, 1, TILE), F32)],
        compiler_params=_cparams(("arbitrary", "arbitrary"), 48),
        name="dsa",
    )(qt, iqt, iwt, p, kn, vt, ik, bias_tiles)


def _out_proj_kernel(yml_ref, ygla_ref, ydsa_ref, w_ref, x_ref, o_ref):
    y = _dot(yml_ref[...], w_ref[0:ML_W, :])
    y = y + _dot(ygla_ref[...], w_ref[ML_W:ML_W + GLA_W, :])
    y = y + _dot(ydsa_ref[...], w_ref[ML_W + GLA_W:D_MIX, :])
    o_ref[...] = x_ref[...] + y


def _out_proj(y_ml, y_gla, y_dsa, w_out, x2d):
    t = x2d.shape[0]
    tm = min(1024, t)
    row = lambda width: pl.BlockSpec((tm, width), lambda i: (i, 0))
    return pl.pallas_call(
        _out_proj_kernel,
        grid=(t // tm,),
        in_specs=[row(ML_W), row(GLA_W), row(DSA_W),
                  pl.BlockSpec((D_MIX, D_MODEL), lambda i: (0, 0)), row(D_MODEL)],
        out_specs=row(D_MODEL),
        out_shape=jax.ShapeDtypeStruct((t, D_MODEL), F32),
        compiler_params=_cparams(("arbitrary",), 40),
        name="out_proj",
    )(y_ml, y_gla, y_dsa, w_out, x2d)


_COLUMN_LAYOUT = (
    ('ml_q', ML_W), ('ml_k', ML_W), ('ml_v', ML_W), ('ml_o', ML_W), ('ml_z', ML_W),
    ('ml_i', ML_HEADS), ('ml_f', ML_HEADS),
    ('gla_q', GLA_WK), ('gla_k', GLA_WK), ('gla_v', GLA_W), ('gla_a', GLA_RANK), ('gla_r', GLA_W),
    ('dsa_q', DSA_W), ('dsa_k', DSA_W), ('dsa_v', DSA_W), ('dsa_z', DSA_W),
    ('idx_q', IDX_W), ('idx_k', IDX_DIM), ('idx_w', IDX_HEADS),
)


def _split_w_in(w_in):
    cols, off = {}, 0
    for name, width in _COLUMN_LAYOUT:
        cols[name] = w_in[:, off:off + width]
        off += width
    return cols


def _pack_w_in(w_in):
    c = _split_w_in(w_in)
    d = w_in.shape[0]
    wbig = jnp.concatenate([c['ml_q'], c['ml_k'], c['ml_v'], c['ml_o'], c['ml_z'], c['idx_q'],
                            c['gla_q'], c['gla_k'], c['gla_v'], c['gla_r'],
                            c['dsa_q'], c['dsa_k'], c['dsa_v'], c['dsa_z']], axis=1)
    assert wbig.shape[1] == P_COLS
    z = lambda n: jnp.zeros((d, n), w_in.dtype)
    wsmall = jnp.concatenate([c['idx_k'], c['ml_i'], z(4), c['ml_f'], z(4), c['gla_a'], c['idx_w'],
                              z(LANES - G_IDXW - IDX_HEADS)], axis=1)
    assert wsmall.shape[1] == LANES
    return wbig.astype(BF16), wsmall.astype(BF16)


def _rel_bucket(rel):
    half = REL_BUCKETS // 2
    max_exact = half // 2
    ret = jnp.where(rel > 0, half, 0)
    n = jnp.abs(rel)
    nf = jnp.maximum(n, 1).astype(F32)
    large = max_exact + (jnp.log(nf / max_exact) / math.log(REL_MAX_DIST / max_exact)
                         * (half - max_exact)).astype(I32)
    large = jnp.minimum(large, half - 1)
    return ret + jnp.where(n < max_exact, n, large)


def _bias_tiles(rel_bias):
    assert REL_MAX_DIST <= TILE
    kl = jnp.arange(TILE, dtype=I32)[:, None]
    ql = jnp.arange(TILE, dtype=I32)[None, :]
    far = jnp.full((TILE, TILE), -(TILE + 1), I32)
    rel = jnp.stack([kl - ql, kl - TILE - ql, far])
    hit = _rel_bucket(rel)[..., None] == jnp.arange(REL_BUCKETS, dtype=I32)
    tab = jnp.sum(jnp.where(hit[..., None], rel_bias.astype(F32), 0.0), axis=3)
    tab = jnp.transpose(tab, (3, 0, 1, 2))
    return (tab - tab[:, 2:3]) * LOG2E


def _layer(x2d, b, s, norm_g, w_in, ml_conv_w, ml_conv_b, ml_b_i, ml_b_f, ml_norm_g,
           gla_w_a, gla_b_a, gla_norm_g, dsa_q_g, dsa_k_g, w_out, bias_tiles):
    wbig, wsmall = _pack_w_in(w_in)
    p, g = _in_proj(x2d, norm_g.reshape(1, D_MODEL), wbig, wsmall)

    gbias = jnp.zeros((1, LANES), F32)
    gbias = gbias.at[0, G_MLI:G_MLI + ML_HEADS].set(ml_b_i).at[0, G_MLF:G_MLF + ML_HEADS].set(ml_b_f)
    y_ml = _mlstm(p, g, gbias, ml_conv_w, ml_conv_b.reshape(1, -1), ml_norm_g.reshape(1, -1), b, s)

    wa_pad = jnp.zeros((LANES, GLA_WK), F32).at[G_GLAA:G_GLAA + GLA_RANK].set(gla_w_a).astype(BF16)
    y_gla = _gla(p, g, wa_pad, gla_b_a.reshape(1, -1), gla_norm_g.reshape(1, -1), b, s)

    qt, kn, vt, iqt, ik, iwt = _dsa_prep(p, g, dsa_q_g.reshape(1, -1), dsa_k_g.reshape(1, -1), b, s)
    y_dsa = _dsa(p, qt, kn, vt, iqt, ik, iwt, bias_tiles, b, s)

    return _out_proj(y_ml, y_gla, y_dsa, w_out.astype(BF16), x2d)


def kernel(x, norm_g, w_in, ml_conv_w, ml_conv_b, ml_b_i, ml_b_f, ml_norm_g, gla_w_a, gla_b_a, gla_norm_g,
           dsa_q_g, dsa_k_g, w_out, rel_bias):
    b, s, d = x.shape
    assert d == D_MODEL and s % TILE == 0
    depth = norm_g.shape[0]
    bias_tiles = _bias_tiles(rel_bias)
    x2d = x.reshape(b * s, d)
    for layer in range(depth):
        x2d = _layer(x2d, b, s, norm_g[layer], w_in[layer], ml_conv_w[layer], ml_conv_b[layer],
                     ml_b_i[layer], ml_b_f[layer], ml_norm_g[layer], gla_w_a[layer], gla_b_a[layer],
                     gla_norm_g[layer], dsa_q_g[layer], dsa_k_g[layer], w_out[layer], bias_tiles)
    return x2d.reshape(b, s, d)
```
